```python
import jax, jax.numpy as jnp
from jax import lax
import numpy as np

D_MODEL = 2048
BATCH = 1
SEQ = 8192
DEPTH = 2

HEAD_DIM = 128
N_HEADS_TOTAL = D_MODEL // HEAD_DIM
N_MEM_HEADS = N_HEADS_TOTAL // 4
N_MIX_HEADS = N_HEADS_TOTAL - N_MEM_HEADS
N_KV_GROUPS = 3
HEADS_PER_GROUP = N_MIX_HEADS // N_KV_GROUPS
N_MEM_TOKENS = 256
CMP_STRIDE = 16
CMP_BLOCK = 2 * CMP_STRIDE
SLC_BLOCK = 64
SLC_TOPK = 16
WINDOW = 512
Q_BLOCK = 128
D_FF = 7 * D_MODEL // 2
N_EXPERTS = 8
TOP_K = 2
N_A_LAYERS = (DEPTH + 1) // 2
N_B_LAYERS = DEPTH - N_A_LAYERS
N_DENSE = (DEPTH + 1) // 2
N_MOE = DEPTH // 2
Q_W = N_MIX_HEADS * HEAD_DIM
KV_W = N_KV_GROUPS * HEAD_DIM
GATE_W = 3 * N_MIX_HEADS
MEM_Q_W = N_MEM_HEADS * HEAD_DIM
A_IN_W = Q_W + 6 * KV_W + GATE_W + MEM_Q_W
B_IN_W = Q_W + MEM_Q_W
NEG = -1e30
BIG = 1e6
EPS = 1e-6

kernel_name = 'hybrid_nsa_stickbreaking_yoco'


def _rmsnorm(x, g):
    xf = x.astype(jnp.float32)
    y = xf * lax.rsqrt(jnp.mean(xf * xf, axis=-1, keepdims=True) + EPS)
    return (y * g.astype(jnp.float32)).astype(x.dtype)


def _heads(x, n):
    b, t, _ = x.shape
    return x.reshape(b, t, n, HEAD_DIM).transpose(0, 2, 1, 3)


def _merge(x):
    b, h, t, d = x.shape
    return x.transpose(0, 2, 1, 3).reshape(b, t, h * d)


def _alibi_slopes(n):
    return jnp.asarray(np.array([2.0 ** (-8.0 * (i + 1) / n) for i in range(n)], dtype=np.float32))


def _masked_softmax(s, mask):
    p = jax.nn.softmax(jnp.where(mask, s, NEG), axis=-1)
    return p * mask.astype(p.dtype)


def _swiglu(h, w_gu, w_down):
    g, u = jnp.split(h @ w_gu, 2, axis=-1)
    return (jax.nn.silu(g) * u) @ w_down


def _compress(kv, w, pos):
    b, g, t, d = kv.shape
    ch = kv.reshape(b, g, t // CMP_STRIDE, CMP_STRIDE, d)
    blocks = jnp.concatenate([ch[:, :, :-1], ch[:, :, 1:]], axis=3) + pos
    return blocks.reshape(b, g, t // CMP_STRIDE - 1, CMP_BLOCK * d) @ w


def _nsa(q, kc, vc, ks, vs, kw, vw, gates, q_gain, k_gain, cmp_w, cmp_pos):
    b, h, t, d = q.shape
    g, hp = N_KV_GROUPS, HEADS_PER_GROUP
    scale = d ** -0.5
    slopes = _alibi_slopes(h).reshape(g, hp)
    q = _rmsnorm(q, q_gain).reshape(b, g, hp, t, d)
    pos = jnp.arange(t)

    k_cmp = _rmsnorm(_compress(kc, cmp_w[0], cmp_pos[0]), k_gain[0])
    v_cmp = _compress(vc, cmp_w[1], cmp_pos[1])
    nc = k_cmp.shape[2]
    c_end = jnp.arange(nc) * CMP_STRIDE + (CMP_BLOCK - 1)
    dist = (pos[:, None] - c_end[None, :]).astype(jnp.float32)
    s = jnp.einsum('bgptd,bgnd->bgptn', q, k_cmp).astype(jnp.float32) * scale - slopes[:, :, None, None] * dist
    p_cmp = _masked_softmax(s, dist >= 0)
    o_cmp = jnp.einsum('bgptn,bgnd->bgptd', p_cmp.astype(v_cmp.dtype), v_cmp)

    nb = t // SLC_BLOCK
    ratio = SLC_BLOCK // CMP_STRIDE
    imp_c = jnp.pad(p_cmp.sum(axis=2), ((0, 0), (0, 0), (0, 0), (1, 1)))
    imp = imp_c[..., :ratio * nb].reshape(b, g, t, nb, ratio).sum(-1) + imp_c[..., ratio::ratio]
    blk = jnp.arange(nb)[None, :]
    cur = (pos // SLC_BLOCK)[:, None]
    score = jnp.where(blk == cur, 2 * BIG,
                      jnp.where(blk == 0, BIG,
                                jnp.where(blk * SLC_BLOCK <= pos[:, None], imp, -BIG)))
    n_sel = min(SLC_TOPK, nb)
    _, idx = lax.top_k(score, n_sel)

    k_sb = _rmsnorm(ks, k_gain[1]).reshape(b, g, nb, SLC_BLOCK, d)
    v_sb = vs.reshape(b, g, nb, SLC_BLOCK, d)
    nq = t // Q_BLOCK
    gather = jax.vmap(jax.vmap(lambda kb, ib: kb[ib]))

    def slc_step(args):
        qi, ii, ti = args
        kg = gather(k_sb, ii)
        vg = gather(v_sb, ii)
        kpos = ii[..., None] * SLC_BLOCK + jnp.arange(SLC_BLOCK)
        dd = (ti[:, None, None] - kpos).astype(jnp.float32)
        ss = jnp.einsum('bgpqd,bgqkld->bgpqkl', qi, kg).astype(jnp.float32) * scale \
            - slopes[:, :, None, None, None] * dd[:, :, None]
        pp = _masked_softmax(ss.reshape(b, g, hp, Q_BLOCK, -1), (dd >= 0).reshape(b, g, 1, Q_BLOCK, -1))
        pp = pp.reshape(b, g, hp, Q_BLOCK, n_sel, SLC_BLOCK)
        return jnp.einsum('bgpqkl,bgqkld->bgpqd', pp.astype(vg.dtype), vg)

    q_blocks = q.reshape(b, g, hp, nq, Q_BLOCK, d).transpose(3, 0, 1, 2, 4, 5)
    i_blocks = idx.reshape(b, g, nq, Q_BLOCK, n_sel).transpose(2, 0, 1, 3, 4)
    t_blocks = pos.reshape(nq, Q_BLOCK)
    o_slc = lax.map(slc_step, (q_blocks, i_blocks, t_blocks)).transpose(1, 2, 3, 0, 4, 5).reshape(b, g, hp, t, d)

    nw = WINDOW // Q_BLOCK

    def band(z):
        zp = jnp.pad(z, ((0, 0), (0, 0), (WINDOW, 0), (0, 0))).reshape(b, g, nq + nw, Q_BLOCK, d)
        return jnp.concatenate([zp[:, :, j:j + nq] for j in range(nw + 1)], axis=3)

    kband = band(_rmsnorm(kw, k_gain[2]))
    vband = band(vw)
    tq = pos.reshape(nq, Q_BLOCK, 1)
    tk = ((jnp.arange(nq) - nw) * Q_BLOCK)[:, None, None] + jnp.arange((nw + 1) * Q_BLOCK)[None, None, :]
    dw = tq - tk
    wmask = (dw >= 0) & (dw < WINDOW) & (tk >= 0)
    qw = q.reshape(b, g, hp, nq, Q_BLOCK, d)
    sw = jnp.einsum('bgpnqd,bgnkd->bgpnqk', qw, kband).astype(jnp.float32) * scale \
        - slopes[:, :, None, None, None] * dw.astype(jnp.float32)
    pw = _masked_softmax(sw, wmask)
    o_win = jnp.einsum('bgpnqk,bgnkd->bgpnqd', pw.astype(vband.dtype), vband).reshape(b, g, hp, t, d)

    gt = jax.nn.sigmoid(gates.astype(jnp.float32)).reshape(b, t, h, 3).transpose(0, 2, 1, 3)
    o = (o_cmp.reshape(b, h, t, d) * gt[..., 0:1]
         + o_slc.reshape(b, h, t, d) * gt[..., 1:2]
         + o_win.reshape(b, h, t, d) * gt[..., 2:3])
    return o.astype(q.dtype)


def _stick_breaking(q, k, v):
    b, h, t, d = q.shape
    scale = d ** -0.5
    nq = t // Q_BLOCK
    kpos = jnp.arange(t)

    def step(args):
        qi, ti = args
        z = jnp.einsum('bhqd,bhsd->bhqs', qi, k).astype(jnp.float32) * scale
        causal = kpos[None, :] < ti[:, None]
        log_keep = jnp.where(causal, jax.nn.log_sigmoid(-z), 0.0)
        after = lax.cumsum(log_keep, axis=3, reverse=True) - log_keep
        a = jnp.where(causal, jnp.exp(jax.nn.log_sigmoid(z) + after), 0.0)
        return jnp.einsum('bhqs,bhsd->bhqd', a.astype(v.dtype), v)

    qb = q.reshape(b, h, nq, Q_BLOCK, d).transpose(2, 0, 1, 3, 4)
    tb = jnp.arange(t).reshape(nq, Q_BLOCK)
    return lax.map(step, (qb, tb)).transpose(1, 2, 0, 3, 4).reshape(b, h, t, d)


def _mem_attn(qm, mem_n, w_kv, qk_gain):
    k, v = jnp.split(mem_n @ w_kv, 2, axis=-1)
    q = _rmsnorm(qm, qk_gain[0])
    k = _rmsnorm(_heads(k, N_MEM_HEADS), qk_gain[1])
    v = _heads(v, N_MEM_HEADS)
    s = jnp.einsum('bhtd,bhmd->bhtm', q, k).astype(jnp.float32) * HEAD_DIM ** -0.5
    p = jax.nn.softmax(s, axis=-1)
    return jnp.einsum('bhtm,bhmd->bhtd', p.astype(v.dtype), v)


def _moe(h, router_w, router_b, w_gu, w_down):
    logits = (h @ router_w).astype(jnp.float32)
    _, top_i = lax.top_k(logits + router_b.astype(jnp.float32), TOP_K)
    w = jax.nn.softmax(jnp.take_along_axis(logits, top_i, axis=-1), axis=-1)
    combine = jnp.sum(jax.nn.one_hot(top_i, N_EXPERTS, dtype=jnp.float32) * w[..., None], axis=-2)
    out = jnp.zeros_like(h)
    for e in range(N_EXPERTS):
        out = out + (combine[..., e:e + 1] * _swiglu(h, w_gu[e], w_down[e])).astype(h.dtype)
    return out


def setup_inputs(seed: int = 0) -> dict:
    key = jax.random.key(seed)
    ks = jax.random.split(key, 24)

    def nrm(k, shape, fan_in):
        return jax.random.normal(k, shape, jnp.float32) * (fan_in ** -0.5)

    def gain(k, shape):
        return 1.0 + 0.02 * jax.random.normal(k, shape, jnp.float32)

    return {
        'x': jax.random.normal(ks[0], (BATCH, SEQ, D_MODEL), jnp.float32),
        'mem': jax.random.normal(ks[1], (BATCH, N_MEM_TOKENS, D_MODEL), jnp.float32),
        'attn_norm': gain(ks[2], (DEPTH, D_MODEL)),
        'mem_norm': gain(ks[3], (DEPTH, D_MODEL)),
        'ffn_norm': gain(ks[4], (DEPTH, D_MODEL)),
        'w_in_a': nrm(ks[5], (N_A_LAYERS, D_MODEL, A_IN_W), D_MODEL),
        'cmp_w': nrm(ks[6], (N_A_LAYERS, 2, CMP_BLOCK * HEAD_DIM, HEAD_DIM), CMP_BLOCK * HEAD_DIM),
        'cmp_pos': 0.1 * jax.random.normal(ks[7], (N_A_LAYERS, 2, CMP_BLOCK, HEAD_DIM), jnp.float32),
        'nsa_q_norm': gain(ks[8], (N_A_LAYERS, HEAD_DIM)),
        'nsa_k_norm': gain(ks[9], (N_A_LAYERS, 3, HEAD_DIM)),
        'w_in_b': nrm(ks[10], (N_B_LAYERS, D_MODEL, B_IN_W), D_MODEL),
        'kv_norm_b': gain(ks[11], (D_MODEL,)),
        'w_kv_b': nrm(ks[12], (D_MODEL, 2 * Q_W), D_MODEL),
        'w_mem_kv': nrm(ks[13], (DEPTH, D_MODEL, 2 * MEM_Q_W), D_MODEL),
        'mem_qk_norm': gain(ks[14], (DEPTH, 2, HEAD_DIM)),
        'w_out': nrm(ks[15], (DEPTH, N_HEADS_TOTAL * HEAD_DIM, D_MODEL), N_HEADS_TOTAL * HEAD_DIM),
        'ffn_w_gu': nrm(ks[16], (N_DENSE, D_MODEL, 2 * D_FF), D_MODEL),
        'ffn_w_down': nrm(ks[17], (N_DENSE, D_FF, D_MODEL), D_FF),
        'router_w': nrm(ks[18], (N_MOE, D_MODEL, N_EXPERTS), D_MODEL),
        'router_b': 0.01 * jax.random.normal(ks[19], (N_MOE, N_EXPERTS), jnp.float32),
        'moe_w_gu': nrm(ks[20], (N_MOE, N_EXPERTS, D_MODEL, 2 * D_FF), D_MODEL),
        'moe_w_down': nrm(ks[21], (N_MOE, N_EXPERTS, D_FF, D_MODEL), D_FF),
    }


def reference(x, mem, attn_norm, mem_norm, ffn_norm, w_in_a, cmp_w, cmp_pos, nsa_q_norm, nsa_k_norm,
              w_in_b, kv_norm_b, w_kv_b, w_mem_kv, mem_qk_norm, w_out, ffn_w_gu, ffn_w_down,
              router_w, router_b, moe_w_gu, moe_w_down):
    for l in range(DEPTH):
        h = _rmsnorm(x, attn_norm[l])
        mem_n = _rmsnorm(mem, mem_norm[l])
        if l < N_A_LAYERS:
            q, kv, gts, qm = jnp.split(h @ w_in_a[l], [Q_W, Q_W + 6 * KV_W, Q_W + 6 * KV_W + GATE_W], axis=-1)
            kc, vc, ksl, vsl, kw, vw = [_heads(z, N_KV_GROUPS) for z in jnp.split(kv, 6, axis=-1)]
            mix = _nsa(_heads(q, N_MIX_HEADS), kc, vc, ksl, vsl, kw, vw, gts,
                       nsa_q_norm[l], nsa_k_norm[l], cmp_w[l], cmp_pos[l])
        else:
            if l == N_A_LAYERS:
                k_s, v_s = jnp.split(_rmsnorm(x, kv_norm_b) @ w_kv_b, 2, axis=-1)
                k_shared = _heads(k_s, N_MIX_HEADS)
                v_shared = _heads(v_s, N_MIX_HEADS)
            q, qm = jnp.split(h @ w_in_b[l - N_A_LAYERS], [Q_W], axis=-1)
            mix = _stick_breaking(_heads(q, N_MIX_HEADS), k_shared, v_shared)
        mo = _mem_attn(_heads(qm, N_MEM_HEADS), mem_n, w_mem_kv[l], mem_qk_norm[l])
        x = x + _merge(jnp.concatenate([mix, mo], axis=1)) @ w_out[l]
        h2 = _rmsnorm(x, ffn_norm[l])
        if l % 2 == 0:
            x = x + _swiglu(h2, ffn_w_gu[l // 2], ffn_w_down[l // 2])
        else:
            x = x + _moe(h2, router_w[l // 2], router_b[l // 2], moe_w_gu[l // 2], moe_w_down[l // 2])
    return x
```

```python
import functools

import numpy as np
import jax
import jax.numpy as jnp
from jax import lax
from jax.experimental import pallas as pl
from jax.experimental.pallas import tpu as pltpu

HEAD_DIM = 128
N_MEM_HEADS = 4
N_MIX_HEADS = 12
N_KV_GROUPS = 3
HEADS_PER_GROUP = 4
CMP_STRIDE = 16
CMP_BLOCK = 32
SLC_BLOCK = 64
SLC_TOPK = 16
WINDOW = 512
N_EXPERTS = 8
Q_W = N_MIX_HEADS * HEAD_DIM
KV_W = N_KV_GROUPS * HEAD_DIM
GATE_W = 3 * N_MIX_HEADS
MEM_Q_W = N_MEM_HEADS * HEAD_DIM
NEG = -1e30
BIG = 1e6
EPS = 1e-6
SCALE = HEAD_DIM ** -0.5
LANES = 128
VMEM_LIMIT = 48 * 1024 * 1024

F32 = jnp.float32
BF16 = jnp.bfloat16


def _cparams(sem):
    return pltpu.CompilerParams(dimension_semantics=sem, vmem_limit_bytes=VMEM_LIMIT)


def _nt_dot(a, b):
    return lax.dot_general(a, b, (((1,), (1,)), ((), ())), preferred_element_type=F32)


def _rms(x, gain):
    return x * lax.rsqrt(jnp.mean(x * x, axis=-1, keepdims=True) + EPS) * gain


def _rmsnorm_cast_kernel(x_ref, g_ref, o_ref):
    o_ref[...] = _rms(x_ref[...], g_ref[...]).astype(o_ref.dtype)


def _rmsnorm_cast(x, gain, tm=512):
    m, d = x.shape
    tm = min(tm, m)
    return pl.pallas_call(
        _rmsnorm_cast_kernel,
        out_shape=jax.ShapeDtypeStruct((m, d), BF16),
        grid=(m // tm,),
        in_specs=[pl.BlockSpec((tm, d), lambda i: (i, 0)), pl.BlockSpec((1, d), lambda i: (0, 0))],
        out_specs=pl.BlockSpec((tm, d), lambda i: (i, 0)),
        compiler_params=_cparams(("parallel",)),
        name="rmsnorm_cast",
    )(x, gain.reshape(1, d))


def _matmul_kernel(*refs, nk, has_res):
    if has_res:
        a_ref, w_ref, r_ref, o_ref = refs[:4]
    else:
        a_ref, w_ref, o_ref = refs[:3]
        r_ref = None
    part = jnp.dot(a_ref[...], w_ref[...].astype(BF16), preferred_element_type=F32)
    if nk == 1:
        if has_res:
            part = part + r_ref[...]
        o_ref[...] = part.astype(o_ref.dtype)
        return
    acc_ref = refs[-1]
    k = pl.program_id(2)

    @pl.when(k == 0)
    def _():
        acc_ref[...] = part

    @pl.when(k > 0)
    def _():
        acc_ref[...] += part

    @pl.when(k == nk - 1)
    def _():
        r = acc_ref[...]
        if has_res:
            r = r + r_ref[...]
        o_ref[...] = r.astype(o_ref.dtype)


def _matmul(a, w, res=None, *, tm, tn, tk=None, out_dtype=F32, name="matmul"):
    m, kdim = a.shape
    n = w.shape[1]
    tm = min(tm, m)
    tk = kdim if tk is None else tk
    nk = kdim // tk
    in_specs = [pl.BlockSpec((tm, tk), lambda i, j, k: (i, k)),
                pl.BlockSpec((tk, tn), lambda i, j, k: (k, j))]
    args = [a, w]
    if res is not None:
        in_specs.append(pl.BlockSpec((tm, tn), lambda i, j, k: (i, j)))
        args.append(res)
    scratch = [pltpu.VMEM((tm, tn), F32)] if nk > 1 else []
    return pl.pallas_call(
        functools.partial(_matmul_kernel, nk=nk, has_res=res is not None),
        out_shape=jax.ShapeDtypeStruct((m, n), out_dtype),
        grid=(m // tm, pl.cdiv(n, tn), nk),
        in_specs=in_specs,
        out_specs=pl.BlockSpec((tm, tn), lambda i, j, k: (i, j)),
        scratch_shapes=scratch,
        compiler_params=_cparams(("parallel", "parallel", "arbitrary")),
        name=name,
    )(*args)


def _swiglu_kernel(*refs, scaled):
    if scaled:
        a_ref, wg_ref, wu_ref, c_ref, o_ref = refs
    else:
        a_ref, wg_ref, wu_ref, o_ref = refs
    a = a_ref[...]
    g = jnp.dot(a, wg_ref[0].astype(BF16), preferred_element_type=F32)
    u = jnp.dot(a, wu_ref[0].astype(BF16), preferred_element_type=F32)
    act = g * (1.0 / (1.0 + jnp.exp(-g))) * u
    if scaled:
        e = pl.program_id(1)
        c = c_ref[...]
        lane = lax.broadcasted_iota(jnp.int32, c.shape, 1)
        act = act * jnp.sum(jnp.where(lane == e, c, 0.0), axis=1, keepdims=True)
    o_ref[...] = act.astype(o_ref.dtype)


def _swiglu_up(a, w_gu, combine=None, *, tm=1024, tn=256):
    m, d = a.shape
    ne, _, f2 = w_gu.shape
    f = f2 // 2
    tm = min(tm, m)
    nj = f // tn
    in_specs = [pl.BlockSpec((tm, d), lambda i, e, j: (i, 0)),
                pl.BlockSpec((1, d, tn), lambda i, e, j: (e, 0, j)),
                pl.BlockSpec((1, d, tn), lambda i, e, j: (e, 0, j + nj))]
    args = [a, w_gu, w_gu]
    if combine is not None:
        in_specs.append(pl.BlockSpec((tm, LANES), lambda i, e, j: (i, 0)))
        args.append(combine)
    return pl.pallas_call(
        functools.partial(_swiglu_kernel, scaled=combine is not None),
        out_shape=jax.ShapeDtypeStruct((m, ne * f), BF16),
        grid=(m // tm, ne, nj),
        in_specs=in_specs,
        out_specs=pl.BlockSpec((tm, tn), lambda i, e, j: (i, e * nj + j)),
        compiler_params=_cparams(("parallel", "parallel", "parallel")),
        name="swiglu_up",
    )(*args)


def _nsa_prep_kernel(p_ref, gtab_ref, o_ref):
    for c in range((Q_W + 6 * KV_W) // HEAD_DIM):
        x = p_ref[:, c * HEAD_DIM:(c + 1) * HEAD_DIM]
        if c < N_MIX_HEADS:
            y = _rms(x, gtab_ref[0:1, :]) * SCALE
        elif 18 <= c < 21:
            y = _rms(x, gtab_ref[1:2, :])
        elif 24 <= c < 27:
            y = _rms(x, gtab_ref[2:3, :])
        else:
            y = x
        o_ref[:, c * HEAD_DIM:(c + 1) * HEAD_DIM] = y.astype(o_ref.dtype)


def _nsa_prep(p, q_gain, k_gain, tm=256):
    t = p.shape[0]
    tm = min(tm, t)
    w = Q_W + 6 * KV_W
    gtab = jnp.zeros((8, HEAD_DIM), F32).at[0].set(q_gain).at[1].set(k_gain[1]).at[2].set(k_gain[2])
    return pl.pallas_call(
        _nsa_prep_kernel,
        out_shape=jax.ShapeDtypeStruct((t, w), BF16),
        grid=(t // tm,),
        in_specs=[pl.BlockSpec((tm, w), lambda i: (i, 0)), pl.BlockSpec((8, HEAD_DIM), lambda i: (0, 0))],
        out_specs=pl.BlockSpec((tm, w), lambda i: (i, 0)),
        compiler_params=_cparams(("parallel",)),
        name="nsa_prep",
    )(p, gtab)


def _compress_kernel(b_ref, pos_ref, w_ref, g_ref, o_ref):
    blk = (b_ref[0, 0] + pos_ref[0]).astype(BF16)
    y = jnp.dot(blk, w_ref[0].astype(BF16), preferred_element_type=F32)
    is_k = pl.program_id(0) == 0
    o_ref[0, 0] = jnp.where(is_k, _rms(y, g_ref[...]), y).astype(o_ref.dtype)


def _compress(blocks, cmp_w, cmp_pos, k_gain0):
    _, g, ncp, bw = blocks.shape
    return pl.pallas_call(
        _compress_kernel,
        out_shape=jax.ShapeDtypeStruct((2, g, ncp, HEAD_DIM), BF16),
        grid=(2, g),
        in_specs=[pl.BlockSpec((1, 1, ncp, bw), lambda s, gi: (s, gi, 0, 0)),
                  pl.BlockSpec((1, 1, bw), lambda s, gi: (s, 0, 0)),
                  pl.BlockSpec((1, bw, HEAD_DIM), lambda s, gi: (s, 0, 0)),
                  pl.BlockSpec((1, HEAD_DIM), lambda s, gi: (0, 0))],
        out_specs=pl.BlockSpec((1, 1, ncp, HEAD_DIM), lambda s, gi: (s, gi, 0, 0)),
        compiler_params=_cparams(("parallel", "parallel")),
        name="nsa_compress",
    )(blocks, cmp_pos.reshape(2, 1, bw), cmp_w, k_gain0.reshape(1, HEAD_DIM))


def _cmp_select_kernel(slopes_ref, q_ref, k_ref, v_ref, imap_ref, o_ref, sel_ref, *, tq, ncp):
    g = pl.program_id(0)
    t0 = pl.program_id(1) * tq
    row = lax.broadcasted_iota(jnp.int32, (tq, ncp), 0) + t0
    col = lax.broadcasted_iota(jnp.int32, (tq, ncp), 1)
    dist_i = row - (col * CMP_STRIDE + (CMP_BLOCK - 1))
    mask = dist_i >= 0
    dist = dist_i.astype(F32)
    k = k_ref[0, 0]
    v = v_ref[0, 0]
    psum = jnp.zeros((tq, ncp), F32)
    for p in range(HEADS_PER_GROUP):
        slope = slopes_ref[g * HEADS_PER_GROUP + p]
        q = q_ref[:, p * HEAD_DIM:(p + 1) * HEAD_DIM]
        s = jnp.where(mask, _nt_dot(q, k) - slope * dist, NEG)
        m = jnp.max(s, axis=1, keepdims=True)
        e = jnp.where(mask, jnp.exp(s - m), 0.0)
        l = jnp.sum(e, axis=1, keepdims=True)
        pr = e * (1.0 / jnp.maximum(l, 1e-30))
        psum = psum + pr
        o_ref[:, p * HEAD_DIM:(p + 1) * HEAD_DIM] = jnp.dot(
            pr.astype(BF16), v, preferred_element_type=F32).astype(o_ref.dtype)

    imap = imap_ref[...]
    hi = psum.astype(BF16)
    r1 = psum - hi.astype(F32)
    mid = r1.astype(BF16)
    lo = (r1 - mid.astype(F32)).astype(BF16)
    imp = (jnp.dot(hi, imap, preferred_element_type=F32)
           + jnp.dot(mid, imap, preferred_element_type=F32)
           + jnp.dot(lo, imap, preferred_element_type=F32))

    nb = imp.shape[1]
    t = lax.broadcasted_iota(jnp.int32, (tq, nb), 0) + t0
    blk = lax.broadcasted_iota(jnp.int32, (tq, nb), 1)
    score = jnp.where(blk == t // SLC_BLOCK, 2 * BIG,
                      jnp.where(blk == 0, BIG,
                                jnp.where(blk * SLC_BLOCK <= t, imp, -BIG)))
    blk_f = blk.astype(F32)

    def pick_one(_, carry):
        work, sel = carry
        m = jnp.max(work, axis=1, keepdims=True)
        first = jnp.min(jnp.where(work == m, blk_f, float(nb)), axis=1, keepdims=True)
        pick = blk_f == first
        return jnp.where(pick, -3e38, work), jnp.where(pick, 1.0, sel)

    _, sel = lax.fori_loop(0, min(SLC_TOPK, nb), pick_one, (score, jnp.zeros((tq, nb), F32)))
    sel_ref[0] = sel.astype(sel_ref.dtype)


def _cmp_select(slopes, qkv, kvc, imap, tq=256):
    t = qkv.shape[0]
    tq = min(tq, t)
    ncp = kvc.shape[2]
    nb = t // SLC_BLOCK
    gw = HEADS_PER_GROUP * HEAD_DIM
    return pl.pallas_call(
        functools.partial(_cmp_select_kernel, tq=tq, ncp=ncp),
        out_shape=(jax.ShapeDtypeStruct((t, Q_W), F32),
                   jax.ShapeDtypeStruct((N_KV_GROUPS, t, nb), BF16)),
        grid_spec=pltpu.PrefetchScalarGridSpec(
            num_scalar_prefetch=1,
            grid=(N_KV_GROUPS, t // tq),
            in_specs=[pl.BlockSpec((tq, gw), lambda g, i, s: (i, g)),
                      pl.BlockSpec((1, 1, ncp, HEAD_DIM), lambda g, i, s: (0, g, 0, 0)),
                      pl.BlockSpec((1, 1, ncp, HEAD_DIM), lambda g, i, s: (1, g, 0, 0)),
                      pl.BlockSpec((ncp, nb), lambda g, i, s: (0, 0))],
            out_specs=(pl.BlockSpec((tq, gw), lambda g, i, s: (i, g)),
                       pl.BlockSpec((1, tq, nb), lambda g, i, s: (g, i, 0)))),
        compiler_params=_cparams(("parallel", "parallel")),
        name="nsa_cmp_select",
    )(slopes, qkv, kvc, kvc, imap)


def _slc_kernel(slopes_ref, q_ref, k_ref, v_ref, sel_ref, o_ref, m_ref, l_ref, acc_ref, *, tq, tk):
    g = pl.program_id(0)
    i = pl.program_id(1)
    t0 = i * tq
    nb = sel_ref.shape[2]
    m_ref[...] = jnp.full(m_ref.shape, NEG, F32)
    l_ref[...] = jnp.zeros(l_ref.shape, F32)
    acc_ref[...] = jnp.zeros(acc_ref.shape, F32)
    sel = sel_ref[0]
    bpt = tk // SLC_BLOCK

    def body(j, carry):
        k0 = pl.multiple_of(j * tk, tk)
        k = k_ref[pl.ds(k0, tk), :]
        v = v_ref[pl.ds(k0, tk), :]
        eb = lax.broadcasted_iota(jnp.int32, (nb, tk), 0)
        ec = lax.broadcasted_iota(jnp.int32, (nb, tk), 1)
        expand = jnp.where(eb == ec // SLC_BLOCK + j * bpt, 1.0, 0.0).astype(BF16)
        selx = jnp.dot(sel, expand, preferred_element_type=F32)
        rel = (t0 - k0) + lax.broadcasted_iota(jnp.int32, (tq, tk), 0) \
            - lax.broadcasted_iota(jnp.int32, (tq, tk), 1)
        mask = jnp.where(rel >= 0, selx, 0.0) > 0.5
        colb = (lax.broadcasted_iota(jnp.int32, (1, tk), 1) + (k0 - t0)).astype(F32)
        for p in range(HEADS_PER_GROUP):
            slope = slopes_ref[g * HEADS_PER_GROUP + p]
            q = q_ref[:, p * HEAD_DIM:(p + 1) * HEAD_DIM]
            s = jnp.where(mask, _nt_dot(q, k) + slope * colb, NEG)
            m_old = m_ref[p]
            m_new = jnp.maximum(m_old, jnp.max(s, axis=1, keepdims=True))
            alpha = jnp.exp(m_old - m_new)
            e = jnp.where(mask, jnp.exp(s - m_new), 0.0)
            l_ref[p] = alpha * l_ref[p] + jnp.sum(e, axis=1, keepdims=True)
            acc_ref[p] = alpha * acc_ref[p] + jnp.dot(e.astype(BF16), v, preferred_element_type=F32)
            m_ref[p] = m_new
        return carry

    n_tiles = (t0 + tq - 1) // tk + 1
    lax.fori_loop(0, n_tiles, body, 0)
    for p in range(HEADS_PER_GROUP):
        o_ref[:, p * HEAD_DIM:(p + 1) * HEAD_DIM] = (
            acc_ref[p] * (1.0 / jnp.maximum(l_ref[p], 1e-30))).astype(o_ref.dtype)


def _slc_attn(slopes, qkv, sel, tq=256, tk=512):
    t = qkv.shape[0]
    tq = min(tq, t)
    tk = min(tk, t)
    nb = sel.shape[2]
    gw = HEADS_PER_GROUP * HEAD_DIM
    ks_col = (Q_W + 2 * KV_W) // HEAD_DIM
    vs_col = (Q_W + 3 * KV_W) // HEAD_DIM
    return pl.pallas_call(
        functools.partial(_slc_kernel, tq=tq, tk=tk),
        out_shape=jax.ShapeDtypeStruct((t, Q_W), F32),
        grid_spec=pltpu.PrefetchScalarGridSpec(
            num_scalar_prefetch=1,
            grid=(N_KV_GROUPS, t // tq),
            in_specs=[pl.BlockSpec((tq, gw), lambda g, i, s: (i, g)),
                      pl.BlockSpec((t, HEAD_DIM), lambda g, i, s: (0, ks_col + g)),
                      pl.BlockSpec((t, HEAD_DIM), lambda g, i, s: (0, vs_col + g)),
                      pl.BlockSpec((1, tq, nb), lambda g, i, s: (g, i, 0))],
            out_specs=pl.BlockSpec((tq, gw), lambda g, i, s: (i, g)),
            scratch_shapes=[pltpu.VMEM((HEADS_PER_GROUP, tq, 1), F32),
                            pltpu.VMEM((HEADS_PER_GROUP, tq, 1), F32),
                            pltpu.VMEM((HEADS_PER_GROUP, tq, HEAD_DIM), F32)]),
        compiler_params=_cparams(("parallel", "parallel")),
        name="nsa_selected",
    )(slopes, qkv, qkv, qkv, sel)


def _win_kernel(slopes_ref, q_ref, k0_ref, k1_ref, k2_ref, v0_ref, v1_ref, v2_ref, o_ref, *, tq):
    g = pl.program_id(0)
    i = pl.program_id(1)
    nprev = WINDOW // tq
    kw = (nprev + 1) * tq
    k = jnp.concatenate([k0_ref[...], k1_ref[...], k2_ref[...]], axis=0)
    v = jnp.concatenate([v0_ref[...], v1_ref[...], v2_ref[...]], axis=0)
    r = lax.broadcasted_iota(jnp.int32, (tq, kw), 0)
    c = lax.broadcasted_iota(jnp.int32, (tq, kw), 1)
    rel = WINDOW + r - c
    first_valid = (nprev - i) * tq
    mask = (rel >= 0) & (rel < WINDOW) & (c >= first_valid)
    rel_f = rel.astype(F32)
    for p in range(HEADS_PER_GROUP):
        slope = slopes_ref[g * HEADS_PER_GROUP + p]
        q = q_ref[:, p * HEAD_DIM:(p + 1) * HEAD_DIM]
        s = jnp.where(mask, _nt_dot(q, k) - slope * rel_f, NEG)
        m = jnp.max(s, axis=1, keepdims=True)
        e = jnp.where(mask, jnp.exp(s - m), 0.0)
        l = jnp.sum(e, axis=1, keepdims=True)
        o = jnp.dot(e.astype(BF16), v, preferred_element_type=F32) * (1.0 / jnp.maximum(l, 1e-30))
        o_ref[:, p * HEAD_DIM:(p + 1) * HEAD_DIM] = o.astype(o_ref.dtype)


def _win_attn(slopes, qkv, tq=256):
    t = qkv.shape[0]
    assert WINDOW % tq == 0 and WINDOW // tq == 2
    gw = HEADS_PER_GROUP * HEAD_DIM
    kw_col = (Q_W + 4 * KV_W) // HEAD_DIM
    vw_col = (Q_W + 5 * KV_W) // HEAD_DIM

    def kv_spec(col, back):
        return pl.BlockSpec((tq, HEAD_DIM), lambda g, i, s: (jnp.maximum(i - back, 0), col + g))

    return pl.pallas_call(
        functools.partial(_win_kernel, tq=tq),
        out_shape=jax.ShapeDtypeStruct((t, Q_W), F32),
        grid_spec=pltpu.PrefetchScalarGridSpec(
            num_scalar_prefetch=1,
            grid=(N_KV_GROUPS, t // tq),
            in_specs=[pl.BlockSpec((tq, gw), lambda g, i, s: (i, g)),
                      kv_spec(kw_col, 2), kv_spec(kw_col, 1), kv_spec(kw_col, 0),
                      kv_spec(vw_col, 2), kv_spec(vw_col, 1), kv_spec(vw_col, 0)],
            out_specs=pl.BlockSpec((tq, gw), lambda g, i, s: (i, g))),
        compiler_params=_cparams(("parallel", "parallel")),
        name="nsa_window",
    )(slopes, qkv, qkv, qkv, qkv, qkv, qkv, qkv)


def _gate_sum_kernel(oc_ref, os_ref, ow_ref, g_ref, o_ref):
    sg = 1.0 / (1.0 + jnp.exp(-g_ref[...]))
    for h in range(N_MIX_HEADS):
        sl = slice(h * HEAD_DIM, (h + 1) * HEAD_DIM)
        o = (oc_ref[:, sl] * sg[:, 3 * h:3 * h + 1]
             + os_ref[:, sl] * sg[:, 3 * h + 1:3 * h + 2]
             + ow_ref[:, sl] * sg[:, 3 * h + 2:3 * h + 3])
        o_ref[:, sl] = o.astype(o_ref.dtype)


def _gate_sum(o_cmp, o_slc, o_win, gates, tm=512):
    t = o_cmp.shape[0]
    tm = min(tm, t)
    big = pl.BlockSpec((tm, Q_W), lambda i: (i, 0))
    return pl.pallas_call(
        _gate_sum_kernel,
        out_shape=jax.ShapeDtypeStruct((t, Q_W), BF16),
        grid=(t // tm,),
        in_specs=[big, big, big, pl.BlockSpec((tm, GATE_W), lambda i: (i, 0))],
        out_specs=big,
        compiler_params=_cparams(("parallel",)),
        name="nsa_gate_sum",
    )(o_cmp, o_slc, o_win, gates)


def _mem_attn_kernel(q_ref, k_ref, v_ref, g_ref, o_ref):
    q = (_rms(q_ref[...], g_ref[0:1, :]) * SCALE).astype(BF16)
    k = _rms(k_ref[...], g_ref[1:2, :]).astype(BF16)
    s = _nt_dot(q, k)
    e = jnp.exp(s - jnp.max(s, axis=1, keepdims=True))
    p = e * (1.0 / jnp.sum(e, axis=1, keepdims=True))
    o_ref[...] = jnp.dot(p.astype(BF16), v_ref[...].astype(BF16),
                         preferred_element_type=F32).astype(o_ref.dtype)


def _mem_attn(qsrc, q_col, memkv, qk_gain, tq=512):
    t = qsrc.shape[0]
    tq = min(tq, t)
    nm = memkv.shape[0]
    gtab = jnp.zeros((8, HEAD_DIM), F32).at[0:2].set(qk_gain)
    return pl.pallas_call(
        _mem_attn_kernel,
        out_shape=jax.ShapeDtypeStruct((t, MEM_Q_W), BF16),
        grid=(N_MEM_HEADS, t // tq),
        in_specs=[pl.BlockSpec((tq, HEAD_DIM), lambda h, i: (i, q_col + h)),
                  pl.BlockSpec((nm, HEAD_DIM), lambda h, i: (0, h)),
                  pl.BlockSpec((nm, HEAD_DIM), lambda h, i: (0, N_MEM_HEADS + h)),
                  pl.BlockSpec((8, HEAD_DIM), lambda h, i: (0, 0))],
        out_specs=pl.BlockSpec((tq, HEAD_DIM), lambda h, i: (i, h)),
        compiler_params=_cparams(("parallel", "parallel")),
        name="mem_attn",
    )(qsrc, memkv, memkv, gtab)


def _stick_kernel(q_ref, k_ref, v_ref, o_ref, *, tq):
    i = pl.program_id(1)
    t0 = i * tq
    q = (q_ref[...] * SCALE).astype(BF16)
    r = lax.broadcasted_iota(jnp.int32, (tq, tq), 0)
    c = lax.broadcasted_iota(jnp.int32, (tq, tq), 1)
    later = jnp.where(r > c, 1.0, 0.0).astype(BF16)

    def body(jj, carry):
        tail, acc = carry
        k0 = pl.multiple_of((i - jj) * tq, tq)
        k = k_ref[pl.ds(k0, tq), :]
        v = v_ref[pl.ds(k0, tq), :]
        z = _nt_dot(q, k)
        causal = (t0 - k0) + r - c > 0
        sp = jnp.maximum(z, 0.0) + jnp.log1p(jnp.exp(-jnp.abs(z)))
        log_keep = jnp.where(causal, -sp, 0.0)
        hi = log_keep.astype(BF16)
        lo = (log_keep - hi.astype(F32)).astype(BF16)
        after = (jnp.dot(hi, later, preferred_element_type=F32)
                 + jnp.dot(lo, later, preferred_element_type=F32) + tail)
        a = jnp.where(causal, jnp.exp(z - sp + after), 0.0)
        acc = acc + jnp.dot(a.astype(BF16), v, preferred_element_type=F32)
        return tail + jnp.sum(log_keep, axis=1, keepdims=True), acc

    _, acc = lax.fori_loop(0, i + 1, body, (jnp.zeros((tq, 1), F32), jnp.zeros((tq, HEAD_DIM), F32)))
    o_ref[...] = acc.astype(o_ref.dtype)


def _stick_attn(pb, kvb, tq=256):
    t = pb.shape[0]
    tq = min(tq, t)
    return pl.pallas_call(
        functools.partial(_stick_kernel, tq=tq),
        out_shape=jax.ShapeDtypeStruct((t, Q_W), BF16),
        grid=(N_MIX_HEADS, t // tq),
        in_specs=[pl.BlockSpec((tq, HEAD_DIM), lambda h, i: (i, h)),
                  pl.BlockSpec((t, HEAD_DIM), lambda h, i: (0, h)),
                  pl.BlockSpec((t, HEAD_DIM), lambda h, i: (0, N_MIX_HEADS + h))],
        out_specs=pl.BlockSpec((tq, HEAD_DIM), lambda h, i: (i, h)),
        compiler_params=_cparams(("parallel", "parallel")),
        name="stick_breaking",
    )(pb, kvb, kvb)


def _router_kernel(x_ref, g_ref, w_ref, b_ref, h_ref, c_ref):
    h = _rms(x_ref[...], g_ref[...])
    h_ref[...] = h.astype(h_ref.dtype)
    logits = jnp.dot(h, w_ref[...], preferred_element_type=F32, precision=lax.Precision.HIGHEST)
    lane = lax.broadcasted_iota(jnp.int32, logits.shape, 1).astype(F32)
    biased = jnp.where(lane < N_EXPERTS, logits + b_ref[...], -3e38)
    m1 = jnp.max(biased, axis=1, keepdims=True)
    i1 = jnp.min(jnp.where(biased == m1, lane, float(LANES)), axis=1, keepdims=True)
    rest = jnp.where(lane == i1, -3e38, biased)
    m2 = jnp.max(rest, axis=1, keepdims=True)
    i2 = jnp.min(jnp.where(rest == m2, lane, float(LANES)), axis=1, keepdims=True)
    l1 = jnp.sum(jnp.where(lane == i1, logits, 0.0), axis=1, keepdims=True)
    l2 = jnp.sum(jnp.where(lane == i2, logits, 0.0), axis=1, keepdims=True)
    mx = jnp.maximum(l1, l2)
    e1 = jnp.exp(l1 - mx)
    e2 = jnp.exp(l2 - mx)
    inv = 1.0 / (e1 + e2)
    c_ref[...] = jnp.where(lane == i1, e1 * inv, 0.0) + jnp.where(lane == i2, e2 * inv, 0.0)


def _router(x, gain, router_w, router_b, tm=512):
    t, d = x.shape
    tm = min(tm, t)
    w = jnp.zeros((d, LANES), F32).at[:, :N_EXPERTS].set(router_w)
    b = jnp.zeros((1, LANES), F32).at[0, :N_EXPERTS].set(router_b)
    return pl.pallas_call(
        _router_kernel,
        out_shape=(jax.ShapeDtypeStruct((t, d), BF16), jax.ShapeDtypeStruct((t, LANES), F32)),
        grid=(t // tm,),
        in_specs=[pl.BlockSpec((tm, d), lambda i: (i, 0)), pl.BlockSpec((1, d), lambda i: (0, 0)),
                  pl.BlockSpec((d, LANES), lambda i: (0, 0)), pl.BlockSpec((1, LANES), lambda i: (0, 0))],
        out_specs=(pl.BlockSpec((tm, d), lambda i: (i, 0)), pl.BlockSpec((tm, LANES), lambda i: (i, 0))),
        compiler_params=_cparams(("parallel",)),
        name="moe_router",
    )(x, gain.reshape(1, d), w, b)


def _moe_block(xs, gain, router_w, router_b, w_gu, w_down):
    ne, d_ff, d = w_down.shape
    h2, combine = _router(xs, gain, router_w, router_b)
    act = _swiglu_up(h2, w_gu, combine)
    return _matmul(act, w_down.reshape(ne * d_ff, d), xs, tm=1024, tn=1024, tk=512, name="moe_down")


def _alibi_slopes():
    return jnp.asarray(np.array([2.0 ** (-8.0 * (i + 1) / N_MIX_HEADS) for i in range(N_MIX_HEADS)],
                                dtype=np.float32))


def _importance_map(ncp, nb):
    ratio = SLC_BLOCK // CMP_STRIDE
    c = np.arange(ncp)[:, None]
    b = np.arange(nb)[None, :]
    m = (c >= ratio * b - 1) & (c <= ratio * b + ratio - 1) & (c < ncp - 1)
    return jnp.asarray(m.astype(np.float32), dtype=BF16)


def _compress_blocks(p, t):
    n = t // CMP_STRIDE
    z = p.reshape(n, CMP_STRIDE, 2, N_KV_GROUPS, HEAD_DIM).transpose(2, 3, 0, 1, 4)
    z = z.reshape(2, N_KV_GROUPS, n, CMP_STRIDE * HEAD_DIM)
    nxt = jnp.concatenate([z[:, :, 1:], jnp.zeros_like(z[:, :, :1])], axis=2)
    return jnp.concatenate([z, nxt], axis=3)


def kernel(x, mem, attn_norm, mem_norm, ffn_norm, w_in_a, cmp_w, cmp_pos, nsa_q_norm, nsa_k_norm, w_in_b,
           kv_norm_b, w_kv_b, w_mem_kv, mem_qk_norm, w_out, ffn_w_gu, ffn_w_down, router_w, router_b,
           moe_w_gu, moe_w_down):
    b, t, d = x.shape
    assert b == 1
    xs = x[0]
    mems = mem[0]
    slopes = _alibi_slopes()
    d_ff = ffn_w_down.shape[1]

    h = _rmsnorm_cast(xs, attn_norm[0])
    p = _matmul(h, w_in_a[0], tm=1024, tn=640, name="in_proj_a")
    qkv = _nsa_prep(p, nsa_q_norm[0], nsa_k_norm[0])
    blocks = _compress_blocks(p[:, Q_W:Q_W + 2 * KV_W], t)
    kvc = _compress(blocks, cmp_w[0], cmp_pos[0].reshape(2, -1), nsa_k_norm[0, 0])
    imap = _importance_map(t // CMP_STRIDE, t // SLC_BLOCK)
    o_cmp, sel = _cmp_select(slopes, qkv, kvc, imap)
    o_slc = _slc_attn(slopes, qkv, sel)
    o_win = _win_attn(slopes, qkv)
    gates = p[:, Q_W + 6 * KV_W:Q_W + 6 * KV_W + GATE_W]
    mix = _gate_sum(o_cmp, o_slc, o_win, gates)
    mem_n = _rmsnorm_cast(mems, mem_norm[0], tm=256)
    memkv = _matmul(mem_n, w_mem_kv[0], tm=256, tn=512, name="mem_kv")
    qm = p[:, Q_W + 6 * KV_W + GATE_W:]
    mo = _mem_attn(qm, 0, memkv, mem_qk_norm[0])
    attn = jnp.concatenate([mix, mo], axis=1)
    xs = _matmul(attn, w_out[0], xs, tm=1024, tn=512, name="out_proj")
    h2 = _rmsnorm_cast(xs, ffn_norm[0])
    act = _swiglu_up(h2, ffn_w_gu)
    xs = _matmul(act, ffn_w_down[0], xs, tm=1024, tn=1024, tk=512, name="ffn_down")

    kvb = _matmul(_rmsnorm_cast(xs, kv_norm_b), w_kv_b, tm=1024, tn=512, out_dtype=BF16, name="kv_proj_b")
    h = _rmsnorm_cast(xs, attn_norm[1])
    pb = _matmul(h, w_in_b[0], tm=1024, tn=512, name="in_proj_b")
    mix = _stick_attn(pb, kvb)
    mem_n = _rmsnorm_cast(mems, mem_norm[1], tm=256)
    memkv = _matmul(mem_n, w_mem_kv[1], tm=256, tn=512, name="mem_kv")
    mo = _mem_attn(pb, N_MIX_HEADS, memkv, mem_qk_norm[1])
    attn = jnp.concatenate([mix, mo], axis=1)
    xs = _matmul(attn, w_out[1], xs, tm=1024, tn=512, name="out_proj")
    xs = _moe_block(xs, ffn_norm[1], router_w[0], router_b[0], moe_w_gu[0], moe_w_down[0])
    return xs[None]
```

```python
import functools

import numpy as np
import jax
import jax.numpy as jnp
from jax import lax
from jax.experimental import pallas as pl
from jax.experimental.pallas import tpu as pltpu

HEAD_DIM = 128
N_MEM_HEADS = 4
N_MIX_HEADS = 12
N_KV_GROUPS = 3
HEADS_PER_GROUP = 4
CMP_STRIDE = 16
CMP_BLOCK = 32
SLC_BLOCK = 64
SLC_TOPK = 16
WINDOW = 512
N_EXPERTS = 8
Q_W = N_MIX_HEADS * HEAD_DIM
KV_W = N_KV_GROUPS * HEAD_DIM
GATE_W = 3 * N_MIX_HEADS
MEM_Q_W = N_MEM_HEADS * HEAD_DIM
NEG = -1e30
BIG = 1e6
EPS = 1e-6
SCALE = HEAD_DIM ** -0.5
STICK_DEAD = -104.0
LANES = 128
VMEM_LIMIT = 48 * 1024 * 1024

F32 = jnp.float32
BF16 = jnp.bfloat16


def _cparams(sem):
    return pltpu.CompilerParams(dimension_semantics=sem, vmem_limit_bytes=VMEM_LIMIT)


def _nt_dot(a, b):
    return lax.dot_general(a, b, (((1,), (1,)), ((), ())), preferred_element_type=F32)


def _rms(x, gain):
    return x * lax.rsqrt(jnp.mean(x * x, axis=-1, keepdims=True) + EPS) * gain


def _rmsnorm_cast_kernel(x_ref, g_ref, o_ref):
    o_ref[...] = _rms(x_ref[...], g_ref[...]).astype(o_ref.dtype)


def _rmsnorm_cast(x, gain, tm=512):
    m, d = x.shape
    tm = min(tm, m)
    return pl.pallas_call(
        _rmsnorm_cast_kernel,
        out_shape=jax.ShapeDtypeStruct((m, d), BF16),
        grid=(m // tm,),
        in_specs=[pl.BlockSpec((tm, d), lambda i: (i, 0)), pl.BlockSpec((1, d), lambda i: (0, 0))],
        out_specs=pl.BlockSpec((tm, d), lambda i: (i, 0)),
        compiler_params=_cparams(("parallel",)),
        name="rmsnorm_cast",
    )(x, gain.reshape(1, d))


def _matmul_kernel(*refs, nk, has_res):
    if has_res:
        a_ref, w_ref, r_ref, o_ref = refs[:4]
    else:
        a_ref, w_ref, o_ref = refs[:3]
        r_ref = None
    part = jnp.dot(a_ref[...], w_ref[...].astype(BF16), preferred_element_type=F32)
    if nk == 1:
        if has_res:
            part = part + r_ref[...]
        o_ref[...] = part.astype(o_ref.dtype)
        return
    acc_ref = refs[-1]
    k = pl.program_id(2)

    @pl.when(k == 0)
    def _():
        acc_ref[...] = part

    @pl.when(k > 0)
    def _():
        acc_ref[...] += part

    @pl.when(k == nk - 1)
    def _():
        r = acc_ref[...]
        if has_res:
            r = r + r_ref[...]
        o_ref[...] = r.astype(o_ref.dtype)


def _matmul(a, w, res=None, *, tm, tn, tk=None, out_dtype=F32, name="matmul"):
    m, kdim = a.shape
    n = w.shape[1]
    tm = min(tm, m)
    tk = kdim if tk is None else tk
    nk = kdim // tk
    in_specs = [pl.BlockSpec((tm, tk), lambda i, j, k: (i, k)),
                pl.BlockSpec((tk, tn), lambda i, j, k: (k, j))]
    args = [a, w]
    if res is not None:
        in_specs.append(pl.BlockSpec((tm, tn), lambda i, j, k: (i, j)))
        args.append(res)
    scratch = [pltpu.VMEM((tm, tn), F32)] if nk > 1 else []
    return pl.pallas_call(
        functools.partial(_matmul_kernel, nk=nk, has_res=res is not None),
        out_shape=jax.ShapeDtypeStruct((m, n), out_dtype),
        grid=(m // tm, pl.cdiv(n, tn), nk),
        in_specs=in_specs,
        out_specs=pl.BlockSpec((tm, tn), lambda i, j, k: (i, j)),
        scratch_shapes=scratch,
        compiler_params=_cparams(("parallel", "parallel", "arbitrary")),
        name=name,
    )(*args)


def _swiglu_kernel(*refs, scaled):
    if scaled:
        a_ref, wg_ref, wu_ref, c_ref, o_ref = refs
    else:
        a_ref, wg_ref, wu_ref, o_ref = refs
    a = a_ref[...]
    g = jnp.dot(a, wg_ref[0].astype(BF16), preferred_element_type=F32)
    u = jnp.dot(a, wu_ref[0].astype(BF16), preferred_element_type=F32)
    act = g * (1.0 / (1.0 + jnp.exp(-g))) * u
    if scaled:
        e = pl.program_id(1)
        c = c_ref[...]
        lane = lax.broadcasted_iota(jnp.int32, c.shape, 1)
        act = act * jnp.sum(jnp.where(lane == e, c, 0.0), axis=1, keepdims=True)
    o_ref[...] = act.astype(o_ref.dtype)


def _swiglu_up(a, w_gu, combine=None, *, tm=1024, tn=256):
    m, d = a.shape
    ne, _, f2 = w_gu.shape
    f = f2 // 2
    tm = min(tm, m)
    nj = f // tn
    in_specs = [pl.BlockSpec((tm, d), lambda i, e, j: (i, 0)),
                pl.BlockSpec((1, d, tn), lambda i, e, j: (e, 0, j)),
                pl.BlockSpec((1, d, tn), lambda i, e, j: (e, 0, j + nj))]
    args = [a, w_gu, w_gu]
    if combine is not None:
        in_specs.append(pl.BlockSpec((tm, LANES), lambda i, e, j: (i, 0)))
        args.append(combine)
    return pl.pallas_call(
        functools.partial(_swiglu_kernel, scaled=combine is not None),
        out_shape=jax.ShapeDtypeStruct((m, ne * f), BF16),
        grid=(m // tm, ne, nj),
        in_specs=in_specs,
        out_specs=pl.BlockSpec((tm, tn), lambda i, e, j: (i, e * nj + j)),
        compiler_params=_cparams(("parallel", "parallel", "parallel")),
        name="swiglu_up",
    )(*args)


def _nsa_prep_kernel(p_ref, gtab_ref, o_ref):
    for c in range((Q_W + 6 * KV_W) // HEAD_DIM):
        x = p_ref[:, c * HEAD_DIM:(c + 1) * HEAD_DIM]
        if c < N_MIX_HEADS:
            y = _rms(x, gtab_ref[0:1, :]) * SCALE
        elif 18 <= c < 21:
            y = _rms(x, gtab_ref[1:2, :])
        elif 24 <= c < 27:
            y = _rms(x, gtab_ref[2:3, :])
        else:
            y = x
        o_ref[:, c * HEAD_DIM:(c + 1) * HEAD_DIM] = y.astype(o_ref.dtype)


def _nsa_prep(p, q_gain, k_gain, tm=256):
    t = p.shape[0]
    tm = min(tm, t)
    w = Q_W + 6 * KV_W
    gtab = jnp.zeros((8, HEAD_DIM), F32).at[0].set(q_gain).at[1].set(k_gain[1]).at[2].set(k_gain[2])
    return pl.pallas_call(
        _nsa_prep_kernel,
        out_shape=jax.ShapeDtypeStruct((t, w), BF16),
        grid=(t // tm,),
        in_specs=[pl.BlockSpec((tm, w), lambda i: (i, 0)), pl.BlockSpec((8, HEAD_DIM), lambda i: (0, 0))],
        out_specs=pl.BlockSpec((tm, w), lambda i: (i, 0)),
        compiler_params=_cparams(("parallel",)),
        name="nsa_prep",
    )(p, gtab)


def _compress_kernel(b_ref, pos_ref, w_ref, g_ref, o_ref):
    blk = (b_ref[0, 0] + pos_ref[0]).astype(BF16)
    y = jnp.dot(blk, w_ref[0].astype(BF16), preferred_element_type=F32)
    is_k = pl.program_id(0) == 0
    o_ref[0, 0] = jnp.where(is_k, _rms(y, g_ref[...]), y).astype(o_ref.dtype)


def _compress(blocks, cmp_w, cmp_pos, k_gain0):
    _, g, ncp, bw = blocks.shape
    return pl.pallas_call(
        _compress_kernel,
        out_shape=jax.ShapeDtypeStruct((2, g, ncp, HEAD_DIM), BF16),
        grid=(2, g),
        in_specs=[pl.BlockSpec((1, 1, ncp, bw), lambda s, gi: (s, gi, 0, 0)),
                  pl.BlockSpec((1, 1, bw), lambda s, gi: (s, 0, 0)),
                  pl.BlockSpec((1, bw, HEAD_DIM), lambda s, gi: (s, 0, 0)),
                  pl.BlockSpec((1, HEAD_DIM), lambda s, gi: (0, 0))],
        out_specs=pl.BlockSpec((1, 1, ncp, HEAD_DIM), lambda s, gi: (s, gi, 0, 0)),
        compiler_params=_cparams(("parallel", "parallel")),
        name="nsa_compress",
    )(blocks, cmp_pos.reshape(2, 1, bw), cmp_w, k_gain0.reshape(1, HEAD_DIM))


def _cmp_select_kernel(slopes_ref, q_ref, k_ref, v_ref, imap_ref, o_ref, sel_ref, act_ref, *, tq, ncp):
    g = pl.program_id(0)
    t0 = pl.program_id(1) * tq
    row = lax.broadcasted_iota(jnp.int32, (tq, ncp), 0) + t0
    col = lax.broadcasted_iota(jnp.int32, (tq, ncp), 1)
    dist_i = row - (col * CMP_STRIDE + (CMP_BLOCK - 1))
    mask = dist_i >= 0
    dist = dist_i.astype(F32)
    k = k_ref[0, 0]
    v = v_ref[0, 0]
    psum = jnp.zeros((tq, ncp), F32)
    for p in range(HEADS_PER_GROUP):
        slope = slopes_ref[g * HEADS_PER_GROUP + p]
        q = q_ref[:, p * HEAD_DIM:(p + 1) * HEAD_DIM]
        s = jnp.where(mask, _nt_dot(q, k) - slope * dist, NEG)
        m = jnp.max(s, axis=1, keepdims=True)
        e = jnp.where(mask, jnp.exp(s - m), 0.0)
        l = jnp.sum(e, axis=1, keepdims=True)
        pr = e * (1.0 / jnp.maximum(l, 1e-30))
        psum = psum + pr
        o_ref[:, p * HEAD_DIM:(p + 1) * HEAD_DIM] = jnp.dot(
            pr.astype(BF16), v, preferred_element_type=F32).astype(o_ref.dtype)

    imap = imap_ref[...]
    hi = psum.astype(BF16)
    r1 = psum - hi.astype(F32)
    mid = r1.astype(BF16)
    lo = (r1 - mid.astype(F32)).astype(BF16)
    imp = (jnp.dot(hi, imap, preferred_element_type=F32)
           + jnp.dot(mid, imap, preferred_element_type=F32)
           + jnp.dot(lo, imap, preferred_element_type=F32))

    nb = imp.shape[1]
    t = lax.broadcasted_iota(jnp.int32, (tq, nb), 0) + t0
    blk = lax.broadcasted_iota(jnp.int32, (tq, nb), 1)
    score = jnp.where(blk == t // SLC_BLOCK, 2 * BIG,
                      jnp.where(blk == 0, BIG,
                                jnp.where(blk * SLC_BLOCK <= t, imp, -BIG)))
    blk_f = blk.astype(F32)

    def pick_one(_, carry):
        work, sel = carry
        m = jnp.max(work, axis=1, keepdims=True)
        first = jnp.min(jnp.where(work == m, blk_f, float(nb)), axis=1, keepdims=True)
        pick = blk_f == first
        return jnp.where(pick, -3e38, work), jnp.where(pick, 1.0, sel)

    _, sel = lax.fori_loop(0, min(SLC_TOPK, nb), pick_one, (score, jnp.zeros((tq, nb), F32)))
    sel_ref[0] = sel.astype(sel_ref.dtype)
    act_ref[0, 0] = jnp.broadcast_to(jnp.max(sel, axis=0, keepdims=True), act_ref.shape[2:])


def _cmp_select(slopes, qkv, kvc, imap, tq=256):
    t = qkv.shape[0]
    tq = min(tq, t)
    ncp = kvc.shape[2]
    nb = t // SLC_BLOCK
    gw = HEADS_PER_GROUP * HEAD_DIM
    return pl.pallas_call(
        functools.partial(_cmp_select_kernel, tq=tq, ncp=ncp),
        out_shape=(jax.ShapeDtypeStruct((t, Q_W), F32),
                   jax.ShapeDtypeStruct((N_KV_GROUPS, t, nb), BF16),
                   jax.ShapeDtypeStruct((N_KV_GROUPS, t // tq, 8, nb), F32)),
        grid_spec=pltpu.PrefetchScalarGridSpec(
            num_scalar_prefetch=1,
            grid=(N_KV_GROUPS, t // tq),
            in_specs=[pl.BlockSpec((tq, gw), lambda g, i, s: (i, g)),
                      pl.BlockSpec((1, 1, ncp, HEAD_DIM), lambda g, i, s: (0, g, 0, 0)),
                      pl.BlockSpec((1, 1, ncp, HEAD_DIM), lambda g, i, s: (1, g, 0, 0)),
                      pl.BlockSpec((ncp, nb), lambda g, i, s: (0, 0))],
            out_specs=(pl.BlockSpec((tq, gw), lambda g, i, s: (i, g)),
                       pl.BlockSpec((1, tq, nb), lambda g, i, s: (g, i, 0)),
                       pl.BlockSpec((1, 1, 8, nb), lambda g, i, s: (g, i, 0, 0)))),
        compiler_params=_cparams(("parallel", "parallel")),
        name="nsa_cmp_select",
    )(slopes, qkv, kvc, kvc, imap)


def _slc_kernel(slopes_ref, flags_ref, q_ref, k_ref, v_ref, sel_ref, o_ref, m_ref, l_ref, acc_ref, *, tq, tk):
    g = pl.program_id(0)
    i = pl.program_id(1)
    t0 = i * tq
    flag_base = (g * pl.num_programs(1) + i) * (k_ref.shape[0] // tk)
    nb = sel_ref.shape[2]
    m_ref[...] = jnp.full(m_ref.shape, NEG, F32)
    l_ref[...] = jnp.zeros(l_ref.shape, F32)
    acc_ref[...] = jnp.zeros(acc_ref.shape, F32)
    sel = sel_ref[0]
    bpt = tk // SLC_BLOCK

    def body(j, carry):
        @pl.when(flags_ref[flag_base + j] > 0)
        def _():
            tile(j)
        return carry

    def tile(j):
        k0 = pl.multiple_of(j * tk, tk)
        k = k_ref[pl.ds(k0, tk), :]
        v = v_ref[pl.ds(k0, tk), :]
        eb = lax.broadcasted_iota(jnp.int32, (nb, tk), 0)
        ec = lax.broadcasted_iota(jnp.int32, (nb, tk), 1)
        expand = jnp.where(eb == ec // SLC_BLOCK + j * bpt, 1.0, 0.0).astype(BF16)
        selx = jnp.dot(sel, expand, preferred_element_type=F32)
        rel = (t0 - k0) + lax.broadcasted_iota(jnp.int32, (tq, tk), 0) \
            - lax.broadcasted_iota(jnp.int32, (tq, tk), 1)
        mask = jnp.where(rel >= 0, selx, 0.0) > 0.5
        colb = (lax.broadcasted_iota(jnp.int32, (1, tk), 1) + (k0 - t0)).astype(F32)
        for p in range(HEADS_PER_GROUP):
            slope = slopes_ref[g * HEADS_PER_GROUP + p]
            q = q_ref[:, p * HEAD_DIM:(p + 1) * HEAD_DIM]
            s = jnp.where(mask, _nt_dot(q, k) + slope * colb, NEG)
            m_old = m_ref[p]
            m_new = jnp.maximum(m_old, jnp.max(s, axis=1, keepdims=True))
            alpha = jnp.exp(m_old - m_new)
            e = jnp.where(mask, jnp.exp(s - m_new), 0.0)
            l_ref[p] = alpha * l_ref[p] + jnp.sum(e, axis=1, keepdims=True)
            acc_ref[p] = alpha * acc_ref[p] + jnp.dot(e.astype(BF16), v, preferred_element_type=F32)
            m_ref[p] = m_new

    n_tiles = (t0 + tq - 1) // tk + 1
    lax.fori_loop(0, n_tiles, body, 0)
    for p in range(HEADS_PER_GROUP):
        o_ref[:, p * HEAD_DIM:(p + 1) * HEAD_DIM] = (
            acc_ref[p] * (1.0 / jnp.maximum(l_ref[p], 1e-30))).astype(o_ref.dtype)


def _slc_attn(slopes, qkv, sel, blk_act, tq=256, tk=256):
    t = qkv.shape[0]
    tq = min(tq, t)
    tk = min(tk, t)
    nb = sel.shape[2]
    assert blk_act.shape[1] == t // tq
    gw = HEADS_PER_GROUP * HEAD_DIM
    ks_col = (Q_W + 2 * KV_W) // HEAD_DIM
    vs_col = (Q_W + 3 * KV_W) // HEAD_DIM
    flags = jnp.max(blk_act.reshape(N_KV_GROUPS, t // tq, t // tk, tk // SLC_BLOCK), axis=-1)
    flags = (flags > 0.5).astype(jnp.int32).reshape(-1)
    return pl.pallas_call(
        functools.partial(_slc_kernel, tq=tq, tk=tk),
        out_shape=jax.ShapeDtypeStruct((t, Q_W), F32),
        grid_spec=pltpu.PrefetchScalarGridSpec(
            num_scalar_prefetch=2,
            grid=(N_KV_GROUPS, t // tq),
            in_specs=[pl.BlockSpec((tq, gw), lambda g, i, s, f: (i, g)),
                      pl.BlockSpec((t, HEAD_DIM), lambda g, i, s, f: (0, ks_col + g)),
                      pl.BlockSpec((t, HEAD_DIM), lambda g, i, s, f: (0, vs_col + g)),
                      pl.BlockSpec((1, tq, nb), lambda g, i, s, f: (g, i, 0))],
            out_specs=pl.BlockSpec((tq, gw), lambda g, i, s, f: (i, g)),
            scratch_shapes=[pltpu.VMEM((HEADS_PER_GROUP, tq, 1), F32),
                            pltpu.VMEM((HEADS_PER_GROUP, tq, 1), F32),
                            pltpu.VMEM((HEADS_PER_GROUP, tq, HEAD_DIM), F32)]),
        compiler_params=_cparams(("parallel", "parallel")),
        name="nsa_selected",
    )(slopes, flags, qkv, qkv, qkv, sel)


def _win_kernel(slopes_ref, q_ref, k0_ref, k1_ref, k2_ref, v0_ref, v1_ref, v2_ref, o_ref, *, tq):
    g = pl.program_id(0)
    i = pl.program_id(1)
    nprev = WINDOW // tq
    kw = (nprev + 1) * tq
    k = jnp.concatenate([k0_ref[...], k1_ref[...], k2_ref[...]], axis=0)
    v = jnp.concatenate([v0_ref[...], v1_ref[...], v2_ref[...]], axis=0)
    r = lax.broadcasted_iota(jnp.int32, (tq, kw), 0)
    c = lax.broadcasted_iota(jnp.int32, (tq, kw), 1)
    rel = WINDOW + r - c
    first_valid = (nprev - i) * tq
    mask = (rel >= 0) & (rel < WINDOW) & (c >= first_valid)
    rel_f = rel.astype(F32)
    for p in range(HEADS_PER_GROUP):
        slope = slopes_ref[g * HEADS_PER_GROUP + p]
        q = q_ref[:, p * HEAD_DIM:(p + 1) * HEAD_DIM]
        s = jnp.where(mask, _nt_dot(q, k) - slope * rel_f, NEG)
        m = jnp.max(s, axis=1, keepdims=True)
        e = jnp.where(mask, jnp.exp(s - m), 0.0)
        l = jnp.sum(e, axis=1, keepdims=True)
        o = jnp.dot(e.astype(BF16), v, preferred_element_type=F32) * (1.0 / jnp.maximum(l, 1e-30))
        o_ref[:, p * HEAD_DIM:(p + 1) * HEAD_DIM] = o.astype(o_ref.dtype)


def _win_attn(slopes, qkv, tq=256):
    t = qkv.shape[0]
    assert WINDOW % tq == 0 and WINDOW // tq == 2
    gw = HEADS_PER_GROUP * HEAD_DIM
    kw_col = (Q_W + 4 * KV_W) // HEAD_DIM
    vw_col = (Q_W + 5 * KV_W) // HEAD_DIM

    def kv_spec(col, back):
        return pl.BlockSpec((tq, HEAD_DIM), lambda g, i, s: (jnp.maximum(i - back, 0), col + g))

    return pl.pallas_call(
        functools.partial(_win_kernel, tq=tq),
        out_shape=jax.ShapeDtypeStruct((t, Q_W), F32),
        grid_spec=pltpu.PrefetchScalarGridSpec(
            num_scalar_prefetch=1,
            grid=(N_KV_GROUPS, t // tq),
            in_specs=[pl.BlockSpec((tq, gw), lambda g, i, s: (i, g)),
                      kv_spec(kw_col, 2), kv_spec(kw_col, 1), kv_spec(kw_col, 0),
                      kv_spec(vw_col, 2), kv_spec(vw_col, 1), kv_spec(vw_col, 0)],
            out_specs=pl.BlockSpec((tq, gw), lambda g, i, s: (i, g))),
        compiler_params=_cparams(("parallel", "parallel")),
        name="nsa_window",
    )(slopes, qkv, qkv, qkv, qkv, qkv, qkv, qkv)


def _gate_sum_kernel(oc_ref, os_ref, ow_ref, g_ref, o_ref):
    sg = 1.0 / (1.0 + jnp.exp(-g_ref[...]))
    for h in range(N_MIX_HEADS):
        sl = slice(h * HEAD_DIM, (h + 1) * HEAD_DIM)
        o = (oc_ref[:, sl] * sg[:, 3 * h:3 * h + 1]
             + os_ref[:, sl] * sg[:, 3 * h + 1:3 * h + 2]
             + ow_ref[:, sl] * sg[:, 3 * h + 2:3 * h + 3])
        o_ref[:, sl] = o.astype(o_ref.dtype)


def _gate_sum(o_cmp, o_slc, o_win, gates, tm=512):
    t = o_cmp.shape[0]
    tm = min(tm, t)
    big = pl.BlockSpec((tm, Q_W), lambda i: (i, 0))
    return pl.pallas_call(
        _gate_sum_kernel,
        out_shape=jax.ShapeDtypeStruct((t, Q_W), BF16),
        grid=(t // tm,),
        in_specs=[big, big, big, pl.BlockSpec((tm, GATE_W), lambda i: (i, 0))],
        out_specs=big,
        compiler_params=_cparams(("parallel",)),
        name="nsa_gate_sum",
    )(o_cmp, o_slc, o_win, gates)


def _mem_attn_kernel(q_ref, k_ref, v_ref, g_ref, o_ref):
    q = (_rms(q_ref[...], g_ref[0:1, :]) * SCALE).astype(BF16)
    k = _rms(k_ref[...], g_ref[1:2, :]).astype(BF16)
    s = _nt_dot(q, k)
    e = jnp.exp(s - jnp.max(s, axis=1, keepdims=True))
    p = e * (1.0 / jnp.sum(e, axis=1, keepdims=True))
    o_ref[...] = jnp.dot(p.astype(BF16), v_ref[...].astype(BF16),
                         preferred_element_type=F32).astype(o_ref.dtype)


def _mem_attn(qsrc, q_col, memkv, qk_gain, tq=512):
    t = qsrc.shape[0]
    tq = min(tq, t)
    nm = memkv.shape[0]
    gtab = jnp.zeros((8, HEAD_DIM), F32).at[0:2].set(qk_gain)
    return pl.pallas_call(
        _mem_attn_kernel,
        out_shape=jax.ShapeDtypeStruct((t, MEM_Q_W), BF16),
        grid=(N_MEM_HEADS, t // tq),
        in_specs=[pl.BlockSpec((tq, HEAD_DIM), lambda h, i: (i, q_col + h)),
                  pl.BlockSpec((nm, HEAD_DIM), lambda h, i: (0, h)),
                  pl.BlockSpec((nm, HEAD_DIM), lambda h, i: (0, N_MEM_HEADS + h)),
                  pl.BlockSpec((8, HEAD_DIM), lambda h, i: (0, 0))],
        out_specs=pl.BlockSpec((tq, HEAD_DIM), lambda h, i: (i, h)),
        compiler_params=_cparams(("parallel", "parallel")),
        name="mem_attn",
    )(qsrc, memkv, memkv, gtab)


def _stick_kernel(q_ref, k_ref, v_ref, o_ref, *, tq):
    i = pl.program_id(1)
    t0 = i * tq
    q = (q_ref[...] * SCALE).astype(BF16)
    r = lax.broadcasted_iota(jnp.int32, (tq, tq), 0)
    c = lax.broadcasted_iota(jnp.int32, (tq, tq), 1)
    later = jnp.where(r > c, 1.0, 0.0).astype(BF16)

    def live(st):
        return jnp.logical_and(st[0] <= i, st[3] > 0)

    def body(st):
        jj, tail, acc, _ = st
        k0 = pl.multiple_of((i - jj) * tq, tq)
        k = k_ref[pl.ds(k0, tq), :]
        v = v_ref[pl.ds(k0, tq), :]
        z = _nt_dot(q, k)
        causal = (t0 - k0) + r - c > 0
        sp = jnp.maximum(z, 0.0) + jnp.log1p(jnp.exp(-jnp.abs(z)))
        log_keep = jnp.where(causal, -sp, 0.0)
        hi = log_keep.astype(BF16)
        lo = (log_keep - hi.astype(F32)).astype(BF16)
        after = (jnp.dot(hi, later, preferred_element_type=F32)
                 + jnp.dot(lo, later, preferred_element_type=F32) + tail)
        a = jnp.where(causal, jnp.exp(z - sp + after), 0.0)
        acc = acc + jnp.dot(a.astype(BF16), v, preferred_element_type=F32)
        tail = tail + jnp.sum(log_keep, axis=1, keepdims=True)
        return jj + 1, tail, acc, (jnp.max(tail) > STICK_DEAD).astype(jnp.int32)

    st = lax.while_loop(live, body, (jnp.int32(0), jnp.zeros((tq, 1), F32),
                                     jnp.zeros((tq, HEAD_DIM), F32), jnp.int32(1)))
    o_ref[...] = st[2].astype(o_ref.dtype)


def _stick_attn(pb, kvb, tq=256):
    t = pb.shape[0]
    tq = min(tq, t)
    return pl.pallas_call(
        functools.partial(_stick_kernel, tq=tq),
        out_shape=jax.ShapeDtypeStruct((t, Q_W), BF16),
        grid=(N_MIX_HEADS, t // tq),
        in_specs=[pl.BlockSpec((tq, HEAD_DIM), lambda h, i: (i, h)),
                  pl.BlockSpec((t, HEAD_DIM), lambda h, i: (0, h)),
                  pl.BlockSpec((t, HEAD_DIM), lambda h, i: (0, N_MIX_HEADS + h))],
        out_specs=pl.BlockSpec((tq, HEAD_DIM), lambda h, i: (i, h)),
        compiler_params=_cparams(("parallel", "parallel")),
        name="stick_breaking",
    )(pb, kvb, kvb)


def _router_kernel(x_ref, g_ref, w_ref, b_ref, c_ref):
    h = _rms(x_ref[...], g_ref[...])
    logits = jnp.dot(h, w_ref[...], preferred_element_type=F32, precision=lax.Precision.HIGHEST)
    lane = lax.broadcasted_iota(jnp.int32, logits.shape, 1).astype(F32)
    biased = jnp.where(lane < N_EXPERTS, logits + b_ref[...], -3e38)
    m1 = jnp.max(biased, axis=1, keepdims=True)
    i1 = jnp.min(jnp.where(biased == m1, lane, float(LANES)), axis=1, keepdims=True)
    rest = jnp.where(lane == i1, -3e38, biased)
    m2 = jnp.max(rest, axis=1, keepdims=True)
    i2 = jnp.min(jnp.where(rest == m2, lane, float(LANES)), axis=1, keepdims=True)
    l1 = jnp.sum(jnp.where(lane == i1, logits, 0.0), axis=1, keepdims=True)
    l2 = jnp.sum(jnp.where(lane == i2, logits, 0.0), axis=1, keepdims=True)
    mx = jnp.maximum(l1, l2)
    e1 = jnp.exp(l1 - mx)
    e2 = jnp.exp(l2 - mx)
    inv = 1.0 / (e1 + e2)
    c_ref[...] = (jnp.where(lane == 0.0, i1, 0.0) + jnp.where(lane == 1.0, i2, 0.0)
                  + jnp.where(lane == 2.0, e1 * inv, 0.0) + jnp.where(lane == 3.0, e2 * inv, 0.0))


def _router(x, gain, router_w, router_b, tm=512):
    t, d = x.shape
    tm = min(tm, t)
    w = jnp.zeros((d, LANES), F32).at[:, :N_EXPERTS].set(router_w)
    b = jnp.zeros((1, LANES), F32).at[0, :N_EXPERTS].set(router_b)
    return pl.pallas_call(
        _router_kernel,
        out_shape=jax.ShapeDtypeStruct((t, LANES), F32),
        grid=(t // tm,),
        in_specs=[pl.BlockSpec((tm, d), lambda i: (i, 0)), pl.BlockSpec((1, d), lambda i: (0, 0)),
                  pl.BlockSpec((d, LANES), lambda i: (0, 0)), pl.BlockSpec((1, LANES), lambda i: (0, 0))],
        out_specs=pl.BlockSpec((tm, LANES), lambda i: (i, 0)),
        compiler_params=_cparams(("parallel",)),
        name="moe_router",
    )(x, gain.reshape(1, d), w, b)


def _route_plan(rt, tm):
    t = rt.shape[0]
    nt = 2 * t // tm + N_EXPERTS
    e_flat = rt[:, 0:2].astype(jnp.int32).reshape(-1)
    w_flat = rt[:, 2:4].reshape(-1)
    onehot = (e_flat[:, None] == jnp.arange(N_EXPERTS, dtype=jnp.int32)[None, :]).astype(jnp.int32)
    csum = jnp.cumsum(onehot, axis=0)
    rank = jnp.sum(csum * onehot, axis=1) - 1
    tiles_e = (csum[-1] + tm - 1) // tm
    tile_end = jnp.cumsum(tiles_e)
    dest = (tile_end - tiles_e)[e_flat] * tm + rank
    n_tiles = tile_end[-1]
    tile_idx = jnp.minimum(jnp.arange(nt, dtype=jnp.int32), n_tiles - 1)
    tile_expert = jnp.sum((tile_idx[:, None] >= tile_end[None, :]).astype(jnp.int32), axis=1)
    row_token = jnp.zeros((nt * tm,), jnp.int32).at[dest].set(jnp.arange(2 * t, dtype=jnp.int32) // 2)
    row_w = jnp.zeros((nt * tm,), F32).at[dest].set(w_flat)
    return dest, row_token, row_w.reshape(-1, 1), tile_expert.astype(jnp.int32), n_tiles.reshape(1)


def _dispatch_kernel(tok_ref, nt_ref, x_hbm, g_ref, o_ref, xbuf, sem, *, tm):
    n = pl.program_id(0)

    @pl.when(n < nt_ref[0])
    def _():
        base = n * tm

        def issue(r, c):
            pltpu.make_async_copy(x_hbm.at[pl.ds(tok_ref[base + r], 1), :], xbuf.at[pl.ds(r, 1), :], sem).start()
            return c

        lax.fori_loop(0, tm, issue, 0)
        pltpu.make_async_copy(x_hbm.at[pl.ds(0, tm), :], xbuf, sem).wait()
        o_ref[...] = _rms(xbuf[...], g_ref[...]).astype(o_ref.dtype)

    @pl.when(n >= nt_ref[0])
    def _():
        o_ref[...] = jnp.zeros(o_ref.shape, o_ref.dtype)


def _dispatch(x, gain, row_token, n_tiles, tm):
    t, d = x.shape
    nt = row_token.shape[0] // tm
    return pl.pallas_call(
        functools.partial(_dispatch_kernel, tm=tm),
        out_shape=jax.ShapeDtypeStruct((nt * tm, d), BF16),
        grid_spec=pltpu.PrefetchScalarGridSpec(
            num_scalar_prefetch=2,
            grid=(nt,),
            in_specs=[pl.BlockSpec(memory_space=pl.ANY), pl.BlockSpec((1, d), lambda n, tok, ntl: (0, 0))],
            out_specs=pl.BlockSpec((tm, d), lambda n, tok, ntl: (n, 0)),
            scratch_shapes=[pltpu.VMEM((tm, d), F32), pltpu.SemaphoreType.DMA(())]),
        compiler_params=_cparams(("arbitrary",)),
        name="moe_dispatch",
    )(row_token, n_tiles, x, gain.reshape(1, d))


def _moe_up_kernel(te_ref, nt_ref, x_ref, wg_ref, wu_ref, o_ref, wgc, wuc):
    n = pl.program_id(1)
    prev = te_ref[jnp.maximum(n - 1, 0)]

    @pl.when(jnp.logical_or(n == 0, te_ref[n] != prev))
    def _():
        wgc[...] = wg_ref[0].astype(BF16)
        wuc[...] = wu_ref[0].astype(BF16)

    @pl.when(n < nt_ref[0])
    def _():
        x = x_ref[...]
        g = jnp.dot(x, wgc[...], preferred_element_type=F32)
        u = jnp.dot(x, wuc[...], preferred_element_type=F32)
        o_ref[...] = (g * (1.0 / (1.0 + jnp.exp(-g))) * u).astype(o_ref.dtype)

    @pl.when(n >= nt_ref[0])
    def _():
        o_ref[...] = jnp.zeros(o_ref.shape, o_ref.dtype)


def _moe_up(xg, w_gu, tile_expert, n_tiles, tm, tf=512):
    p, d = xg.shape
    nt = p // tm
    f = w_gu.shape[2] // 2
    nj = f // tf

    def row(j, n, te, ntl):
        return jnp.minimum(n, ntl[0] - 1)

    return pl.pallas_call(
        _moe_up_kernel,
        out_shape=jax.ShapeDtypeStruct((p, f), BF16),
        grid_spec=pltpu.PrefetchScalarGridSpec(
            num_scalar_prefetch=2,
            grid=(nj, nt),
            in_specs=[pl.BlockSpec((tm, d), lambda j, n, te, ntl: (row(j, n, te, ntl), 0)),
                      pl.BlockSpec((1, d, tf), lambda j, n, te, ntl: (te[n], 0, j)),
                      pl.BlockSpec((1, d, tf), lambda j, n, te, ntl: (te[n], 0, j + nj))],
            out_specs=pl.BlockSpec((tm, tf), lambda j, n, te, ntl: (n, j)),
            scratch_shapes=[pltpu.VMEM((d, tf), BF16), pltpu.VMEM((d, tf), BF16)]),
        compiler_params=_cparams(("arbitrary", "arbitrary")),
        name="moe_up",
    )(tile_expert, n_tiles, xg, w_gu, w_gu)


def _moe_down_kernel(te_ref, nt_ref, a_ref, w_ref, rw_ref, o_ref, wc):
    n = pl.program_id(1)
    prev = te_ref[jnp.maximum(n - 1, 0)]

    @pl.when(jnp.logical_or(n == 0, te_ref[n] != prev))
    def _():
        wc[...] = w_ref[0].astype(BF16)

    @pl.when(n < nt_ref[0])
    def _():
        o_ref[...] = jnp.dot(a_ref[...], wc[...], preferred_element_type=F32) * rw_ref[...]

    @pl.when(n >= nt_ref[0])
    def _():
        o_ref[...] = jnp.zeros(o_ref.shape, o_ref.dtype)


def _moe_down(act, w_down, row_w, tile_expert, n_tiles, tm, tn=256):
    p, f = act.shape
    nt = p // tm
    d = w_down.shape[2]

    def row(n, ntl):
        return jnp.minimum(n, ntl[0] - 1)

    return pl.pallas_call(
        _moe_down_kernel,
        out_shape=jax.ShapeDtypeStruct((p, d), F32),
        grid_spec=pltpu.PrefetchScalarGridSpec(
            num_scalar_prefetch=2,
            grid=(d // tn, nt),
            in_specs=[pl.BlockSpec((tm, f), lambda j, n, te, ntl: (row(n, ntl), 0)),
                      pl.BlockSpec((1, f, tn), lambda j, n, te, ntl: (te[n], 0, j)),
                      pl.BlockSpec((tm, 1), lambda j, n, te, ntl: (row(n, ntl), 0))],
            out_specs=pl.BlockSpec((tm, tn), lambda j, n, te, ntl: (n, j)),
            scratch_shapes=[pltpu.VMEM((f, tn), BF16)]),
        compiler_params=_cparams(("arbitrary", "arbitrary")),
        name="moe_down",
    )(tile_expert, n_tiles, act, w_down, row_w)


def _moe_combine_kernel(dest_ref, x_ref, y_hbm, o_ref, ybuf, sem, *, tm):
    base = pl.program_id(0) * tm

    def issue(r, c):
        for k in range(2):
            pltpu.make_async_copy(y_hbm.at[pl.ds(dest_ref[2 * (base + r) + k], 1), :],
                                  ybuf.at[pl.ds(k * tm + r, 1), :], sem).start()
        return c

    lax.fori_loop(0, tm, issue, 0)
    pltpu.make_async_copy(y_hbm.at[pl.ds(0, 2 * tm), :], ybuf, sem).wait()
    o_ref[...] = x_ref[...] + ybuf[0:tm, :] + ybuf[tm:2 * tm, :]


def _moe_combine(x, y, dest, tm=256):
    t, d = x.shape
    tm = min(tm, t)
    return pl.pallas_call(
        functools.partial(_moe_combine_kernel, tm=tm),
        out_shape=jax.ShapeDtypeStruct((t, d), F32),
        grid_spec=pltpu.PrefetchScalarGridSpec(
            num_scalar_prefetch=1,
            grid=(t // tm,),
            in_specs=[pl.BlockSpec((tm, d), lambda i, dst: (i, 0)), pl.BlockSpec(memory_space=pl.ANY)],
            out_specs=pl.BlockSpec((tm, d), lambda i, dst: (i, 0)),
            scratch_shapes=[pltpu.VMEM((2 * tm, d), F32), pltpu.SemaphoreType.DMA(())]),
        compiler_params=_cparams(("arbitrary",)),
        name="moe_combine",
    )(dest, x, y)


def _moe_block(xs, gain, router_w, router_b, w_gu, w_down, tm=512):
    tm = min(tm, xs.shape[0])
    rt = _router(xs, gain, router_w, router_b)
    dest, row_token, row_w, tile_expert, n_tiles = _route_plan(rt, tm)
    xg = _dispatch(xs, gain, row_token, n_tiles, tm)
    act = _moe_up(xg, w_gu, tile_expert, n_tiles, tm)
    y = _moe_down(act, w_down, row_w, tile_expert, n_tiles, tm)
    return _moe_combine(xs, y, dest)


def _alibi_slopes():
    return jnp.asarray(np.array([2.0 ** (-8.0 * (i + 1) / N_MIX_HEADS) for i in range(N_MIX_HEADS)],
                                dtype=np.float32))


def _importance_map(ncp, nb):
    ratio = SLC_BLOCK // CMP_STRIDE
    c = np.arange(ncp)[:, None]
    b = np.arange(nb)[None, :]
    m = (c >= ratio * b - 1) & (c <= ratio * b + ratio - 1) & (c < ncp - 1)
    return jnp.asarray(m.astype(np.float32), dtype=BF16)


def _compress_blocks(p, t):
    n = t // CMP_STRIDE
    z = p.reshape(n, CMP_STRIDE, 2, N_KV_GROUPS, HEAD_DIM).transpose(2, 3, 0, 1, 4)
    z = z.reshape(2, N_KV_GROUPS, n, CMP_STRIDE * HEAD_DIM)
    nxt = jnp.concatenate([z[:, :, 1:], jnp.zeros_like(z[:, :, :1])], axis=2)
    return jnp.concatenate([z, nxt], axis=3)


def kernel(x, mem, attn_norm, mem_norm, ffn_norm, w_in_a, cmp_w, cmp_pos, nsa_q_norm, nsa_k_norm, w_in_b,
           kv_norm_b, w_kv_b, w_mem_kv, mem_qk_norm, w_out, ffn_w_gu, ffn_w_down, router_w, router_b,
           moe_w_gu, moe_w_down):
    b, t, d = x.shape
    assert b == 1
    xs = x[0]
    mems = mem[0]
    slopes = _alibi_slopes()
    d_ff = ffn_w_down.shape[1]

    h = _rmsnorm_cast(xs, attn_norm[0])
    p = _matmul(h, w_in_a[0], tm=1024, tn=640, name="in_proj_a")
    qkv = _nsa_prep(p, nsa_q_norm[0], nsa_k_norm[0])
    blocks = _compress_blocks(p[:, Q_W:Q_W + 2 * KV_W], t)
    kvc = _compress(blocks, cmp_w[0], cmp_pos[0].reshape(2, -1), nsa_k_norm[0, 0])
    imap = _importance_map(t // CMP_STRIDE, t // SLC_BLOCK)
    o_cmp, sel, blk_act = _cmp_select(slopes, qkv, kvc, imap)
    o_slc = _slc_attn(slopes, qkv, sel, blk_act[:, :, 0, :])
    o_win = _win_attn(slopes, qkv)
    gates = p[:, Q_W + 6 * KV_W:Q_W + 6 * KV_W + GATE_W]
    mix = _gate_sum(o_cmp, o_slc, o_win, gates)
    mem_n = _rmsnorm_cast(mems, mem_norm[0], tm=256)
    memkv = _matmul(mem_n, w_mem_kv[0], tm=256, tn=512, name="mem_kv")
    qm = p[:, Q_W + 6 * KV_W + GATE_W:]
    mo = _mem_attn(qm, 0, memkv, mem_qk_norm[0])
    attn = jnp.concatenate([mix, mo], axis=1)
    xs = _matmul(attn, w_out[0], xs, tm=1024, tn=512, name="out_proj")
    h2 = _rmsnorm_cast(xs, ffn_norm[0])
    act = _swiglu_up(h2, ffn_w_gu)
    xs = _matmul(act, ffn_w_down[0], xs, tm=1024, tn=1024, tk=512, name="ffn_down")

    kvb = _matmul(_rmsnorm_cast(xs, kv_norm_b), w_kv_b, tm=1024, tn=512, out_dtype=BF16, name="kv_proj_b")
    h = _rmsnorm_cast(xs, attn_norm[1])
    pb = _matmul(h, w_in_b[0], tm=1024, tn=512, name="in_proj_b")
    mix = _stick_attn(pb, kvb)
    mem_n = _rmsnorm_cast(mems, mem_norm[1], tm=256)
    memkv = _matmul(mem_n, w_mem_kv[1], tm=256, tn=512, name="mem_kv")
    mo = _mem_attn(pb, N_MIX_HEADS, memkv, mem_qk_norm[1])
    attn = jnp.concatenate([mix, mo], axis=1)
    xs = _matmul(attn, w_out[1], xs, tm=1024, tn=512, name="out_proj")
    xs = _moe_block(xs, ffn_norm[1], router_w[0], router_b[0], moe_w_gu[0], moe_w_down[0])
    return xs[None]
```

```python
import functools

import numpy as np
import jax
import jax.numpy as jnp
from jax import lax
from jax.experimental import pallas as pl
from jax.experimental.pallas import tpu as pltpu

HEAD_DIM = 128
N_MEM_HEADS = 4
N_MIX_HEADS = 12
N_KV_GROUPS = 3
HEADS_PER_GROUP = 4
CMP_STRIDE = 16
CMP_BLOCK = 32
SLC_BLOCK = 64
SLC_TOPK = 16
WINDOW = 512
N_EXPERTS = 8
Q_W = N_MIX_HEADS * HEAD_DIM
KV_W = N_KV_GROUPS * HEAD_DIM
GATE_W = 3 * N_MIX_HEADS
MEM_Q_W = N_MEM_HEADS * HEAD_DIM
NEG = -1e30
BIG = 1e6
EPS = 1e-6
SCALE = HEAD_DIM ** -0.5
STICK_DEAD = -104.0
LANES = 128
VMEM_LIMIT = 48 * 1024 * 1024

F32 = jnp.float32
BF16 = jnp.bfloat16


def _cparams(sem):
    return pltpu.CompilerParams(dimension_semantics=sem, vmem_limit_bytes=VMEM_LIMIT)


def _nt_dot(a, b):
    return lax.dot_general(a, b, (((1,), (1,)), ((), ())), preferred_element_type=F32)


def _rms(x, gain):
    return x * lax.rsqrt(jnp.mean(x * x, axis=-1, keepdims=True) + EPS) * gain


def _sigmoid(x):
    return 1.0 / (1.0 + jnp.exp(-x))


def _rmsnorm_cast_kernel(x_ref, g_ref, o_ref):
    o_ref[...] = _rms(x_ref[...], g_ref[...]).astype(o_ref.dtype)


def _rmsnorm_cast(x, gain, tm=512):
    m, d = x.shape
    tm = min(tm, m)
    return pl.pallas_call(
        _rmsnorm_cast_kernel,
        out_shape=jax.ShapeDtypeStruct((m, d), BF16),
        grid=(m // tm,),
        in_specs=[pl.BlockSpec((tm, d), lambda i: (i, 0)), pl.BlockSpec((1, d), lambda i: (0, 0))],
        out_specs=pl.BlockSpec((tm, d), lambda i: (i, 0)),
        compiler_params=_cparams(("parallel",)),
        name="rmsnorm_cast",
    )(x, gain.reshape(1, d))


def _norm_matmul_kernel(x_ref, g_ref, w_ref, o_ref, xn_ref):
    @pl.when(pl.program_id(1) == 0)
    def _():
        xn_ref[...] = _rms(x_ref[...], g_ref[...]).astype(BF16)

    o_ref[...] = jnp.dot(xn_ref[...], w_ref[...].astype(BF16), preferred_element_type=F32).astype(o_ref.dtype)


def _norm_matmul(x, gain, w, *, tm, tn, out_dtype=F32, name="norm_matmul"):
    m, d = x.shape
    n = w.shape[1]
    tm = min(tm, m)
    return pl.pallas_call(
        _norm_matmul_kernel,
        out_shape=jax.ShapeDtypeStruct((m, n), out_dtype),
        grid=(m // tm, pl.cdiv(n, tn)),
        in_specs=[pl.BlockSpec((tm, d), lambda i, j: (i, 0)),
                  pl.BlockSpec((1, d), lambda i, j: (0, 0)),
                  pl.BlockSpec((d, tn), lambda i, j: (0, j))],
        out_specs=pl.BlockSpec((tm, tn), lambda i, j: (i, j)),
        scratch_shapes=[pltpu.VMEM((tm, d), BF16)],
        compiler_params=_cparams(("parallel", "arbitrary")),
        name=name,
    )(x, gain.reshape(1, d), w)


def _matmul_kernel(*refs, nk, has_res):
    if has_res:
        a_ref, w_ref, r_ref, o_ref = refs[:4]
    else:
        a_ref, w_ref, o_ref = refs[:3]
        r_ref = None
    part = jnp.dot(a_ref[...], w_ref[...].astype(BF16), preferred_element_type=F32)
    if nk == 1:
        if has_res:
            part = part + r_ref[...]
        o_ref[...] = part.astype(o_ref.dtype)
        return
    acc_ref = refs[-1]
    k = pl.program_id(2)

    @pl.when(k == 0)
    def _():
        acc_ref[...] = part

    @pl.when(k > 0)
    def _():
        acc_ref[...] += part

    @pl.when(k == nk - 1)
    def _():
        r = acc_ref[...]
        if has_res:
            r = r + r_ref[...]
        o_ref[...] = r.astype(o_ref.dtype)


def _matmul(a, w, res=None, *, tm, tn, tk=None, out_dtype=F32, name="matmul"):
    m, kdim = a.shape
    n = w.shape[1]
    tm = min(tm, m)
    tk = kdim if tk is None else tk
    nk = kdim // tk
    in_specs = [pl.BlockSpec((tm, tk), lambda i, j, k: (i, k)),
                pl.BlockSpec((tk, tn), lambda i, j, k: (k, j))]
    args = [a, w]
    if res is not None:
        in_specs.append(pl.BlockSpec((tm, tn), lambda i, j, k: (i, j)))
        args.append(res)
    scratch = [pltpu.VMEM((tm, tn), F32)] if nk > 1 else []
    return pl.pallas_call(
        functools.partial(_matmul_kernel, nk=nk, has_res=res is not None),
        out_shape=jax.ShapeDtypeStruct((m, n), out_dtype),
        grid=(m // tm, pl.cdiv(n, tn), nk),
        in_specs=in_specs,
        out_specs=pl.BlockSpec((tm, tn), lambda i, j, k: (i, j)),
        scratch_shapes=scratch,
        compiler_params=_cparams(("parallel", "parallel", "arbitrary")),
        name=name,
    )(*args)


def _out_proj_kernel(mix_ref, mo_ref, w1_ref, w2_ref, r_ref, o_ref):
    o_ref[...] = (jnp.dot(mix_ref[...], w1_ref[...].astype(BF16), preferred_element_type=F32)
                  + jnp.dot(mo_ref[...], w2_ref[...].astype(BF16), preferred_element_type=F32)
                  + r_ref[...])


def _out_proj(mix, mo, w, res, tm=1024, tn=512):
    t = mix.shape[0]
    tm = min(tm, t)
    n = w.shape[1]
    assert Q_W % MEM_Q_W == 0
    return pl.pallas_call(
        _out_proj_kernel,
        out_shape=jax.ShapeDtypeStruct((t, n), F32),
        grid=(t // tm, n // tn),
        in_specs=[pl.BlockSpec((tm, Q_W), lambda i, j: (i, 0)),
                  pl.BlockSpec((tm, MEM_Q_W), lambda i, j: (i, 0)),
                  pl.BlockSpec((Q_W, tn), lambda i, j: (0, j)),
                  pl.BlockSpec((MEM_Q_W, tn), lambda i, j: (Q_W // MEM_Q_W, j)),
                  pl.BlockSpec((tm, tn), lambda i, j: (i, j))],
        out_specs=pl.BlockSpec((tm, tn), lambda i, j: (i, j)),
        compiler_params=_cparams(("parallel", "parallel")),
        name="out_proj",
    )(mix, mo, w, w, res)


def _swiglu_kernel(x_ref, gn_ref, wg_ref, wu_ref, o_ref, xn_ref):
    @pl.when(pl.program_id(1) == 0)
    def _():
        xn_ref[...] = _rms(x_ref[...], gn_ref[...]).astype(BF16)

    a = xn_ref[...]
    g = jnp.dot(a, wg_ref[...].astype(BF16), preferred_element_type=F32)
    u = jnp.dot(a, wu_ref[...].astype(BF16), preferred_element_type=F32)
    o_ref[...] = (g * _sigmoid(g) * u).astype(o_ref.dtype)


def _swiglu_up(x, gain, w_gu, *, tm=1024, tn=256):
    m, d = x.shape
    f = w_gu.shape[1] // 2
    tm = min(tm, m)
    nj = f // tn
    return pl.pallas_call(
        _swiglu_kernel,
        out_shape=jax.ShapeDtypeStruct((m, f), BF16),
        grid=(m // tm, nj),
        in_specs=[pl.BlockSpec((tm, d), lambda i, j: (i, 0)),
                  pl.BlockSpec((1, d), lambda i, j: (0, 0)),
                  pl.BlockSpec((d, tn), lambda i, j: (0, j)),
                  pl.BlockSpec((d, tn), lambda i, j: (0, j + nj))],
        out_specs=pl.BlockSpec((tm, tn), lambda i, j: (i, j)),
        scratch_shapes=[pltpu.VMEM((tm, d), BF16)],
        compiler_params=_cparams(("parallel", "arbitrary")),
        name="swiglu_up",
    )(x, gain.reshape(1, d), w_gu, w_gu)


def _nsa_prep_kernel(p_ref, gtab_ref, o_ref):
    for c in range((Q_W + 6 * KV_W) // HEAD_DIM):
        x = p_ref[:, c * HEAD_DIM:(c + 1) * HEAD_DIM]
        if c < N_MIX_HEADS:
            y = _rms(x, gtab_ref[0:1, :]) * SCALE
        elif 18 <= c < 21:
            y = _rms(x, gtab_ref[1:2, :])
        elif 24 <= c < 27:
            y = _rms(x, gtab_ref[2:3, :])
        else:
            y = x
        o_ref[:, c * HEAD_DIM:(c + 1) * HEAD_DIM] = y.astype(o_ref.dtype)


def _nsa_prep(p, q_gain, k_gain, tm=256):
    t = p.shape[0]
    tm = min(tm, t)
    w = Q_W + 6 * KV_W
    gtab = jnp.zeros((8, HEAD_DIM), F32).at[0].set(q_gain).at[1].set(k_gain[1]).at[2].set(k_gain[2])
    return pl.pallas_call(
        _nsa_prep_kernel,
        out_shape=jax.ShapeDtypeStruct((t, w), BF16),
        grid=(t // tm,),
        in_specs=[pl.BlockSpec((tm, w), lambda i: (i, 0)), pl.BlockSpec((8, HEAD_DIM), lambda i: (0, 0))],
        out_specs=pl.BlockSpec((tm, w), lambda i: (i, 0)),
        compiler_params=_cparams(("parallel",)),
        name="nsa_prep",
    )(p, gtab)


def _compress_kernel(b_ref, pos_ref, w_ref, g_ref, o_ref):
    blk = (b_ref[0, 0] + pos_ref[0]).astype(BF16)
    y = jnp.dot(blk, w_ref[0].astype(BF16), preferred_element_type=F32)
    is_k = pl.program_id(0) == 0
    o_ref[0, 0] = jnp.where(is_k, _rms(y, g_ref[...]), y).astype(o_ref.dtype)


def _compress(blocks, cmp_w, cmp_pos, k_gain0):
    _, g, ncp, bw = blocks.shape
    return pl.pallas_call(
        _compress_kernel,
        out_shape=jax.ShapeDtypeStruct((2, g, ncp, HEAD_DIM), BF16),
        grid=(2, g),
        in_specs=[pl.BlockSpec((1, 1, ncp, bw), lambda s, gi: (s, gi, 0, 0)),
                  pl.BlockSpec((1, 1, bw), lambda s, gi: (s, 0, 0)),
                  pl.BlockSpec((1, bw, HEAD_DIM), lambda s, gi: (s, 0, 0)),
                  pl.BlockSpec((1, HEAD_DIM), lambda s, gi: (0, 0))],
        out_specs=pl.BlockSpec((1, 1, ncp, HEAD_DIM), lambda s, gi: (s, gi, 0, 0)),
        compiler_params=_cparams(("parallel", "parallel")),
        name="nsa_compress",
    )(blocks, cmp_pos.reshape(2, 1, bw), cmp_w, k_gain0.reshape(1, HEAD_DIM))


def _cmp_kernel(slopes_ref, q_ref, k_ref, v_ref, imap_ref, o_ref, score_ref, *, tq, ncp):
    g = pl.program_id(0)
    t0 = pl.program_id(1) * tq
    row = lax.broadcasted_iota(jnp.int32, (tq, ncp), 0) + t0
    col = lax.broadcasted_iota(jnp.int32, (tq, ncp), 1)
    c_end = col * CMP_STRIDE + (CMP_BLOCK - 1)
    neg = jnp.where(row >= c_end, 0.0, NEG)
    colb = (lax.broadcasted_iota(jnp.int32, (1, ncp), 1) * CMP_STRIDE + (CMP_BLOCK - 1) - t0).astype(F32)
    has_key = (lax.broadcasted_iota(jnp.int32, (tq, 1), 0) + t0 >= CMP_BLOCK - 1).astype(F32)
    k = k_ref[0, 0]
    v = v_ref[0, 0]
    psum = jnp.zeros((tq, ncp), F32)
    for p in range(HEADS_PER_GROUP):
        slope = slopes_ref[g * HEADS_PER_GROUP + p]
        q = q_ref[:, p * HEAD_DIM:(p + 1) * HEAD_DIM]
        s = _nt_dot(q, k) + slope * colb + neg
        e = jnp.exp(s - jnp.max(s, axis=1, keepdims=True))
        pr = e * (has_key / jnp.sum(e, axis=1, keepdims=True))
        psum = psum + pr
        o_ref[:, p * HEAD_DIM:(p + 1) * HEAD_DIM] = jnp.dot(
            pr.astype(BF16), v, preferred_element_type=F32).astype(o_ref.dtype)

    imap = imap_ref[...]
    hi = psum.astype(BF16)
    r1 = psum - hi.astype(F32)
    mid = r1.astype(BF16)
    lo = (r1 - mid.astype(F32)).astype(BF16)
    imp = (jnp.dot(hi, imap, preferred_element_type=F32)
           + jnp.dot(mid, imap, preferred_element_type=F32)
           + jnp.dot(lo, imap, preferred_element_type=F32))
    nb = imp.shape[1]
    t = lax.broadcasted_iota(jnp.int32, (tq, nb), 0) + t0
    blk = lax.broadcasted_iota(jnp.int32, (tq, nb), 1)
    score_ref[0] = jnp.where(blk == t // SLC_BLOCK, 2 * BIG,
                             jnp.where(blk == 0, BIG,
                                       jnp.where(blk * SLC_BLOCK <= t, imp, -BIG)))


def _cmp_attn(slopes, qkv, kvc, imap, tq=256):
    t = qkv.shape[0]
    tq = min(tq, t)
    ncp = kvc.shape[2]
    nb = t // SLC_BLOCK
    gw = HEADS_PER_GROUP * HEAD_DIM
    return pl.pallas_call(
        functools.partial(_cmp_kernel, tq=tq, ncp=ncp),
        out_shape=(jax.ShapeDtypeStruct((t, Q_W), F32),
                   jax.ShapeDtypeStruct((N_KV_GROUPS, t, nb), F32)),
        grid_spec=pltpu.PrefetchScalarGridSpec(
            num_scalar_prefetch=1,
            grid=(N_KV_GROUPS, t // tq),
            in_specs=[pl.BlockSpec((tq, gw), lambda g, i, s: (i, g)),
                      pl.BlockSpec((1, 1, ncp, HEAD_DIM), lambda g, i, s: (0, g, 0, 0)),
                      pl.BlockSpec((1, 1, ncp, HEAD_DIM), lambda g, i, s: (1, g, 0, 0)),
                      pl.BlockSpec((ncp, nb), lambda g, i, s: (0, 0))],
            out_specs=(pl.BlockSpec((tq, gw), lambda g, i, s: (i, g)),
                       pl.BlockSpec((1, tq, nb), lambda g, i, s: (g, i, 0)))),
        compiler_params=_cparams(("parallel", "parallel")),
        name="nsa_cmp",
    )(slopes, qkv, kvc, kvc, imap)


def _topk_kernel(score_ref, sel_ref, act_ref, *, tm, sub):
    score = score_ref[0]
    nb = score.shape[1]
    blk_f = lax.broadcasted_iota(jnp.int32, (tm, nb), 1).astype(F32)

    def pick_one(_, carry):
        work, sel = carry
        m = jnp.max(work, axis=1, keepdims=True)
        first = jnp.min(jnp.where(work == m, blk_f, float(nb)), axis=1, keepdims=True)
        pick = blk_f == first
        return jnp.where(pick, -3e38, work), jnp.where(pick, 1.0, sel)

    _, sel = lax.fori_loop(0, min(SLC_TOPK, nb), pick_one, (score, jnp.zeros((tm, nb), F32)))
    sel_ref[0] = sel.astype(sel_ref.dtype)
    for a in range(tm // sub):
        act_ref[0, a] = jnp.broadcast_to(jnp.max(sel[a * sub:(a + 1) * sub], axis=0, keepdims=True),
                                         act_ref.shape[2:])


def _topk_select(score, sub, tm=1024):
    g, t, nb = score.shape
    tm = min(tm, t)
    sub = min(sub, tm)
    return pl.pallas_call(
        functools.partial(_topk_kernel, tm=tm, sub=sub),
        out_shape=(jax.ShapeDtypeStruct((g, t, nb), BF16),
                   jax.ShapeDtypeStruct((g, t // sub, 8, nb), F32)),
        grid=(g, t // tm),
        in_specs=[pl.BlockSpec((1, tm, nb), lambda gi, i: (gi, i, 0))],
        out_specs=(pl.BlockSpec((1, tm, nb), lambda gi, i: (gi, i, 0)),
                   pl.BlockSpec((1, tm // sub, 8, nb), lambda gi, i: (gi, i, 0, 0))),
        compiler_params=_cparams(("parallel", "parallel")),
        name="nsa_topk",
    )(score)


def _slc_kernel(slopes_ref, flags_ref, q_ref, k_ref, v_ref, sel_ref, o_ref, m_ref, l_ref, acc_ref, *, tq, tk):
    g = pl.program_id(0)
    i = pl.program_id(1)
    t0 = i * tq
    flag_base = (g * pl.num_programs(1) + i) * (k_ref.shape[0] // tk)
    nb = sel_ref.shape[2]
    m_ref[...] = jnp.full(m_ref.shape, NEG, F32)
    l_ref[...] = jnp.zeros(l_ref.shape, F32)
    acc_ref[...] = jnp.zeros(acc_ref.shape, F32)
    sel = sel_ref[0]
    bpt = tk // SLC_BLOCK

    def body(j, carry):
        @pl.when(flags_ref[flag_base + j] > 0)
        def _():
            tile(j)
        return carry

    def tile(j):
        k0 = pl.multiple_of(j * tk, tk)
        k = k_ref[pl.ds(k0, tk), :]
        v = v_ref[pl.ds(k0, tk), :]
        eb = lax.broadcasted_iota(jnp.int32, (nb, tk), 0)
        ec = lax.broadcasted_iota(jnp.int32, (nb, tk), 1)
        expand = jnp.where(eb == ec // SLC_BLOCK + j * bpt, 1.0, 0.0).astype(BF16)
        selx = jnp.dot(sel, expand, preferred_element_type=F32)
        rel = (t0 - k0) + lax.broadcasted_iota(jnp.int32, (tq, tk), 0) \
            - lax.broadcasted_iota(jnp.int32, (tq, tk), 1)
        neg = jnp.where(jnp.where(rel >= 0, selx, 0.0) > 0.5, 0.0, NEG)
        colb = (lax.broadcasted_iota(jnp.int32, (1, tk), 1) + (k0 - t0)).astype(F32)
        for p in range(HEADS_PER_GROUP):
            slope = slopes_ref[g * HEADS_PER_GROUP + p]
            q = q_ref[:, p * HEAD_DIM:(p + 1) * HEAD_DIM]
            s = _nt_dot(q, k) + slope * colb + neg
            m_old = m_ref[p]
            m_new = jnp.maximum(m_old, jnp.max(s, axis=1, keepdims=True))
            alpha = jnp.exp(m_old - m_new)
            e = jnp.exp(s - jnp.tile(m_new, (1, tk // LANES)))
            l_ref[p] = alpha * l_ref[p] + jnp.sum(e, axis=1, keepdims=True)
            acc_ref[p] = alpha * acc_ref[p] + jnp.dot(e.astype(BF16), v, preferred_element_type=F32)
            m_ref[p] = m_new

    n_tiles = (t0 + tq - 1) // tk + 1
    lax.fori_loop(0, n_tiles, body, 0)
    for p in range(HEADS_PER_GROUP):
        o_ref[:, p * HEAD_DIM:(p + 1) * HEAD_DIM] = (acc_ref[p] / l_ref[p]).astype(o_ref.dtype)


def _slc_attn(slopes, qkv, sel, blk_act, tq=256, tk=256):
    t = qkv.shape[0]
    tq = min(tq, t)
    tk = min(tk, t)
    nb = sel.shape[2]
    assert blk_act.shape[1] == t // tq and tq % tk == 0
    gw = HEADS_PER_GROUP * HEAD_DIM
    ks_col = (Q_W + 2 * KV_W) // HEAD_DIM
    vs_col = (Q_W + 3 * KV_W) // HEAD_DIM
    flags = jnp.max(blk_act.reshape(N_KV_GROUPS, t // tq, t // tk, tk // SLC_BLOCK), axis=-1)
    flags = (flags > 0.5).astype(jnp.int32).reshape(-1)
    return pl.pallas_call(
        functools.partial(_slc_kernel, tq=tq, tk=tk),
        out_shape=jax.ShapeDtypeStruct((t, Q_W), F32),
        grid_spec=pltpu.PrefetchScalarGridSpec(
            num_scalar_prefetch=2,
            grid=(N_KV_GROUPS, t // tq),
            in_specs=[pl.BlockSpec((tq, gw), lambda g, i, s, f: (i, g)),
                      pl.BlockSpec((t, HEAD_DIM), lambda g, i, s, f: (0, ks_col + g)),
                      pl.BlockSpec((t, HEAD_DIM), lambda g, i, s, f: (0, vs_col + g)),
                      pl.BlockSpec((1, tq, nb), lambda g, i, s, f: (g, i, 0))],
            out_specs=pl.BlockSpec((tq, gw), lambda g, i, s, f: (i, g)),
            scratch_shapes=[pltpu.VMEM((HEADS_PER_GROUP, tq, LANES), F32),
                            pltpu.VMEM((HEADS_PER_GROUP, tq, LANES), F32),
                            pltpu.VMEM((HEADS_PER_GROUP, tq, HEAD_DIM), F32)]),
        compiler_params=_cparams(("parallel", "parallel")),
        name="nsa_selected",
    )(slopes, flags, qkv, qkv, qkv, sel)


def _win_kernel(slopes_ref, q_ref, k0_ref, k1_ref, k2_ref, v0_ref, v1_ref, v2_ref, o_ref, *, tq):
    g = pl.program_id(0)
    i = pl.program_id(1)
    nprev = WINDOW // tq
    kw = (nprev + 1) * tq
    k = jnp.concatenate([k0_ref[...], k1_ref[...], k2_ref[...]], axis=0)
    v = jnp.concatenate([v0_ref[...], v1_ref[...], v2_ref[...]], axis=0)
    r = lax.broadcasted_iota(jnp.int32, (tq, kw), 0)
    c = lax.broadcasted_iota(jnp.int32, (tq, kw), 1)
    rel = WINDOW + r - c
    first_valid = (nprev - i) * tq
    inside = jnp.where(rel >= 0, jnp.where(rel < WINDOW, 1.0, 0.0), 0.0)
    neg = jnp.where(jnp.where(c >= first_valid, inside, 0.0) > 0.5, 0.0, NEG)
    colb = (lax.broadcasted_iota(jnp.int32, (1, kw), 1) - WINDOW).astype(F32)
    for p in range(HEADS_PER_GROUP):
        slope = slopes_ref[g * HEADS_PER_GROUP + p]
        q = q_ref[:, p * HEAD_DIM:(p + 1) * HEAD_DIM]
        s = _nt_dot(q, k) + slope * colb + neg
        e = jnp.exp(s - jnp.max(s, axis=1, keepdims=True))
        o = jnp.dot(e.astype(BF16), v, preferred_element_type=F32) / jnp.sum(e, axis=1, keepdims=True)
        o_ref[:, p * HEAD_DIM:(p + 1) * HEAD_DIM] = o.astype(o_ref.dtype)


def _win_attn(slopes, qkv, tq=256):
    t = qkv.shape[0]
    assert WINDOW % tq == 0 and WINDOW // tq == 2
    gw = HEADS_PER_GROUP * HEAD_DIM
    kw_col = (Q_W + 4 * KV_W) // HEAD_DIM
    vw_col = (Q_W + 5 * KV_W) // HEAD_DIM

    def kv_spec(col, back):
        return pl.BlockSpec((tq, HEAD_DIM), lambda g, i, s: (jnp.maximum(i - back, 0), col + g))

    return pl.pallas_call(
        functools.partial(_win_kernel, tq=tq),
        out_shape=jax.ShapeDtypeStruct((t, Q_W), F32),
        grid_spec=pltpu.PrefetchScalarGridSpec(
            num_scalar_prefetch=1,
            grid=(N_KV_GROUPS, t // tq),
            in_specs=[pl.BlockSpec((tq, gw), lambda g, i, s: (i, g)),
                      kv_spec(kw_col, 2), kv_spec(kw_col, 1), kv_spec(kw_col, 0),
                      kv_spec(vw_col, 2), kv_spec(vw_col, 1), kv_spec(vw_col, 0)],
            out_specs=pl.BlockSpec((tq, gw), lambda g, i, s: (i, g))),
        compiler_params=_cparams(("parallel", "parallel")),
        name="nsa_window",
    )(slopes, qkv, qkv, qkv, qkv, qkv, qkv, qkv)


def _gate_sum_kernel(oc_ref, os_ref, ow_ref, g_ref, o_ref):
    sg = _sigmoid(g_ref[...])
    for h in range(N_MIX_HEADS):
        sl = slice(h * HEAD_DIM, (h + 1) * HEAD_DIM)
        o = (oc_ref[:, sl] * sg[:, 3 * h:3 * h + 1]
             + os_ref[:, sl] * sg[:, 3 * h + 1:3 * h + 2]
             + ow_ref[:, sl] * sg[:, 3 * h + 2:3 * h + 3])
        o_ref[:, sl] = o.astype(o_ref.dtype)


def _gate_sum(o_cmp, o_slc, o_win, gates, tm=512):
    t = o_cmp.shape[0]
    tm = min(tm, t)
    big = pl.BlockSpec((tm, Q_W), lambda i: (i, 0))
    return pl.pallas_call(
        _gate_sum_kernel,
        out_shape=jax.ShapeDtypeStruct((t, Q_W), BF16),
        grid=(t // tm,),
        in_specs=[big, big, big, pl.BlockSpec((tm, GATE_W), lambda i: (i, 0))],
        out_specs=big,
        compiler_params=_cparams(("parallel",)),
        name="nsa_gate_sum",
    )(o_cmp, o_slc, o_win, gates)


def _mem_attn_kernel(q_ref, k_ref, v_ref, g_ref, o_ref):
    q = (_rms(q_ref[...], g_ref[0:1, :]) * SCALE).astype(BF16)
    k = _rms(k_ref[...], g_ref[1:2, :]).astype(BF16)
    s = _nt_dot(q, k)
    e = jnp.exp(s - jnp.max(s, axis=1, keepdims=True))
    p = e * (1.0 / jnp.sum(e, axis=1, keepdims=True))
    o_ref[...] = jnp.dot(p.astype(BF16), v_ref[...].astype(BF16),
                         preferred_element_type=F32).astype(o_ref.dtype)


def _mem_attn(qsrc, q_col, memkv, qk_gain, tq=512):
    t = qsrc.shape[0]
    tq = min(tq, t)
    nm = memkv.shape[0]
    gtab = jnp.zeros((8, HEAD_DIM), F32).at[0:2].set(qk_gain)
    return pl.pallas_call(
        _mem_attn_kernel,
        out_shape=jax.ShapeDtypeStruct((t, MEM_Q_W), BF16),
        grid=(N_MEM_HEADS, t // tq),
        in_specs=[pl.BlockSpec((tq, HEAD_DIM), lambda h, i: (i, q_col + h)),
                  pl.BlockSpec((nm, HEAD_DIM), lambda h, i: (0, h)),
                  pl.BlockSpec((nm, HEAD_DIM), lambda h, i: (0, N_MEM_HEADS + h)),
                  pl.BlockSpec((8, HEAD_DIM), lambda h, i: (0, 0))],
        out_specs=pl.BlockSpec((tq, HEAD_DIM), lambda h, i: (i, h)),
        compiler_params=_cparams(("parallel", "parallel")),
        name="mem_attn",
    )(qsrc, memkv, memkv, gtab)


def _stick_kernel(q_ref, k_ref, v_ref, o_ref, *, tq):
    i = pl.program_id(1)
    q = (q_ref[...] * SCALE).astype(BF16)
    r = lax.broadcasted_iota(jnp.int32, (tq, tq), 0)
    c = lax.broadcasted_iota(jnp.int32, (tq, tq), 1)
    later = jnp.where(r > c, 1.0, 0.0).astype(BF16)

    def tile(k0, tail, acc, diagonal):
        k = k_ref[pl.ds(k0, tq), :]
        v = v_ref[pl.ds(k0, tq), :]
        z = _nt_dot(q, k)
        sp = jnp.maximum(z, 0.0) + jnp.log(1.0 + jnp.exp(-jnp.abs(z)))
        log_keep = jnp.where(r > c, -sp, 0.0) if diagonal else -sp
        hi = log_keep.astype(BF16)
        lo = (log_keep - hi.astype(F32)).astype(BF16)
        after = (jnp.dot(hi, later, preferred_element_type=F32)
                 + jnp.dot(lo, later, preferred_element_type=F32) + tail)
        a = jnp.exp(z - sp + after)
        if diagonal:
            a = jnp.where(r > c, a, 0.0)
        acc = acc + jnp.dot(a.astype(BF16), v, preferred_element_type=F32)
        return tail + jnp.sum(log_keep, axis=1, keepdims=True), acc

    tail, acc = tile(pl.multiple_of(i * tq, tq), jnp.zeros((tq, 1), F32), jnp.zeros((tq, HEAD_DIM), F32), True)

    def live(st):
        return jnp.logical_and(st[0] <= i, st[3] > 0)

    def body(st):
        jj, tail, acc, _ = st
        tail, acc = tile(pl.multiple_of((i - jj) * tq, tq), tail, acc, False)
        return jj + 1, tail, acc, (jnp.max(tail) > STICK_DEAD).astype(jnp.int32)

    st = lax.while_loop(live, body, (jnp.int32(1), tail, acc, jnp.int32(1)))
    o_ref[...] = st[2].astype(o_ref.dtype)


def _stick_attn(pb, kvb, tq=256):
    t = pb.shape[0]
    tq = min(tq, t)
    return pl.pallas_call(
        functools.partial(_stick_kernel, tq=tq),
        out_shape=jax.ShapeDtypeStruct((t, Q_W), BF16),
        grid=(N_MIX_HEADS, t // tq),
        in_specs=[pl.BlockSpec((tq, HEAD_DIM), lambda h, i: (i, h)),
                  pl.BlockSpec((t, HEAD_DIM), lambda h, i: (0, h)),
                  pl.BlockSpec((t, HEAD_DIM), lambda h, i: (0, N_MIX_HEADS + h))],
        out_specs=pl.BlockSpec((tq, HEAD_DIM), lambda h, i: (i, h)),
        compiler_params=_cparams(("parallel", "parallel")),
        name="stick_breaking",
    )(pb, kvb, kvb)


def _router_kernel(x_ref, g_ref, w_ref, b_ref, c_ref):
    h = _rms(x_ref[...], g_ref[...])
    logits = jnp.dot(h, w_ref[...], preferred_element_type=F32, precision=lax.Precision.HIGHEST)
    lane = lax.broadcasted_iota(jnp.int32, logits.shape, 1).astype(F32)
    biased = jnp.where(lane < N_EXPERTS, logits + b_ref[...], -3e38)
    m1 = jnp.max(biased, axis=1, keepdims=True)
    i1 = jnp.min(jnp.where(biased == m1, lane, float(LANES)), axis=1, keepdims=True)
    rest = jnp.where(lane == i1, -3e38, biased)
    m2 = jnp.max(rest, axis=1, keepdims=True)
    i2 = jnp.min(jnp.where(rest == m2, lane, float(LANES)), axis=1, keepdims=True)
    l1 = jnp.sum(jnp.where(lane == i1, logits, 0.0), axis=1, keepdims=True)
    l2 = jnp.sum(jnp.where(lane == i2, logits, 0.0), axis=1, keepdims=True)
    mx = jnp.maximum(l1, l2)
    e1 = jnp.exp(l1 - mx)
    e2 = jnp.exp(l2 - mx)
    inv = 1.0 / (e1 + e2)
    c_ref[...] = (jnp.where(lane == 0.0, i1, 0.0) + jnp.where(lane == 1.0, i2, 0.0)
                  + jnp.where(lane == 2.0, e1 * inv, 0.0) + jnp.where(lane == 3.0, e2 * inv, 0.0))


def _router(x, gain, router_w, router_b, tm=512):
    t, d = x.shape
    tm = min(tm, t)
    w = jnp.zeros((d, LANES), F32).at[:, :N_EXPERTS].set(router_w)
    b = jnp.zeros((1, LANES), F32).at[0, :N_EXPERTS].set(router_b)
    return pl.pallas_call(
        _router_kernel,
        out_shape=jax.ShapeDtypeStruct((t, LANES), F32),
        grid=(t // tm,),
        in_specs=[pl.BlockSpec((tm, d), lambda i: (i, 0)), pl.BlockSpec((1, d), lambda i: (0, 0)),
                  pl.BlockSpec((d, LANES), lambda i: (0, 0)), pl.BlockSpec((1, LANES), lambda i: (0, 0))],
        out_specs=pl.BlockSpec((tm, LANES), lambda i: (i, 0)),
        compiler_params=_cparams(("parallel",)),
        name="moe_router",
    )(x, gain.reshape(1, d), w, b)


def _route_plan(rt, tm):
    t = rt.shape[0]
    nt = 2 * t // tm + N_EXPERTS
    e_flat = rt[:, 0:2].astype(jnp.int32).reshape(-1)
    onehot = (e_flat[:, None] == jnp.arange(N_EXPERTS, dtype=jnp.int32)[None, :]).astype(jnp.int32)
    csum = jnp.cumsum(onehot, axis=0)
    rank = jnp.sum(csum * onehot, axis=1) - 1
    tiles_e = (csum[-1] + tm - 1) // tm
    tile_end = jnp.cumsum(tiles_e)
    dest = (tile_end - tiles_e)[e_flat] * tm + rank
    n_tiles = tile_end[-1]
    tile_idx = jnp.minimum(jnp.arange(nt, dtype=jnp.int32), n_tiles - 1)
    tile_expert = jnp.sum((tile_idx[:, None] >= tile_end[None, :]).astype(jnp.int32), axis=1)
    return dest.astype(jnp.int32), tile_expert.astype(jnp.int32), n_tiles.reshape(1).astype(jnp.int32)


def _dispatch_kernel(dest_ref, nt_ref, x_hbm, g_ref, o_ref, tok_ref, xbuf, sems, *, tm):
    n = pl.program_id(0)
    n_tiles = nt_ref[0]

    def gather(tile, slot):
        base = tile * tm

        def issue(r, c):
            pltpu.make_async_copy(x_hbm.at[pl.ds(tok_ref[base + r], 1), :],
                                  xbuf.at[slot, pl.ds(r, 1), :], sems.at[slot]).start()
            return c

        lax.fori_loop(0, tm, issue, 0, unroll=8)

    @pl.when(n == 0)
    def _():
        def clear(r, c):
            tok_ref[r] = 0
            return c

        lax.fori_loop(0, tok_ref.shape[0], clear, 0, unroll=8)

        def fill(p, c):
            tok_ref[dest_ref[p]] = p // 2
            return c

        lax.fori_loop(0, dest_ref.shape[0], fill, 0, unroll=8)
        gather(0, 0)

    slot = n % 2

    @pl.when(n + 1 < n_tiles)
    def _():
        gather(n + 1, 1 - slot)

    @pl.when(n < n_tiles)
    def _():
        pltpu.make_async_copy(x_hbm.at[pl.ds(0, tm), :], xbuf.at[slot], sems.at[slot]).wait()
        o_ref[...] = _rms(xbuf[slot], g_ref[...]).astype(o_ref.dtype)

    @pl.when(n >= n_tiles)
    def _():
        o_ref[...] = jnp.zeros(o_ref.shape, o_ref.dtype)


def _dispatch(x, gain, dest, n_tiles, tm, nt):
    t, d = x.shape
    return pl.pallas_call(
        functools.partial(_dispatch_kernel, tm=tm),
        out_shape=jax.ShapeDtypeStruct((nt * tm, d), BF16),
        grid_spec=pltpu.PrefetchScalarGridSpec(
            num_scalar_prefetch=2,
            grid=(nt,),
            in_specs=[pl.BlockSpec(memory_space=pl.ANY), pl.BlockSpec((1, d), lambda n, dst, ntl: (0, 0))],
            out_specs=pl.BlockSpec((tm, d), lambda n, dst, ntl: (n, 0)),
            scratch_shapes=[pltpu.SMEM((nt * tm,), jnp.int32), pltpu.VMEM((2, tm, d), F32),
                            pltpu.SemaphoreType.DMA((2,))]),
        compiler_params=_cparams(("arbitrary",)),
        name="moe_dispatch",
    )(dest, n_tiles, x, gain.reshape(1, d))


def _moe_up_kernel(te_ref, nt_ref, x_ref, wg_ref, wu_ref, o_ref, wgc, wuc):
    n = pl.program_id(1)
    prev = te_ref[jnp.maximum(n - 1, 0)]

    @pl.when(jnp.logical_or(n == 0, te_ref[n] != prev))
    def _():
        wgc[...] = wg_ref[0].astype(BF16)
        wuc[...] = wu_ref[0].astype(BF16)

    @pl.when(n < nt_ref[0])
    def _():
        x = x_ref[...]
        g = jnp.dot(x, wgc[...], preferred_element_type=F32)
        u = jnp.dot(x, wuc[...], preferred_element_type=F32)
        o_ref[...] = (g * _sigmoid(g) * u).astype(o_ref.dtype)

    @pl.when(n >= nt_ref[0])
    def _():
        o_ref[...] = jnp.zeros(o_ref.shape, o_ref.dtype)


def _moe_up(xg, w_gu, tile_expert, n_tiles, tm, tf=512):
    p, d = xg.shape
    nt = p // tm
    f = w_gu.shape[2] // 2
    nj = f // tf
    return pl.pallas_call(
        _moe_up_kernel,
        out_shape=jax.ShapeDtypeStruct((p, f), BF16),
        grid_spec=pltpu.PrefetchScalarGridSpec(
            num_scalar_prefetch=2,
            grid=(nj, nt),
            in_specs=[pl.BlockSpec((tm, d), lambda j, n, te, ntl: (jnp.minimum(n, ntl[0] - 1), 0)),
                      pl.BlockSpec((1, d, tf), lambda j, n, te, ntl: (te[n], 0, j)),
                      pl.BlockSpec((1, d, tf), lambda j, n, te, ntl: (te[n], 0, j + nj))],
            out_specs=pl.BlockSpec((tm, tf), lambda j, n, te, ntl: (n, j)),
            scratch_shapes=[pltpu.VMEM((d, tf), BF16), pltpu.VMEM((d, tf), BF16)]),
        compiler_params=_cparams(("arbitrary", "arbitrary")),
        name="moe_up",
    )(tile_expert, n_tiles, xg, w_gu, w_gu)


def _moe_down_kernel(te_ref, nt_ref, a_ref, w_ref, o_ref, wc, *, split):
    n = pl.program_id(1)
    prev = te_ref[jnp.maximum(n - 1, 0) // split]

    @pl.when(jnp.logical_or(n == 0, te_ref[n // split] != prev))
    def _():
        wc[...] = w_ref[0].astype(BF16)

    @pl.when(n < nt_ref[0] * split)
    def _():
        o_ref[...] = jnp.dot(a_ref[...], wc[...], preferred_element_type=F32)

    @pl.when(n >= nt_ref[0] * split)
    def _():
        o_ref[...] = jnp.zeros(o_ref.shape, o_ref.dtype)


def _moe_down(act, w_down, tile_expert, n_tiles, tm, split=2, tn=512):
    p, f = act.shape
    d = w_down.shape[2]
    tr = tm // split
    nr = p // tr

    def row(n, ntl):
        return jnp.minimum(n, ntl[0] * split - 1)

    return pl.pallas_call(
        functools.partial(_moe_down_kernel, split=split),
        out_shape=jax.ShapeDtypeStruct((p, d), F32),
        grid_spec=pltpu.PrefetchScalarGridSpec(
            num_scalar_prefetch=2,
            grid=(d // tn, nr),
            in_specs=[pl.BlockSpec((tr, f), lambda j, n, te, ntl: (row(n, ntl), 0)),
                      pl.BlockSpec((1, f, tn), lambda j, n, te, ntl: (te[n // split], 0, j))],
            out_specs=pl.BlockSpec((tr, tn), lambda j, n, te, ntl: (n, j)),
            scratch_shapes=[pltpu.VMEM((f, tn), BF16)]),
        compiler_params=_cparams(("arbitrary", "arbitrary")),
        name="moe_down",
    )(tile_expert, n_tiles, act, w_down)


def _moe_combine_kernel(dest_ref, x_ref, rt_ref, y_hbm, o_ref, ybuf, sems, *, tm):
    i = pl.program_id(0)

    def gather(tile, slot):
        base = tile * tm

        def issue(r, c):
            for k in range(2):
                pltpu.make_async_copy(y_hbm.at[pl.ds(dest_ref[2 * (base + r) + k], 1), :],
                                      ybuf.at[slot, pl.ds(k * tm + r, 1), :], sems.at[slot]).start()
            return c

        lax.fori_loop(0, tm, issue, 0, unroll=4)

    @pl.when(i == 0)
    def _():
        gather(0, 0)

    slot = i % 2

    @pl.when(i + 1 < pl.num_programs(0))
    def _():
        gather(i + 1, 1 - slot)

    pltpu.make_async_copy(y_hbm.at[pl.ds(0, 2 * tm), :], ybuf.at[slot], sems.at[slot]).wait()
    w = rt_ref[...]
    o_ref[...] = (x_ref[...] + w[:, 2:3] * ybuf[slot, 0:tm, :] + w[:, 3:4] * ybuf[slot, tm:2 * tm, :])


def _moe_combine(x, rt, y, dest, tm=256):
    t, d = x.shape
    tm = min(tm, t)
    return pl.pallas_call(
        functools.partial(_moe_combine_kernel, tm=tm),
        out_shape=jax.ShapeDtypeStruct((t, d), F32),
        grid_spec=pltpu.PrefetchScalarGridSpec(
            num_scalar_prefetch=1,
            grid=(t // tm,),
            in_specs=[pl.BlockSpec((tm, d), lambda i, dst: (i, 0)),
                      pl.BlockSpec((tm, LANES), lambda i, dst: (i, 0)),
                      pl.BlockSpec(memory_space=pl.ANY)],
            out_specs=pl.BlockSpec((tm, d), lambda i, dst: (i, 0)),
            scratch_shapes=[pltpu.VMEM((2, 2 * tm, d), F32), pltpu.SemaphoreType.DMA((2,))]),
        compiler_params=_cparams(("arbitrary",)),
        name="moe_combine",
    )(dest, x, rt, y)


def _moe_block(xs, gain, router_w, router_b, w_gu, w_down, tm=512):
    t = xs.shape[0]
    tm = min(tm, t)
    nt = 2 * t // tm + N_EXPERTS
    rt = _router(xs, gain, router_w, router_b)
    dest, tile_expert, n_tiles = _route_plan(rt, tm)
    xg = _dispatch(xs, gain, dest, n_tiles, tm, nt)
    act = _moe_up(xg, w_gu, tile_expert, n_tiles, tm)
    y = _moe_down(act, w_down, tile_expert, n_tiles, tm)
    return _moe_combine(xs, rt, y, dest)


def _alibi_slopes():
    return jnp.asarray(np.array([2.0 ** (-8.0 * (i + 1) / N_MIX_HEADS) for i in range(N_MIX_HEADS)],
                                dtype=np.float32))


def _importance_map(ncp, nb):
    ratio = SLC_BLOCK // CMP_STRIDE
    c = np.arange(ncp)[:, None]
    b = np.arange(nb)[None, :]
    m = (c >= ratio * b - 1) & (c <= ratio * b + ratio - 1) & (c < ncp - 1)
    return jnp.asarray(m.astype(np.float32), dtype=BF16)


def _compress_blocks(p, t):
    n = t // CMP_STRIDE
    z = p.reshape(n, CMP_STRIDE, 2, N_KV_GROUPS, HEAD_DIM).transpose(2, 3, 0, 1, 4)
    z = z.reshape(2, N_KV_GROUPS, n, CMP_STRIDE * HEAD_DIM)
    nxt = jnp.concatenate([z[:, :, 1:], jnp.zeros_like(z[:, :, :1])], axis=2)
    return jnp.concatenate([z, nxt], axis=3)


def _nsa_mix(p, slopes, q_gain, k_gain, cmp_w, cmp_pos, slc_tq=256):
    t = p.shape[0]
    qkv = _nsa_prep(p, q_gain, k_gain)
    blocks = _compress_blocks(p[:, Q_W:Q_W + 2 * KV_W], t)
    kvc = _compress(blocks, cmp_w, cmp_pos.reshape(2, -1), k_gain[0])
    imap = _importance_map(t // CMP_STRIDE, t // SLC_BLOCK)
    o_cmp, score = _cmp_attn(slopes, qkv, kvc, imap)
    sel, blk_act = _topk_select(score, sub=slc_tq)
    o_slc = _slc_attn(slopes, qkv, sel, blk_act[:, :, 0, :], tq=slc_tq)
    o_win = _win_attn(slopes, qkv)
    gates = p[:, Q_W + 6 * KV_W:Q_W + 6 * KV_W + GATE_W]
    return _gate_sum(o_cmp, o_slc, o_win, gates)


def kernel(x, mem, attn_norm, mem_norm, ffn_norm, w_in_a, cmp_w, cmp_pos, nsa_q_norm, nsa_k_norm, w_in_b,
           kv_norm_b, w_kv_b, w_mem_kv, mem_qk_norm, w_out, ffn_w_gu, ffn_w_down, router_w, router_b,
           moe_w_gu, moe_w_down):
    b, t, d = x.shape
    assert b == 1
    xs = x[0]
    mems = mem[0]
    slopes = _alibi_slopes()

    p = _norm_matmul(xs, attn_norm[0], w_in_a[0], tm=1024, tn=640, name="in_proj_a")
    mix = _nsa_mix(p, slopes, nsa_q_norm[0], nsa_k_norm[0], cmp_w[0], cmp_pos[0])
    memkv = _norm_matmul(mems, mem_norm[0], w_mem_kv[0], tm=256, tn=512, name="mem_kv")
    qm = p[:, Q_W + 6 * KV_W + GATE_W:]
    mo = _mem_attn(qm, 0, memkv, mem_qk_norm[0])
    xs = _out_proj(mix, mo, w_out[0], xs)
    act = _swiglu_up(xs, ffn_norm[0], ffn_w_gu[0])
    xs = _matmul(act, ffn_w_down[0], xs, tm=1024, tn=1024, tk=512, name="ffn_down")

    kvb = _norm_matmul(xs, kv_norm_b, w_kv_b, tm=1024, tn=512, out_dtype=BF16, name="kv_proj_b")
    pb = _norm_matmul(xs, attn_norm[1], w_in_b[0], tm=1024, tn=512, name="in_proj_b")
    mix = _stick_attn(pb, kvb)
    memkv = _norm_matmul(mems, mem_norm[1], w_mem_kv[1], tm=256, tn=512, name="mem_kv")
    mo = _mem_attn(pb, N_MIX_HEADS, memkv, mem_qk_norm[1])
    xs = _out_proj(mix, mo, w_out[1], xs)
    xs = _moe_block(xs, ffn_norm[1], router_w[0], router_b[0], moe_w_gu[0], moe_w_down[0])
    return xs[None]
```

```python
import functools

import numpy as np
import jax
import jax.numpy as jnp
from jax import lax
from jax.experimental import pallas as pl
from jax.experimental.pallas import tpu as pltpu

HEAD_DIM = 128
N_MEM_HEADS = 4
N_MIX_HEADS = 12
N_KV_GROUPS = 3
HEADS_PER_GROUP = 4
CMP_STRIDE = 16
CMP_BLOCK = 32
SLC_BLOCK = 64
SLC_TOPK = 16
WINDOW = 512
N_EXPERTS = 8
Q_W = N_MIX_HEADS * HEAD_DIM
KV_W = N_KV_GROUPS * HEAD_DIM
GATE_W = 3 * N_MIX_HEADS
MEM_Q_W = N_MEM_HEADS * HEAD_DIM
NEG = -1e30
BIG = 1e6
EPS = 1e-6
SCALE = HEAD_DIM ** -0.5
STICK_DEAD = -104.0
LANES = 128
VMEM_LIMIT = 48 * 1024 * 1024
MOE_VMEM_LIMIT = 56 * 1024 * 1024
MOE_SUB = 256
MOE_TILES_PER_GROUP = 9

F32 = jnp.float32
BF16 = jnp.bfloat16


def _cparams(sem):
    return pltpu.CompilerParams(dimension_semantics=sem, vmem_limit_bytes=VMEM_LIMIT)


def _nt_dot(a, b):
    return lax.dot_general(a, b, (((1,), (1,)), ((), ())), preferred_element_type=F32)


def _rms(x, gain):
    return x * lax.rsqrt(jnp.mean(x * x, axis=-1, keepdims=True) + EPS) * gain


def _sigmoid(x):
    return 1.0 / (1.0 + jnp.exp(-x))


def _rmsnorm_cast_kernel(x_ref, g_ref, o_ref):
    o_ref[...] = _rms(x_ref[...], g_ref[...]).astype(o_ref.dtype)


def _rmsnorm_cast(x, gain, tm=512):
    m, d = x.shape
    tm = min(tm, m)
    return pl.pallas_call(
        _rmsnorm_cast_kernel,
        out_shape=jax.ShapeDtypeStruct((m, d), BF16),
        grid=(m // tm,),
        in_specs=[pl.BlockSpec((tm, d), lambda i: (i, 0)), pl.BlockSpec((1, d), lambda i: (0, 0))],
        out_specs=pl.BlockSpec((tm, d), lambda i: (i, 0)),
        compiler_params=_cparams(("parallel",)),
        name="rmsnorm_cast",
    )(x, gain.reshape(1, d))


def _norm_matmul_kernel(x_ref, g_ref, w_ref, o_ref, xn_ref):
    @pl.when(pl.program_id(1) == 0)
    def _():
        xn_ref[...] = _rms(x_ref[...], g_ref[...]).astype(BF16)

    o_ref[...] = jnp.dot(xn_ref[...], w_ref[...].astype(BF16), preferred_element_type=F32).astype(o_ref.dtype)


def _norm_matmul(x, gain, w, *, tm, tn, out_dtype=F32, name="norm_matmul"):
    m, d = x.shape
    n = w.shape[1]
    tm = min(tm, m)
    return pl.pallas_call(
        _norm_matmul_kernel,
        out_shape=jax.ShapeDtypeStruct((m, n), out_dtype),
        grid=(m // tm, pl.cdiv(n, tn)),
        in_specs=[pl.BlockSpec((tm, d), lambda i, j: (i, 0)),
                  pl.BlockSpec((1, d), lambda i, j: (0, 0)),
                  pl.BlockSpec((d, tn), lambda i, j: (0, j))],
        out_specs=pl.BlockSpec((tm, tn), lambda i, j: (i, j)),
        scratch_shapes=[pltpu.VMEM((tm, d), BF16)],
        compiler_params=_cparams(("parallel", "arbitrary")),
        name=name,
    )(x, gain.reshape(1, d), w)


def _matmul_kernel(*refs, nk, has_res):
    if has_res:
        a_ref, w_ref, r_ref, o_ref = refs[:4]
    else:
        a_ref, w_ref, o_ref = refs[:3]
        r_ref = None
    part = jnp.dot(a_ref[...], w_ref[...].astype(BF16), preferred_element_type=F32)
    if nk == 1:
        if has_res:
            part = part + r_ref[...]
        o_ref[...] = part.astype(o_ref.dtype)
        return
    acc_ref = refs[-1]
    k = pl.program_id(2)

    @pl.when(k == 0)
    def _():
        acc_ref[...] = part

    @pl.when(k > 0)
    def _():
        acc_ref[...] += part

    @pl.when(k == nk - 1)
    def _():
        r = acc_ref[...]
        if has_res:
            r = r + r_ref[...]
        o_ref[...] = r.astype(o_ref.dtype)


def _matmul(a, w, res=None, *, tm, tn, tk=None, out_dtype=F32, name="matmul"):
    m, kdim = a.shape
    n = w.shape[1]
    tm = min(tm, m)
    tk = kdim if tk is None else tk
    nk = kdim // tk
    in_specs = [pl.BlockSpec((tm, tk), lambda i, j, k: (i, k)),
                pl.BlockSpec((tk, tn), lambda i, j, k: (k, j))]
    args = [a, w]
    if res is not None:
        in_specs.append(pl.BlockSpec((tm, tn), lambda i, j, k: (i, j)))
        args.append(res)
    scratch = [pltpu.VMEM((tm, tn), F32)] if nk > 1 else []
    return pl.pallas_call(
        functools.partial(_matmul_kernel, nk=nk, has_res=res is not None),
        out_shape=jax.ShapeDtypeStruct((m, n), out_dtype),
        grid=(m // tm, pl.cdiv(n, tn), nk),
        in_specs=in_specs,
        out_specs=pl.BlockSpec((tm, tn), lambda i, j, k: (i, j)),
        scratch_shapes=scratch,
        compiler_params=_cparams(("parallel", "parallel", "arbitrary")),
        name=name,
    )(*args)


def _out_proj_kernel(mix_ref, mo_ref, w1_ref, w2_ref, r_ref, o_ref):
    o_ref[...] = (jnp.dot(mix_ref[...], w1_ref[...].astype(BF16), preferred_element_type=F32)
                  + jnp.dot(mo_ref[...], w2_ref[...].astype(BF16), preferred_element_type=F32)
                  + r_ref[...])


def _out_proj(mix, mo, w, res, tm=1024, tn=512):
    t = mix.shape[0]
    tm = min(tm, t)
    n = w.shape[1]
    assert Q_W % MEM_Q_W == 0
    return pl.pallas_call(
        _out_proj_kernel,
        out_shape=jax.ShapeDtypeStruct((t, n), F32),
        grid=(t // tm, n // tn),
        in_specs=[pl.BlockSpec((tm, Q_W), lambda i, j: (i, 0)),
                  pl.BlockSpec((tm, MEM_Q_W), lambda i, j: (i, 0)),
                  pl.BlockSpec((Q_W, tn), lambda i, j: (0, j)),
                  pl.BlockSpec((MEM_Q_W, tn), lambda i, j: (Q_W // MEM_Q_W, j)),
                  pl.BlockSpec((tm, tn), lambda i, j: (i, j))],
        out_specs=pl.BlockSpec((tm, tn), lambda i, j: (i, j)),
        compiler_params=_cparams(("parallel", "parallel")),
        name="out_proj",
    )(mix, mo, w, w, res)


def _swiglu_kernel(x_ref, gn_ref, wg_ref, wu_ref, o_ref, xn_ref):
    @pl.when(pl.program_id(1) == 0)
    def _():
        xn_ref[...] = _rms(x_ref[...], gn_ref[...]).astype(BF16)

    a = xn_ref[...]
    g = jnp.dot(a, wg_ref[...].astype(BF16), preferred_element_type=F32)
    u = jnp.dot(a, wu_ref[...].astype(BF16), preferred_element_type=F32)
    o_ref[...] = (g * _sigmoid(g) * u).astype(o_ref.dtype)


def _swiglu_up(x, gain, w_gu, *, tm=1024, tn=256):
    m, d = x.shape
    f = w_gu.shape[1] // 2
    tm = min(tm, m)
    nj = f // tn
    return pl.pallas_call(
        _swiglu_kernel,
        out_shape=jax.ShapeDtypeStruct((m, f), BF16),
        grid=(m // tm, nj),
        in_specs=[pl.BlockSpec((tm, d), lambda i, j: (i, 0)),
                  pl.BlockSpec((1, d), lambda i, j: (0, 0)),
                  pl.BlockSpec((d, tn), lambda i, j: (0, j)),
                  pl.BlockSpec((d, tn), lambda i, j: (0, j + nj))],
        out_specs=pl.BlockSpec((tm, tn), lambda i, j: (i, j)),
        scratch_shapes=[pltpu.VMEM((tm, d), BF16)],
        compiler_params=_cparams(("parallel", "arbitrary")),
        name="swiglu_up",
    )(x, gain.reshape(1, d), w_gu, w_gu)


def _nsa_prep_kernel(p_ref, gtab_ref, o_ref):
    for c in range((Q_W + 6 * KV_W) // HEAD_DIM):
        x = p_ref[:, c * HEAD_DIM:(c + 1) * HEAD_DIM]
        if c < N_MIX_HEADS:
            y = _rms(x, gtab_ref[0:1, :]) * SCALE
        elif 18 <= c < 21:
            y = _rms(x, gtab_ref[1:2, :])
        elif 24 <= c < 27:
            y = _rms(x, gtab_ref[2:3, :])
        else:
            y = x
        o_ref[:, c * HEAD_DIM:(c + 1) * HEAD_DIM] = y.astype(o_ref.dtype)


def _nsa_prep(p, q_gain, k_gain, tm=256):
    t = p.shape[0]
    tm = min(tm, t)
    w = Q_W + 6 * KV_W
    gtab = jnp.zeros((8, HEAD_DIM), F32).at[0].set(q_gain).at[1].set(k_gain[1]).at[2].set(k_gain[2])
    return pl.pallas_call(
        _nsa_prep_kernel,
        out_shape=jax.ShapeDtypeStruct((t, w), BF16),
        grid=(t // tm,),
        in_specs=[pl.BlockSpec((tm, w), lambda i: (i, 0)), pl.BlockSpec((8, HEAD_DIM), lambda i: (0, 0))],
        out_specs=pl.BlockSpec((tm, w), lambda i: (i, 0)),
        compiler_params=_cparams(("parallel",)),
        name="nsa_prep",
    )(p, gtab)


def _compress_kernel(b_ref, pos_ref, w_ref, g_ref, o_ref):
    blk = (b_ref[0, 0] + pos_ref[0]).astype(BF16)
    y = jnp.dot(blk, w_ref[0].astype(BF16), preferred_element_type=F32)
    is_k = pl.program_id(0) == 0
    o_ref[0, 0] = jnp.where(is_k, _rms(y, g_ref[...]), y).astype(o_ref.dtype)


def _compress(blocks, cmp_w, cmp_pos, k_gain0):
    _, g, ncp, bw = blocks.shape
    return pl.pallas_call(
        _compress_kernel,
        out_shape=jax.ShapeDtypeStruct((2, g, ncp, HEAD_DIM), BF16),
        grid=(2, g),
        in_specs=[pl.BlockSpec((1, 1, ncp, bw), lambda s, gi: (s, gi, 0, 0)),
                  pl.BlockSpec((1, 1, bw), lambda s, gi: (s, 0, 0)),
                  pl.BlockSpec((1, bw, HEAD_DIM), lambda s, gi: (s, 0, 0)),
                  pl.BlockSpec((1, HEAD_DIM), lambda s, gi: (0, 0))],
        out_specs=pl.BlockSpec((1, 1, ncp, HEAD_DIM), lambda s, gi: (s, gi, 0, 0)),
        compiler_params=_cparams(("parallel", "parallel")),
        name="nsa_compress",
    )(blocks, cmp_pos.reshape(2, 1, bw), cmp_w, k_gain0.reshape(1, HEAD_DIM))


def _cmp_kernel(slopes_ref, q_ref, k_ref, v_ref, imap_ref, o_ref, score_ref, *, tq, ncp):
    g = pl.program_id(0)
    t0 = pl.program_id(1) * tq
    row = lax.broadcasted_iota(jnp.int32, (tq, ncp), 0) + t0
    col = lax.broadcasted_iota(jnp.int32, (tq, ncp), 1)
    c_end = col * CMP_STRIDE + (CMP_BLOCK - 1)
    neg = jnp.where(row >= c_end, 0.0, NEG)
    colb = (lax.broadcasted_iota(jnp.int32, (1, ncp), 1) * CMP_STRIDE + (CMP_BLOCK - 1) - t0).astype(F32)
    has_key = (lax.broadcasted_iota(jnp.int32, (tq, 1), 0) + t0 >= CMP_BLOCK - 1).astype(F32)
    k = k_ref[0, 0]
    v = v_ref[0, 0]
    psum = jnp.zeros((tq, ncp), F32)
    for p in range(HEADS_PER_GROUP):
        slope = slopes_ref[g * HEADS_PER_GROUP + p]
        q = q_ref[:, p * HEAD_DIM:(p + 1) * HEAD_DIM]
        s = _nt_dot(q, k) + slope * colb + neg
        e = jnp.exp(s - jnp.max(s, axis=1, keepdims=True))
        pr = e * (has_key / jnp.sum(e, axis=1, keepdims=True))
        psum = psum + pr
        o_ref[:, p * HEAD_DIM:(p + 1) * HEAD_DIM] = jnp.dot(
            pr.astype(BF16), v, preferred_element_type=F32).astype(o_ref.dtype)

    imap = imap_ref[...]
    hi = psum.astype(BF16)
    r1 = psum - hi.astype(F32)
    mid = r1.astype(BF16)
    lo = (r1 - mid.astype(F32)).astype(BF16)
    imp = (jnp.dot(hi, imap, preferred_element_type=F32)
           + jnp.dot(mid, imap, preferred_element_type=F32)
           + jnp.dot(lo, imap, preferred_element_type=F32))
    nb = imp.shape[1]
    t = lax.broadcasted_iota(jnp.int32, (tq, nb), 0) + t0
    blk = lax.broadcasted_iota(jnp.int32, (tq, nb), 1)
    score_ref[0] = jnp.where(blk == t // SLC_BLOCK, 2 * BIG,
                             jnp.where(blk == 0, BIG,
                                       jnp.where(blk * SLC_BLOCK <= t, imp, -BIG)))


def _cmp_attn(slopes, qkv, kvc, imap, tq=256):
    t = qkv.shape[0]
    tq = min(tq, t)
    ncp = kvc.shape[2]
    nb = t // SLC_BLOCK
    gw = HEADS_PER_GROUP * HEAD_DIM
    return pl.pallas_call(
        functools.partial(_cmp_kernel, tq=tq, ncp=ncp),
        out_shape=(jax.ShapeDtypeStruct((t, Q_W), F32),
                   jax.ShapeDtypeStruct((N_KV_GROUPS, t, nb), F32)),
        grid_spec=pltpu.PrefetchScalarGridSpec(
            num_scalar_prefetch=1,
            grid=(N_KV_GROUPS, t // tq),
            in_specs=[pl.BlockSpec((tq, gw), lambda g, i, s: (i, g)),
                      pl.BlockSpec((1, 1, ncp, HEAD_DIM), lambda g, i, s: (0, g, 0, 0)),
                      pl.BlockSpec((1, 1, ncp, HEAD_DIM), lambda g, i, s: (1, g, 0, 0)),
                      pl.BlockSpec((ncp, nb), lambda g, i, s: (0, 0))],
            out_specs=(pl.BlockSpec((tq, gw), lambda g, i, s: (i, g)),
                       pl.BlockSpec((1, tq, nb), lambda g, i, s: (g, i, 0)))),
        compiler_params=_cparams(("parallel", "parallel")),
        name="nsa_cmp",
    )(slopes, qkv, kvc, kvc, imap)


def _topk_kernel(score_ref, sel_ref, act_ref, *, tm, sub):
    score = score_ref[0]
    nb = score.shape[1]
    blk_f = lax.broadcasted_iota(jnp.int32, (tm, nb), 1).astype(F32)

    def pick_one(_, carry):
        work, sel = carry
        m = jnp.max(work, axis=1, keepdims=True)
        first = jnp.min(jnp.where(work == m, blk_f, float(nb)), axis=1, keepdims=True)
        pick = blk_f == first
        return jnp.where(pick, -3e38, work), jnp.where(pick, 1.0, sel)

    _, sel = lax.fori_loop(0, min(SLC_TOPK, nb), pick_one, (score, jnp.zeros((tm, nb), F32)))
    sel_ref[0] = sel.astype(sel_ref.dtype)
    for a in range(tm // sub):
        act_ref[0, a] = jnp.broadcast_to(jnp.max(sel[a * sub:(a + 1) * sub], axis=0, keepdims=True),
                                         act_ref.shape[2:])


def _topk_select(score, sub, tm=1024):
    g, t, nb = score.shape
    tm = min(tm, t)
    sub = min(sub, tm)
    return pl.pallas_call(
        functools.partial(_topk_kernel, tm=tm, sub=sub),
        out_shape=(jax.ShapeDtypeStruct((g, t, nb), BF16),
                   jax.ShapeDtypeStruct((g, t // sub, 8, nb), F32)),
        grid=(g, t // tm),
        in_specs=[pl.BlockSpec((1, tm, nb), lambda gi, i: (gi, i, 0))],
        out_specs=(pl.BlockSpec((1, tm, nb), lambda gi, i: (gi, i, 0)),
                   pl.BlockSpec((1, tm // sub, 8, nb), lambda gi, i: (gi, i, 0, 0))),
        compiler_params=_cparams(("parallel", "parallel")),
        name="nsa_topk",
    )(score)


def _slc_kernel(slopes_ref, flags_ref, q_ref, k_ref, v_ref, sel_ref, o_ref, m_ref, l_ref, acc_ref, *, tq, tk):
    g = pl.program_id(0)
    i = pl.program_id(1)
    t0 = i * tq
    flag_base = (g * pl.num_programs(1) + i) * (k_ref.shape[0] // tk)
    nb = sel_ref.shape[2]
    m_ref[...] = jnp.full(m_ref.shape, NEG, F32)
    l_ref[...] = jnp.zeros(l_ref.shape, F32)
    acc_ref[...] = jnp.zeros(acc_ref.shape, F32)
    sel = sel_ref[0]
    bpt = tk // SLC_BLOCK

    def body(j, carry):
        @pl.when(flags_ref[flag_base + j] > 0)
        def _():
            tile(j)
        return carry

    def tile(j):
        k0 = pl.multiple_of(j * tk, tk)
        k = k_ref[pl.ds(k0, tk), :]
        v = v_ref[pl.ds(k0, tk), :]
        eb = lax.broadcasted_iota(jnp.int32, (nb, tk), 0)
        ec = lax.broadcasted_iota(jnp.int32, (nb, tk), 1)
        expand = jnp.where(eb == ec // SLC_BLOCK + j * bpt, 1.0, 0.0).astype(BF16)
        selx = jnp.dot(sel, expand, preferred_element_type=F32)
        rel = (t0 - k0) + lax.broadcasted_iota(jnp.int32, (tq, tk), 0) \
            - lax.broadcasted_iota(jnp.int32, (tq, tk), 1)
        neg = jnp.where(jnp.where(rel >= 0, selx, 0.0) > 0.5, 0.0, NEG)
        colb = (lax.broadcasted_iota(jnp.int32, (1, tk), 1) + (k0 - t0)).astype(F32)
        for p in range(HEADS_PER_GROUP):
            slope = slopes_ref[g * HEADS_PER_GROUP + p]
            q = q_ref[:, p * HEAD_DIM:(p + 1) * HEAD_DIM]
            s = _nt_dot(q, k) + slope * colb + neg
            m_old = m_ref[p]
            m_new = jnp.maximum(m_old, jnp.max(s, axis=1, keepdims=True))
            alpha = jnp.exp(m_old - m_new)
            e = jnp.exp(s - jnp.tile(m_new, (1, tk // LANES)))
            l_ref[p] = alpha * l_ref[p] + jnp.sum(e, axis=1, keepdims=True)
            acc_ref[p] = alpha * acc_ref[p] + jnp.dot(e.astype(BF16), v, preferred_element_type=F32)
            m_ref[p] = m_new

    n_tiles = (t0 + tq - 1) // tk + 1
    lax.fori_loop(0, n_tiles, body, 0)
    for p in range(HEADS_PER_GROUP):
        o_ref[:, p * HEAD_DIM:(p + 1) * HEAD_DIM] = (acc_ref[p] / l_ref[p]).astype(o_ref.dtype)


def _slc_attn(slopes, qkv, sel, blk_act, tq=256, tk=256):
    t = qkv.shape[0]
    tq = min(tq, t)
    tk = min(tk, t)
    nb = sel.shape[2]
    assert blk_act.shape[1] == t // tq and tq % tk == 0
    gw = HEADS_PER_GROUP * HEAD_DIM
    ks_col = (Q_W + 2 * KV_W) // HEAD_DIM
    vs_col = (Q_W + 3 * KV_W) // HEAD_DIM
    flags = jnp.max(blk_act.reshape(N_KV_GROUPS, t // tq, t // tk, tk // SLC_BLOCK), axis=-1)
    flags = (flags > 0.5).astype(jnp.int32).reshape(-1)
    return pl.pallas_call(
        functools.partial(_slc_kernel, tq=tq, tk=tk),
        out_shape=jax.ShapeDtypeStruct((t, Q_W), F32),
        grid_spec=pltpu.PrefetchScalarGridSpec(
            num_scalar_prefetch=2,
            grid=(N_KV_GROUPS, t // tq),
            in_specs=[pl.BlockSpec((tq, gw), lambda g, i, s, f: (i, g)),
                      pl.BlockSpec((t, HEAD_DIM), lambda g, i, s, f: (0, ks_col + g)),
                      pl.BlockSpec((t, HEAD_DIM), lambda g, i, s, f: (0, vs_col + g)),
                      pl.BlockSpec((1, tq, nb), lambda g, i, s, f: (g, i, 0))],
            out_specs=pl.BlockSpec((tq, gw), lambda g, i, s, f: (i, g)),
            scratch_shapes=[pltpu.VMEM((HEADS_PER_GROUP, tq, LANES), F32),
                            pltpu.VMEM((HEADS_PER_GROUP, tq, LANES), F32),
                            pltpu.VMEM((HEADS_PER_GROUP, tq, HEAD_DIM), F32)]),
        compiler_params=_cparams(("parallel", "parallel")),
        name="nsa_selected",
    )(slopes, flags, qkv, qkv, qkv, sel)


def _win_kernel(slopes_ref, q_ref, k0_ref, k1_ref, k2_ref, v0_ref, v1_ref, v2_ref, o_ref, *, tq):
    g = pl.program_id(0)
    i = pl.program_id(1)
    nprev = WINDOW // tq
    kw = (nprev + 1) * tq
    k = jnp.concatenate([k0_ref[...], k1_ref[...], k2_ref[...]], axis=0)
    v = jnp.concatenate([v0_ref[...], v1_ref[...], v2_ref[...]], axis=0)
    r = lax.broadcasted_iota(jnp.int32, (tq, kw), 0)
    c = lax.broadcasted_iota(jnp.int32, (tq, kw), 1)
    rel = WINDOW + r - c
    first_valid = (nprev - i) * tq
    inside = jnp.where(rel >= 0, jnp.where(rel < WINDOW, 1.0, 0.0), 0.0)
    neg = jnp.where(jnp.where(c >= first_valid, inside, 0.0) > 0.5, 0.0, NEG)
    colb = (lax.broadcasted_iota(jnp.int32, (1, kw), 1) - WINDOW).astype(F32)
    for p in range(HEADS_PER_GROUP):
        slope = slopes_ref[g * HEADS_PER_GROUP + p]
        q = q_ref[:, p * HEAD_DIM:(p + 1) * HEAD_DIM]
        s = _nt_dot(q, k) + slope * colb + neg
        e = jnp.exp(s - jnp.max(s, axis=1, keepdims=True))
        o = jnp.dot(e.astype(BF16), v, preferred_element_type=F32) / jnp.sum(e, axis=1, keepdims=True)
        o_ref[:, p * HEAD_DIM:(p + 1) * HEAD_DIM] = o.astype(o_ref.dtype)


def _win_attn(slopes, qkv, tq=256):
    t = qkv.shape[0]
    assert WINDOW % tq == 0 and WINDOW // tq == 2
    gw = HEADS_PER_GROUP * HEAD_DIM
    kw_col = (Q_W + 4 * KV_W) // HEAD_DIM
    vw_col = (Q_W + 5 * KV_W) // HEAD_DIM

    def kv_spec(col, back):
        return pl.BlockSpec((tq, HEAD_DIM), lambda g, i, s: (jnp.maximum(i - back, 0), col + g))

    return pl.pallas_call(
        functools.partial(_win_kernel, tq=tq),
        out_shape=jax.ShapeDtypeStruct((t, Q_W), F32),
        grid_spec=pltpu.PrefetchScalarGridSpec(
            num_scalar_prefetch=1,
            grid=(N_KV_GROUPS, t // tq),
            in_specs=[pl.BlockSpec((tq, gw), lambda g, i, s: (i, g)),
                      kv_spec(kw_col, 2), kv_spec(kw_col, 1), kv_spec(kw_col, 0),
                      kv_spec(vw_col, 2), kv_spec(vw_col, 1), kv_spec(vw_col, 0)],
            out_specs=pl.BlockSpec((tq, gw), lambda g, i, s: (i, g))),
        compiler_params=_cparams(("parallel", "parallel")),
        name="nsa_window",
    )(slopes, qkv, qkv, qkv, qkv, qkv, qkv, qkv)


def _gate_sum_kernel(oc_ref, os_ref, ow_ref, g_ref, o_ref):
    sg = _sigmoid(g_ref[...])
    for h in range(N_MIX_HEADS):
        sl = slice(h * HEAD_DIM, (h + 1) * HEAD_DIM)
        o = (oc_ref[:, sl] * sg[:, 3 * h:3 * h + 1]
             + os_ref[:, sl] * sg[:, 3 * h + 1:3 * h + 2]
             + ow_ref[:, sl] * sg[:, 3 * h + 2:3 * h + 3])
        o_ref[:, sl] = o.astype(o_ref.dtype)


def _gate_sum(o_cmp, o_slc, o_win, gates, tm=512):
    t = o_cmp.shape[0]
    tm = min(tm, t)
    big = pl.BlockSpec((tm, Q_W), lambda i: (i, 0))
    return pl.pallas_call(
        _gate_sum_kernel,
        out_shape=jax.ShapeDtypeStruct((t, Q_W), BF16),
        grid=(t // tm,),
        in_specs=[big, big, big, pl.BlockSpec((tm, GATE_W), lambda i: (i, 0))],
        out_specs=big,
        compiler_params=_cparams(("parallel",)),
        name="nsa_gate_sum",
    )(o_cmp, o_slc, o_win, gates)


def _mem_attn_kernel(q_ref, k_ref, v_ref, g_ref, o_ref):
    q = (_rms(q_ref[...], g_ref[0:1, :]) * SCALE).astype(BF16)
    k = _rms(k_ref[...], g_ref[1:2, :]).astype(BF16)
    s = _nt_dot(q, k)
    e = jnp.exp(s - jnp.max(s, axis=1, keepdims=True))
    p = e * (1.0 / jnp.sum(e, axis=1, keepdims=True))
    o_ref[...] = jnp.dot(p.astype(BF16), v_ref[...].astype(BF16),
                         preferred_element_type=F32).astype(o_ref.dtype)


def _mem_attn(qsrc, q_col, memkv, qk_gain, tq=512):
    t = qsrc.shape[0]
    tq = min(tq, t)
    nm = memkv.shape[0]
    gtab = jnp.zeros((8, HEAD_DIM), F32).at[0:2].set(qk_gain)
    return pl.pallas_call(
        _mem_attn_kernel,
        out_shape=jax.ShapeDtypeStruct((t, MEM_Q_W), BF16),
        grid=(N_MEM_HEADS, t // tq),
        in_specs=[pl.BlockSpec((tq, HEAD_DIM), lambda h, i: (i, q_col + h)),
                  pl.BlockSpec((nm, HEAD_DIM), lambda h, i: (0, h)),
                  pl.BlockSpec((nm, HEAD_DIM), lambda h, i: (0, N_MEM_HEADS + h)),
                  pl.BlockSpec((8, HEAD_DIM), lambda h, i: (0, 0))],
        out_specs=pl.BlockSpec((tq, HEAD_DIM), lambda h, i: (i, h)),
        compiler_params=_cparams(("parallel", "parallel")),
        name="mem_attn",
    )(qsrc, memkv, memkv, gtab)


def _stick_kernel(q_ref, k_ref, v_ref, o_ref, *, tq, hp):
    i = pl.program_id(1)
    r = lax.broadcasted_iota(jnp.int32, (tq, tq), 0)
    c = lax.broadcasted_iota(jnp.int32, (tq, tq), 1)
    later = jnp.where(r > c, 1.0, 0.0).astype(BF16)
    qs = [(q_ref[:, h * HEAD_DIM:(h + 1) * HEAD_DIM] * SCALE).astype(BF16) for h in range(hp)]

    def tile(h, k0, tail, acc, diagonal):
        k = k_ref[pl.ds(k0, tq), h * HEAD_DIM:(h + 1) * HEAD_DIM]
        v = v_ref[pl.ds(k0, tq), h * HEAD_DIM:(h + 1) * HEAD_DIM]
        z = _nt_dot(qs[h], k)
        sp = jnp.maximum(z, 0.0) + jnp.log(1.0 + jnp.exp(-jnp.abs(z)))
        log_keep = jnp.where(r > c, -sp, 0.0) if diagonal else -sp
        hi = log_keep.astype(BF16)
        lo = (log_keep - hi.astype(F32)).astype(BF16)
        after = (jnp.dot(hi, later, preferred_element_type=F32)
                 + jnp.dot(lo, later, preferred_element_type=F32) + tail)
        a = jnp.exp(z - sp + after)
        if diagonal:
            a = jnp.where(r > c, a, 0.0)
        acc = acc + jnp.dot(a.astype(BF16), v, preferred_element_type=F32)
        return tail + jnp.sum(log_keep, axis=1, keepdims=True), acc

    k_diag = pl.multiple_of(i * tq, tq)
    state = [tile(h, k_diag, jnp.zeros((tq, 1), F32), jnp.zeros((tq, HEAD_DIM), F32), True) for h in range(hp)]

    def live(st):
        return jnp.logical_and(st[0] <= i, st[1] > 0)

    def body(st):
        k0 = pl.multiple_of((i - st[0]) * tq, tq)
        new = [tile(h, k0, st[2 + 2 * h], st[3 + 2 * h], False) for h in range(hp)]
        top = functools.reduce(jnp.maximum, [jnp.max(tl) for tl, _ in new])
        return (st[0] + 1, (top > STICK_DEAD).astype(jnp.int32)) + tuple(x for pair in new for x in pair)

    st = lax.while_loop(live, body, (jnp.int32(1), jnp.int32(1)) + tuple(x for pair in state for x in pair))
    for h in range(hp):
        o_ref[:, h * HEAD_DIM:(h + 1) * HEAD_DIM] = st[3 + 2 * h].astype(o_ref.dtype)


def _stick_attn(pb, kvb, tq=256, hp=2):
    t = pb.shape[0]
    tq = min(tq, t)
    hw = hp * HEAD_DIM
    return pl.pallas_call(
        functools.partial(_stick_kernel, tq=tq, hp=hp),
        out_shape=jax.ShapeDtypeStruct((t, Q_W), BF16),
        grid=(N_MIX_HEADS // hp, t // tq),
        in_specs=[pl.BlockSpec((tq, hw), lambda h, i: (i, h)),
                  pl.BlockSpec((t, hw), lambda h, i: (0, h)),
                  pl.BlockSpec((t, hw), lambda h, i: (0, N_MIX_HEADS // hp + h))],
        out_specs=pl.BlockSpec((tq, hw), lambda h, i: (i, h)),
        compiler_params=_cparams(("parallel", "parallel")),
        name="stick_breaking",
    )(pb, kvb, kvb)


def _router_kernel(x_ref, g_ref, w_ref, b_ref, c_ref):
    h = _rms(x_ref[...], g_ref[...])
    logits = jnp.dot(h, w_ref[...], preferred_element_type=F32, precision=lax.Precision.HIGHEST)
    lane = lax.broadcasted_iota(jnp.int32, logits.shape, 1).astype(F32)
    biased = jnp.where(lane < N_EXPERTS, logits + b_ref[...], -3e38)
    m1 = jnp.max(biased, axis=1, keepdims=True)
    i1 = jnp.min(jnp.where(biased == m1, lane, float(LANES)), axis=1, keepdims=True)
    rest = jnp.where(lane == i1, -3e38, biased)
    m2 = jnp.max(rest, axis=1, keepdims=True)
    i2 = jnp.min(jnp.where(rest == m2, lane, float(LANES)), axis=1, keepdims=True)
    l1 = jnp.sum(jnp.where(lane == i1, logits, 0.0), axis=1, keepdims=True)
    l2 = jnp.sum(jnp.where(lane == i2, logits, 0.0), axis=1, keepdims=True)
    mx = jnp.maximum(l1, l2)
    e1 = jnp.exp(l1 - mx)
    e2 = jnp.exp(l2 - mx)
    inv = 1.0 / (e1 + e2)
    c_ref[...] = (jnp.where(lane == 0.0, i1, 0.0) + jnp.where(lane == 1.0, i2, 0.0)
                  + jnp.where(lane == 2.0, e1 * inv, 0.0) + jnp.where(lane == 3.0, e2 * inv, 0.0))


def _router(x, gain, router_w, router_b, tm=512):
    t, d = x.shape
    tm = min(tm, t)
    w = jnp.zeros((d, LANES), F32).at[:, :N_EXPERTS].set(router_w)
    b = jnp.zeros((1, LANES), F32).at[0, :N_EXPERTS].set(router_b)
    return pl.pallas_call(
        _router_kernel,
        out_shape=jax.ShapeDtypeStruct((t, LANES), F32),
        grid=(t // tm,),
        in_specs=[pl.BlockSpec((tm, d), lambda i: (i, 0)), pl.BlockSpec((1, d), lambda i: (0, 0)),
                  pl.BlockSpec((d, LANES), lambda i: (0, 0)), pl.BlockSpec((1, LANES), lambda i: (0, 0))],
        out_specs=pl.BlockSpec((tm, LANES), lambda i: (i, 0)),
        compiler_params=_cparams(("parallel",)),
        name="moe_router",
    )(x, gain.reshape(1, d), w, b)


def _route_plan(rt, sub, per_group):
    t = rt.shape[0]
    i32 = jnp.int32
    e_flat = rt[:, 0:2].astype(i32).reshape(-1)
    onehot = (e_flat[:, None] == jnp.arange(N_EXPERTS, dtype=i32)[None, :]).astype(i32)
    csum = jnp.cumsum(onehot, axis=0)
    rank = jnp.sum(csum * onehot, axis=1) - 1
    tiles_e = (csum[-1] + sub - 1) // sub
    tile_end = jnp.cumsum(tiles_e)
    tile_start = tile_end - tiles_e
    dest = tile_start[e_flat] * sub + rank
    groups_e = (tiles_e + per_group - 1) // per_group
    group_end = jnp.cumsum(groups_e)
    n_groups = group_end[-1]
    max_groups = (2 * t // sub + N_EXPERTS) // per_group + N_EXPERTS
    gidx = jnp.arange(max_groups, dtype=i32)
    gcl = jnp.minimum(gidx, n_groups - 1)
    g_expert = jnp.sum((gcl[:, None] >= group_end[None, :]).astype(i32), axis=1)
    k_in_e = gcl - (group_end - groups_e)[g_expert]
    g_start = (tile_start[g_expert] + k_in_e * per_group) * sub
    g_tiles = jnp.where(gidx < n_groups, jnp.clip(tiles_e[g_expert] - k_in_e * per_group, 0, per_group), 0)
    return (dest.astype(i32), tile_end[-1].reshape(1).astype(i32),
            g_expert.astype(i32), g_start.astype(i32), g_tiles.astype(i32))


def _dispatch_kernel(dest_ref, nt_ref, x_hbm, g_ref, o_ref, tok_ref, xbuf, sems, *, tm):
    n = pl.program_id(0)
    n_tiles = nt_ref[0]

    def gather(tile, slot):
        base = tile * tm

        def issue(r, c):
            pltpu.make_async_copy(x_hbm.at[pl.ds(tok_ref[base + r], 1), :],
                                  xbuf.at[slot, pl.ds(r, 1), :], sems.at[slot]).start()
            return c

        lax.fori_loop(0, tm, issue, 0, unroll=8)

    @pl.when(n == 0)
    def _():
        def clear(r, c):
            tok_ref[r] = 0
            return c

        lax.fori_loop(0, tok_ref.shape[0], clear, 0, unroll=8)

        def fill(p, c):
            tok_ref[dest_ref[p]] = p // 2
            return c

        lax.fori_loop(0, dest_ref.shape[0], fill, 0, unroll=8)
        gather(0, 0)

    slot = n % 2

    @pl.when(n + 1 < n_tiles)
    def _():
        gather(n + 1, 1 - slot)

    @pl.when(n < n_tiles)
    def _():
        pltpu.make_async_copy(x_hbm.at[pl.ds(0, tm), :], xbuf.at[slot], sems.at[slot]).wait()
        o_ref[...] = _rms(xbuf[slot], g_ref[...]).astype(o_ref.dtype)

    @pl.when(n >= n_tiles)
    def _():
        o_ref[...] = jnp.zeros(o_ref.shape, o_ref.dtype)


def _dispatch(x, gain, dest, n_tiles, tm, nt):
    t, d = x.shape
    return pl.pallas_call(
        functools.partial(_dispatch_kernel, tm=tm),
        out_shape=jax.ShapeDtypeStruct((nt * tm, d), BF16),
        grid_spec=pltpu.PrefetchScalarGridSpec(
            num_scalar_prefetch=2,
            grid=(nt,),
            in_specs=[pl.BlockSpec(memory_space=pl.ANY), pl.BlockSpec((1, d), lambda n, dst, ntl: (0, 0))],
            out_specs=pl.BlockSpec((tm, d), lambda n, dst, ntl: (n, 0)),
            scratch_shapes=[pltpu.SMEM((nt * tm,), jnp.int32), pltpu.VMEM((2, tm, d), F32),
                            pltpu.SemaphoreType.DMA((2,))]),
        compiler_params=_cparams(("arbitrary",)),
        name="moe_dispatch",
    )(dest, n_tiles, x, gain.reshape(1, d))


def _experts_kernel(ge_ref, gs_ref, gn_ref, nt_ref, xg_hbm, wg_ref, wu_ref, wd_ref, y_hbm,
                    xbuf, yacc, wgc, wuc, wdc, sem_in, sem_out, *, sub):
    s = pl.program_id(0)
    j = pl.program_id(1)
    n_sub = gn_ref[s]
    start = pl.multiple_of(gs_ref[s], sub)
    rows = xbuf.shape[0]

    @pl.when(jnp.logical_and(s == 0, j == 0))
    def _():
        yacc[0:sub, :] = jnp.zeros((sub, yacc.shape[1]), F32)
        first, last = nt_ref[0], y_hbm.shape[0] // sub

        def put(r, c):
            pltpu.make_async_copy(yacc.at[pl.ds(0, sub), :],
                                  y_hbm.at[pl.ds(pl.multiple_of(r * sub, sub), sub), :], sem_out).start()
            return c

        def done(r, c):
            pltpu.make_async_copy(yacc.at[pl.ds(0, sub), :], y_hbm.at[pl.ds(0, sub), :], sem_out).wait()
            return c

        lax.fori_loop(first, last, put, 0)
        lax.fori_loop(first, last, done, 0)

    @pl.when(n_sub > 0)
    def _():
        @pl.when(j == 0)
        def _():
            cp = pltpu.make_async_copy(xg_hbm.at[pl.ds(start, rows), :], xbuf, sem_in)
            cp.start()
            yacc[...] = jnp.zeros(yacc.shape, F32)
            cp.wait()

        wgc[...] = wg_ref[0].astype(BF16)
        wuc[...] = wu_ref[0].astype(BF16)
        wdc[...] = wd_ref[0].astype(BF16)

        def sub_tile(r, c):
            r0 = pl.multiple_of(r * sub, sub)
            x = xbuf[pl.ds(r0, sub), :]
            g = jnp.dot(x, wgc[...], preferred_element_type=F32)
            u = jnp.dot(x, wuc[...], preferred_element_type=F32)
            a = (g * _sigmoid(g) * u).astype(BF16)
            yacc[pl.ds(r0, sub), :] += jnp.dot(a, wdc[...], preferred_element_type=F32)
            return c

        lax.fori_loop(0, n_sub, sub_tile, 0)

        @pl.when(j == pl.num_programs(1) - 1)
        def _():
            def put(r, c):
                r0 = pl.multiple_of(r * sub, sub)
                pltpu.make_async_copy(yacc.at[pl.ds(r0, sub), :], y_hbm.at[pl.ds(start + r0, sub), :],
                                      sem_out).start()
                return c

            def done(r, c):
                pltpu.make_async_copy(yacc.at[pl.ds(0, sub), :], y_hbm.at[pl.ds(0, sub), :], sem_out).wait()
                return c

            lax.fori_loop(0, n_sub, put, 0)
            lax.fori_loop(0, n_sub, done, 0)


def _moe_experts(xg, w_gu, w_down, g_expert, g_start, g_tiles, n_tiles, *, sub, per_group, tf=256):
    p, d = xg.shape
    f = w_down.shape[1]
    nj = f // tf
    rows = sub * per_group
    n_groups = g_expert.shape[0]

    def jcl(s, j, gn):
        return jnp.where(gn[s] > 0, j, nj - 1)

    return pl.pallas_call(
        functools.partial(_experts_kernel, sub=sub),
        out_shape=jax.ShapeDtypeStruct((p, d), F32),
        grid_spec=pltpu.PrefetchScalarGridSpec(
            num_scalar_prefetch=4,
            grid=(n_groups, nj),
            in_specs=[pl.BlockSpec(memory_space=pl.ANY),
                      pl.BlockSpec((1, d, tf), lambda s, j, ge, gs, gn, nt: (ge[s], 0, jcl(s, j, gn))),
                      pl.BlockSpec((1, d, tf), lambda s, j, ge, gs, gn, nt: (ge[s], 0, jcl(s, j, gn) + nj)),
                      pl.BlockSpec((1, tf, d), lambda s, j, ge, gs, gn, nt: (ge[s], jcl(s, j, gn), 0))],
            out_specs=pl.BlockSpec(memory_space=pl.ANY),
            scratch_shapes=[pltpu.VMEM((rows, d), BF16), pltpu.VMEM((rows, d), F32),
                            pltpu.VMEM((d, tf), BF16), pltpu.VMEM((d, tf), BF16), pltpu.VMEM((tf, d), BF16),
                            pltpu.SemaphoreType.DMA(()), pltpu.SemaphoreType.DMA(())]),
        compiler_params=pltpu.CompilerParams(dimension_semantics=("arbitrary", "arbitrary"),
                                             vmem_limit_bytes=MOE_VMEM_LIMIT),
        name="moe_experts",
    )(g_expert, g_start, g_tiles, n_tiles, xg, w_gu, w_gu, w_down)


def _moe_combine_kernel(dest_ref, x_ref, rt_ref, y_hbm, o_ref, ybuf, sems, *, tm):
    i = pl.program_id(0)

    def gather(tile, slot):
        base = tile * tm

        def issue(r, c):
            for k in range(2):
                pltpu.make_async_copy(y_hbm.at[pl.ds(dest_ref[2 * (base + r) + k], 1), :],
                                      ybuf.at[slot, pl.ds(k * tm + r, 1), :], sems.at[slot]).start()
            return c

        lax.fori_loop(0, tm, issue, 0, unroll=4)

    @pl.when(i == 0)
    def _():
        gather(0, 0)

    slot = i % 2

    @pl.when(i + 1 < pl.num_programs(0))
    def _():
        gather(i + 1, 1 - slot)

    pltpu.make_async_copy(y_hbm.at[pl.ds(0, 2 * tm), :], ybuf.at[slot], sems.at[slot]).wait()
    w = rt_ref[...]
    o_ref[...] = (x_ref[...] + w[:, 2:3] * ybuf[slot, 0:tm, :] + w[:, 3:4] * ybuf[slot, tm:2 * tm, :])


def _moe_combine(x, rt, y, dest, tm=256):
    t, d = x.shape
    tm = min(tm, t)
    return pl.pallas_call(
        functools.partial(_moe_combine_kernel, tm=tm),
        out_shape=jax.ShapeDtypeStruct((t, d), F32),
        grid_spec=pltpu.PrefetchScalarGridSpec(
            num_scalar_prefetch=1,
            grid=(t // tm,),
            in_specs=[pl.BlockSpec((tm, d), lambda i, dst: (i, 0)),
                      pl.BlockSpec((tm, LANES), lambda i, dst: (i, 0)),
                      pl.BlockSpec(memory_space=pl.ANY)],
            out_specs=pl.BlockSpec((tm, d), lambda i, dst: (i, 0)),
            scratch_shapes=[pltpu.VMEM((2, 2 * tm, d), F32), pltpu.SemaphoreType.DMA((2,))]),
        compiler_params=_cparams(("arbitrary",)),
        name="moe_combine",
    )(dest, x, rt, y)


def _moe_block(xs, gain, router_w, router_b, w_gu, w_down):
    t = xs.shape[0]
    sub = min(MOE_SUB, t)
    nt = 2 * t // sub + N_EXPERTS + MOE_TILES_PER_GROUP
    rt = _router(xs, gain, router_w, router_b)
    dest, n_tiles, g_expert, g_start, g_tiles = _route_plan(rt, sub, MOE_TILES_PER_GROUP)
    xg = _dispatch(xs, gain, dest, n_tiles, sub, nt)
    y = _moe_experts(xg, w_gu, w_down, g_expert, g_start, g_tiles, n_tiles,
                     sub=sub, per_group=MOE_TILES_PER_GROUP)
    return _moe_combine(xs, rt, y, dest)


def _alibi_slopes():
    return jnp.asarray(np.array([2.0 ** (-8.0 * (i + 1) / N_MIX_HEADS) for i in range(N_MIX_HEADS)],
                                dtype=np.float32))


def _importance_map(ncp, nb):
    ratio = SLC_BLOCK // CMP_STRIDE
    c = np.arange(ncp)[:, None]
    b = np.arange(nb)[None, :]
    m = (c >= ratio * b - 1) & (c <= ratio * b + ratio - 1) & (c < ncp - 1)
    return jnp.asarray(m.astype(np.float32), dtype=BF16)


def _compress_blocks(p, t):
    n = t // CMP_STRIDE
    z = p.reshape(n, CMP_STRIDE, 2, N_KV_GROUPS, HEAD_DIM).transpose(2, 3, 0, 1, 4)
    z = z.reshape(2, N_KV_GROUPS, n, CMP_STRIDE * HEAD_DIM)
    nxt = jnp.concatenate([z[:, :, 1:], jnp.zeros_like(z[:, :, :1])], axis=2)
    return jnp.concatenate([z, nxt], axis=3)


def _nsa_mix(p, slopes, q_gain, k_gain, cmp_w, cmp_pos, slc_tq=256):
    t = p.shape[0]
    qkv = _nsa_prep(p, q_gain, k_gain)
    blocks = _compress_blocks(p[:, Q_W:Q_W + 2 * KV_W], t)
    kvc = _compress(blocks, cmp_w, cmp_pos.reshape(2, -1), k_gain[0])
    imap = _importance_map(t // CMP_STRIDE, t // SLC_BLOCK)
    o_cmp, score = _cmp_attn(slopes, qkv, kvc, imap)
    sel, blk_act = _topk_select(score, sub=slc_tq)
    o_slc = _slc_attn(slopes, qkv, sel, blk_act[:, :, 0, :], tq=slc_tq)
    o_win = _win_attn(slopes, qkv)
    gates = p[:, Q_W + 6 * KV_W:Q_W + 6 * KV_W + GATE_W]
    return _gate_sum(o_cmp, o_slc, o_win, gates)


def kernel(x, mem, attn_norm, mem_norm, ffn_norm, w_in_a, cmp_w, cmp_pos, nsa_q_norm, nsa_k_norm, w_in_b,
           kv_norm_b, w_kv_b, w_mem_kv, mem_qk_norm, w_out, ffn_w_gu, ffn_w_down, router_w, router_b,
           moe_w_gu, moe_w_down):
    b, t, d = x.shape
    assert b == 1
    xs = x[0]
    mems = mem[0]
    slopes = _alibi_slopes()

    p = _norm_matmul(xs, attn_norm[0], w_in_a[0], tm=1024, tn=640, name="in_proj_a")
    mix = _nsa_mix(p, slopes, nsa_q_norm[0], nsa_k_norm[0], cmp_w[0], cmp_pos[0])
    memkv = _norm_matmul(mems, mem_norm[0], w_mem_kv[0], tm=256, tn=512, name="mem_kv")
    qm = p[:, Q_W + 6 * KV_W + GATE_W:]
    mo = _mem_attn(qm, 0, memkv, mem_qk_norm[0])
    xs = _out_proj(mix, mo, w_out[0], xs)
    act = _swiglu_up(xs, ffn_norm[0], ffn_w_gu[0])
    xs = _matmul(act, ffn_w_down[0], xs, tm=1024, tn=1024, tk=512, name="ffn_down")

    kvb = _norm_matmul(xs, kv_norm_b, w_kv_b, tm=1024, tn=512, out_dtype=BF16, name="kv_proj_b")
    pb = _norm_matmul(xs, attn_norm[1], w_in_b[0], tm=1024, tn=512, name="in_proj_b")
    mix = _stick_attn(pb, kvb)
    memkv = _norm_matmul(mems, mem_norm[1], w_mem_kv[1], tm=256, tn=512, name="mem_kv")
    mo = _mem_attn(pb, N_MIX_HEADS, memkv, mem_qk_norm[1])
    xs = _out_proj(mix, mo, w_out[1], xs)
    xs = _moe_block(xs, ffn_norm[1], router_w[0], router_b[0], moe_w_gu[0], moe_w_down[0])
    return xs[None]
```

```python
import functools

import numpy as np
import jax
import jax.numpy as jnp
from jax import lax
from jax.experimental import pallas as pl
from jax.experimental.pallas import tpu as pltpu

HEAD_DIM = 128
N_MEM_HEADS = 4
N_MIX_HEADS = 12
N_KV_GROUPS = 3
HEADS_PER_GROUP = 4
CMP_STRIDE = 16
CMP_BLOCK = 32
SLC_BLOCK = 64
SLC_TOPK = 16
WINDOW = 512
N_EXPERTS = 8
Q_W = N_MIX_HEADS * HEAD_DIM
KV_W = N_KV_GROUPS * HEAD_DIM
GATE_W = 3 * N_MIX_HEADS
MEM_Q_W = N_MEM_HEADS * HEAD_DIM
NEG = -1e30
BIG = 1e6
EPS = 1e-6
SCALE = HEAD_DIM ** -0.5
STICK_DEAD = -104.0
LANES = 128
VMEM_LIMIT = 48 * 1024 * 1024
MOE_VMEM_LIMIT = 56 * 1024 * 1024
MOE_SUB = 256
MOE_TILES_PER_GROUP = 9

F32 = jnp.float32
BF16 = jnp.bfloat16


def _cparams(sem):
    return pltpu.CompilerParams(dimension_semantics=sem, vmem_limit_bytes=VMEM_LIMIT)


def _nt_dot(a, b):
    return lax.dot_general(a, b, (((1,), (1,)), ((), ())), preferred_element_type=F32)


def _rms(x, gain):
    return x * lax.rsqrt(jnp.mean(x * x, axis=-1, keepdims=True) + EPS) * gain


def _sigmoid(x):
    return 1.0 / (1.0 + jnp.exp(-x))


def _rmsnorm_cast_kernel(x_ref, g_ref, o_ref):
    o_ref[...] = _rms(x_ref[...], g_ref[...]).astype(o_ref.dtype)


def _rmsnorm_cast(x, gain, tm=512):
    m, d = x.shape
    tm = min(tm, m)
    return pl.pallas_call(
        _rmsnorm_cast_kernel,
        out_shape=jax.ShapeDtypeStruct((m, d), BF16),
        grid=(m // tm,),
        in_specs=[pl.BlockSpec((tm, d), lambda i: (i, 0)), pl.BlockSpec((1, d), lambda i: (0, 0))],
        out_specs=pl.BlockSpec((tm, d), lambda i: (i, 0)),
        compiler_params=_cparams(("parallel",)),
        name="rmsnorm_cast",
    )(x, gain.reshape(1, d))


def _norm_matmul_kernel(x_ref, g_ref, w_ref, o_ref, xn_ref):
    @pl.when(pl.program_id(1) == 0)
    def _():
        xn_ref[...] = _rms(x_ref[...], g_ref[...]).astype(BF16)

    o_ref[...] = jnp.dot(xn_ref[...], w_ref[...].astype(BF16), preferred_element_type=F32).astype(o_ref.dtype)


def _norm_matmul(x, gain, w, *, tm, tn, out_dtype=F32, name="norm_matmul"):
    m, d = x.shape
    n = w.shape[1]
    tm = min(tm, m)
    return pl.pallas_call(
        _norm_matmul_kernel,
        out_shape=jax.ShapeDtypeStruct((m, n), out_dtype),
        grid=(m // tm, pl.cdiv(n, tn)),
        in_specs=[pl.BlockSpec((tm, d), lambda i, j: (i, 0)),
                  pl.BlockSpec((1, d), lambda i, j: (0, 0)),
                  pl.BlockSpec((d, tn), lambda i, j: (0, j))],
        out_specs=pl.BlockSpec((tm, tn), lambda i, j: (i, j)),
        scratch_shapes=[pltpu.VMEM((tm, d), BF16)],
        compiler_params=_cparams(("parallel", "arbitrary")),
        name=name,
    )(x, gain.reshape(1, d), w)


def _matmul_kernel(*refs, nk, has_res):
    if has_res:
        a_ref, w_ref, r_ref, o_ref = refs[:4]
    else:
        a_ref, w_ref, o_ref = refs[:3]
        r_ref = None
    part = jnp.dot(a_ref[...], w_ref[...].astype(BF16), preferred_element_type=F32)
    if nk == 1:
        if has_res:
            part = part + r_ref[...]
        o_ref[...] = part.astype(o_ref.dtype)
        return
    acc_ref = refs[-1]
    k = pl.program_id(2)

    @pl.when(k == 0)
    def _():
        acc_ref[...] = part

    @pl.when(k > 0)
    def _():
        acc_ref[...] += part

    @pl.when(k == nk - 1)
    def _():
        r = acc_ref[...]
        if has_res:
            r = r + r_ref[...]
        o_ref[...] = r.astype(o_ref.dtype)


def _matmul(a, w, res=None, *, tm, tn, tk=None, out_dtype=F32, name="matmul"):
    m, kdim = a.shape
    n = w.shape[1]
    tm = min(tm, m)
    tk = kdim if tk is None else tk
    nk = kdim // tk
    in_specs = [pl.BlockSpec((tm, tk), lambda i, j, k: (i, k)),
                pl.BlockSpec((tk, tn), lambda i, j, k: (k, j))]
    args = [a, w]
    if res is not None:
        in_specs.append(pl.BlockSpec((tm, tn), lambda i, j, k: (i, j)))
        args.append(res)
    scratch = [pltpu.VMEM((tm, tn), F32)] if nk > 1 else []
    return pl.pallas_call(
        functools.partial(_matmul_kernel, nk=nk, has_res=res is not None),
        out_shape=jax.ShapeDtypeStruct((m, n), out_dtype),
        grid=(m // tm, pl.cdiv(n, tn), nk),
        in_specs=in_specs,
        out_specs=pl.BlockSpec((tm, tn), lambda i, j, k: (i, j)),
        scratch_shapes=scratch,
        compiler_params=_cparams(("parallel", "parallel", "arbitrary")),
        name=name,
    )(*args)


def _out_proj_kernel(mix_ref, mo_ref, w1_ref, w2_ref, r_ref, o_ref):
    o_ref[...] = (jnp.dot(mix_ref[...], w1_ref[...].astype(BF16), preferred_element_type=F32)
                  + jnp.dot(mo_ref[...], w2_ref[...].astype(BF16), preferred_element_type=F32)
                  + r_ref[...])


def _out_proj(mix, mo, w, res, tm=1024, tn=512):
    t = mix.shape[0]
    tm = min(tm, t)
    n = w.shape[1]
    assert Q_W % MEM_Q_W == 0
    return pl.pallas_call(
        _out_proj_kernel,
        out_shape=jax.ShapeDtypeStruct((t, n), F32),
        grid=(t // tm, n // tn),
        in_specs=[pl.BlockSpec((tm, Q_W), lambda i, j: (i, 0)),
                  pl.BlockSpec((tm, MEM_Q_W), lambda i, j: (i, 0)),
                  pl.BlockSpec((Q_W, tn), lambda i, j: (0, j)),
                  pl.BlockSpec((MEM_Q_W, tn), lambda i, j: (Q_W // MEM_Q_W, j)),
                  pl.BlockSpec((tm, tn), lambda i, j: (i, j))],
        out_specs=pl.BlockSpec((tm, tn), lambda i, j: (i, j)),
        compiler_params=_cparams(("parallel", "parallel")),
        name="out_proj",
    )(mix, mo, w, w, res)


def _swiglu_kernel(x_ref, gn_ref, wg_ref, wu_ref, o_ref, xn_ref):
    @pl.when(pl.program_id(1) == 0)
    def _():
        xn_ref[...] = _rms(x_ref[...], gn_ref[...]).astype(BF16)

    a = xn_ref[...]
    g = jnp.dot(a, wg_ref[...].astype(BF16), preferred_element_type=F32)
    u = jnp.dot(a, wu_ref[...].astype(BF16), preferred_element_type=F32)
    o_ref[...] = (g * _sigmoid(g) * u).astype(o_ref.dtype)


def _swiglu_up(x, gain, w_gu, *, tm=1024, tn=256):
    m, d = x.shape
    f = w_gu.shape[1] // 2
    tm = min(tm, m)
    nj = f // tn
    return pl.pallas_call(
        _swiglu_kernel,
        out_shape=jax.ShapeDtypeStruct((m, f), BF16),
        grid=(m // tm, nj),
        in_specs=[pl.BlockSpec((tm, d), lambda i, j: (i, 0)),
                  pl.BlockSpec((1, d), lambda i, j: (0, 0)),
                  pl.BlockSpec((d, tn), lambda i, j: (0, j)),
                  pl.BlockSpec((d, tn), lambda i, j: (0, j + nj))],
        out_specs=pl.BlockSpec((tm, tn), lambda i, j: (i, j)),
        scratch_shapes=[pltpu.VMEM((tm, d), BF16)],
        compiler_params=_cparams(("parallel", "arbitrary")),
        name="swiglu_up",
    )(x, gain.reshape(1, d), w_gu, w_gu)


def _nsa_prep_kernel(p_ref, gtab_ref, o_ref):
    for c in range((Q_W + 6 * KV_W) // HEAD_DIM):
        x = p_ref[:, c * HEAD_DIM:(c + 1) * HEAD_DIM]
        if c < N_MIX_HEADS:
            y = _rms(x, gtab_ref[0:1, :]) * SCALE
        elif 18 <= c < 21:
            y = _rms(x, gtab_ref[1:2, :])
        elif 24 <= c < 27:
            y = _rms(x, gtab_ref[2:3, :])
        else:
            y = x
        o_ref[:, c * HEAD_DIM:(c + 1) * HEAD_DIM] = y.astype(o_ref.dtype)


def _nsa_prep(p, q_gain, k_gain, tm=256):
    t = p.shape[0]
    tm = min(tm, t)
    w = Q_W + 6 * KV_W
    gtab = jnp.zeros((8, HEAD_DIM), F32).at[0].set(q_gain).at[1].set(k_gain[1]).at[2].set(k_gain[2])
    return pl.pallas_call(
        _nsa_prep_kernel,
        out_shape=jax.ShapeDtypeStruct((t, w), BF16),
        grid=(t // tm,),
        in_specs=[pl.BlockSpec((tm, w), lambda i: (i, 0)), pl.BlockSpec((8, HEAD_DIM), lambda i: (0, 0))],
        out_specs=pl.BlockSpec((tm, w), lambda i: (i, 0)),
        compiler_params=_cparams(("parallel",)),
        name="nsa_prep",
    )(p, gtab)


def _compress_kernel(b_ref, pos_ref, w_ref, g_ref, o_ref):
    blk = (b_ref[0, 0] + pos_ref[0]).astype(BF16)
    y = jnp.dot(blk, w_ref[0].astype(BF16), preferred_element_type=F32)
    is_k = pl.program_id(0) == 0
    o_ref[0, 0] = jnp.where(is_k, _rms(y, g_ref[...]), y).astype(o_ref.dtype)


def _compress(blocks, cmp_w, cmp_pos, k_gain0):
    _, g, ncp, bw = blocks.shape
    return pl.pallas_call(
        _compress_kernel,
        out_shape=jax.ShapeDtypeStruct((2, g, ncp, HEAD_DIM), BF16),
        grid=(2, g),
        in_specs=[pl.BlockSpec((1, 1, ncp, bw), lambda s, gi: (s, gi, 0, 0)),
                  pl.BlockSpec((1, 1, bw), lambda s, gi: (s, 0, 0)),
                  pl.BlockSpec((1, bw, HEAD_DIM), lambda s, gi: (s, 0, 0)),
                  pl.BlockSpec((1, HEAD_DIM), lambda s, gi: (0, 0))],
        out_specs=pl.BlockSpec((1, 1, ncp, HEAD_DIM), lambda s, gi: (s, gi, 0, 0)),
        compiler_params=_cparams(("parallel", "parallel")),
        name="nsa_compress",
    )(blocks, cmp_pos.reshape(2, 1, bw), cmp_w, k_gain0.reshape(1, HEAD_DIM))


def _cmp_kernel(slopes_ref, q_ref, k_ref, v_ref, imap_ref, o_ref, score_ref, *, tq, ncp):
    g = pl.program_id(0)
    t0 = pl.program_id(1) * tq
    row = lax.broadcasted_iota(jnp.int32, (tq, ncp), 0) + t0
    col = lax.broadcasted_iota(jnp.int32, (tq, ncp), 1)
    c_end = col * CMP_STRIDE + (CMP_BLOCK - 1)
    neg = jnp.where(row >= c_end, 0.0, NEG)
    colb = (lax.broadcasted_iota(jnp.int32, (1, ncp), 1) * CMP_STRIDE + (CMP_BLOCK - 1) - t0).astype(F32)
    has_key = (lax.broadcasted_iota(jnp.int32, (tq, 1), 0) + t0 >= CMP_BLOCK - 1).astype(F32)
    k = k_ref[0, 0]
    v = v_ref[0, 0]
    psum = jnp.zeros((tq, ncp), F32)
    for p in range(HEADS_PER_GROUP):
        slope = slopes_ref[g * HEADS_PER_GROUP + p]
        q = q_ref[:, p * HEAD_DIM:(p + 1) * HEAD_DIM]
        s = _nt_dot(q, k) + slope * colb + neg
        e = jnp.exp(s - jnp.max(s, axis=1, keepdims=True))
        pr = e * (has_key / jnp.sum(e, axis=1, keepdims=True))
        psum = psum + pr
        o_ref[:, p * HEAD_DIM:(p + 1) * HEAD_DIM] = jnp.dot(
            pr.astype(BF16), v, preferred_element_type=F32).astype(o_ref.dtype)

    imap = imap_ref[...]
    hi = psum.astype(BF16)
    r1 = psum - hi.astype(F32)
    mid = r1.astype(BF16)
    lo = (r1 - mid.astype(F32)).astype(BF16)
    imp = (jnp.dot(hi, imap, preferred_element_type=F32)
           + jnp.dot(mid, imap, preferred_element_type=F32)
           + jnp.dot(lo, imap, preferred_element_type=F32))
    nb = imp.shape[1]
    t = lax.broadcasted_iota(jnp.int32, (tq, nb), 0) + t0
    blk = lax.broadcasted_iota(jnp.int32, (tq, nb), 1)
    score_ref[0] = jnp.where(blk == t // SLC_BLOCK, 2 * BIG,
                             jnp.where(blk == 0, BIG,
                                       jnp.where(blk * SLC_BLOCK <= t, imp, -BIG)))


def _cmp_attn(slopes, qkv, kvc, imap, tq=256):
    t = qkv.shape[0]
    tq = min(tq, t)
    ncp = kvc.shape[2]
    nb = t // SLC_BLOCK
    gw = HEADS_PER_GROUP * HEAD_DIM
    return pl.pallas_call(
        functools.partial(_cmp_kernel, tq=tq, ncp=ncp),
        out_shape=(jax.ShapeDtypeStruct((t, Q_W), F32),
                   jax.ShapeDtypeStruct((N_KV_GROUPS, t, nb), F32)),
        grid_spec=pltpu.PrefetchScalarGridSpec(
            num_scalar_prefetch=1,
            grid=(N_KV_GROUPS, t // tq),
            in_specs=[pl.BlockSpec((tq, gw), lambda g, i, s: (i, g)),
                      pl.BlockSpec((1, 1, ncp, HEAD_DIM), lambda g, i, s: (0, g, 0, 0)),
                      pl.BlockSpec((1, 1, ncp, HEAD_DIM), lambda g, i, s: (1, g, 0, 0)),
                      pl.BlockSpec((ncp, nb), lambda g, i, s: (0, 0))],
            out_specs=(pl.BlockSpec((tq, gw), lambda g, i, s: (i, g)),
                       pl.BlockSpec((1, tq, nb), lambda g, i, s: (g, i, 0)))),
        compiler_params=_cparams(("parallel", "parallel")),
        name="nsa_cmp",
    )(slopes, qkv, kvc, kvc, imap)


def _topk_kernel(score_ref, sel_ref, act_ref, *, tm, sub):
    score = score_ref[0]
    nb = score.shape[1]
    blk_f = lax.broadcasted_iota(jnp.int32, (tm, nb), 1).astype(F32)

    def pick_one(_, carry):
        work, sel = carry
        m = jnp.max(work, axis=1, keepdims=True)
        first = jnp.min(jnp.where(work == m, blk_f, float(nb)), axis=1, keepdims=True)
        pick = blk_f == first
        return jnp.where(pick, -3e38, work), jnp.where(pick, 1.0, sel)

    _, sel = lax.fori_loop(0, min(SLC_TOPK, nb), pick_one, (score, jnp.zeros((tm, nb), F32)))
    sel_ref[0] = sel.astype(sel_ref.dtype)
    for a in range(tm // sub):
        act_ref[0, a] = jnp.broadcast_to(jnp.max(sel[a * sub:(a + 1) * sub], axis=0, keepdims=True),
                                         act_ref.shape[2:])


def _topk_select(score, sub, tm=1024):
    g, t, nb = score.shape
    tm = min(tm, t)
    sub = min(sub, tm)
    return pl.pallas_call(
        functools.partial(_topk_kernel, tm=tm, sub=sub),
        out_shape=(jax.ShapeDtypeStruct((g, t, nb), BF16),
                   jax.ShapeDtypeStruct((g, t // sub, 8, nb), F32)),
        grid=(g, t // tm),
        in_specs=[pl.BlockSpec((1, tm, nb), lambda gi, i: (gi, i, 0))],
        out_specs=(pl.BlockSpec((1, tm, nb), lambda gi, i: (gi, i, 0)),
                   pl.BlockSpec((1, tm // sub, 8, nb), lambda gi, i: (gi, i, 0, 0))),
        compiler_params=_cparams(("parallel", "parallel")),
        name="nsa_topk",
    )(score)


def _slc_kernel(slopes_ref, flags_ref, q_ref, k_ref, v_ref, sel_ref, o_ref, m_ref, l_ref, acc_ref, *, tq, tk):
    g = pl.program_id(0)
    i = pl.program_id(1)
    t0 = i * tq
    flag_base = (g * pl.num_programs(1) + i) * (k_ref.shape[0] // tk)
    nb = sel_ref.shape[2]
    m_ref[...] = jnp.full(m_ref.shape, NEG, F32)
    l_ref[...] = jnp.zeros(l_ref.shape, F32)
    acc_ref[...] = jnp.zeros(acc_ref.shape, F32)
    sel = sel_ref[0]
    bpt = tk // SLC_BLOCK

    def body(j, carry):
        @pl.when(flags_ref[flag_base + j] > 0)
        def _():
            tile(j)
        return carry

    def tile(j):
        k0 = pl.multiple_of(j * tk, tk)
        k = k_ref[pl.ds(k0, tk), :]
        v = v_ref[pl.ds(k0, tk), :]
        eb = lax.broadcasted_iota(jnp.int32, (nb, tk), 0)
        ec = lax.broadcasted_iota(jnp.int32, (nb, tk), 1)
        expand = jnp.where(eb == ec // SLC_BLOCK + j * bpt, 1.0, 0.0).astype(BF16)
        selx = jnp.dot(sel, expand, preferred_element_type=F32)
        rel = (t0 - k0) + lax.broadcasted_iota(jnp.int32, (tq, tk), 0) \
            - lax.broadcasted_iota(jnp.int32, (tq, tk), 1)
        neg = jnp.where(jnp.where(rel >= 0, selx, 0.0) > 0.5, 0.0, NEG)
        colb = (lax.broadcasted_iota(jnp.int32, (1, tk), 1) + (k0 - t0)).astype(F32)
        for p in range(HEADS_PER_GROUP):
            slope = slopes_ref[g * HEADS_PER_GROUP + p]
            q = q_ref[:, p * HEAD_DIM:(p + 1) * HEAD_DIM]
            s = _nt_dot(q, k) + slope * colb + neg
            m_old = m_ref[p]
            m_new = jnp.maximum(m_old, jnp.max(s, axis=1, keepdims=True))
            alpha = jnp.exp(m_old - m_new)
            e = jnp.exp(s - jnp.tile(m_new, (1, tk // LANES)))
            l_ref[p] = alpha * l_ref[p] + jnp.sum(e, axis=1, keepdims=True)
            acc_ref[p] = alpha * acc_ref[p] + jnp.dot(e.astype(BF16), v, preferred_element_type=F32)
            m_ref[p] = m_new

    n_tiles = (t0 + tq - 1) // tk + 1
    lax.fori_loop(0, n_tiles, body, 0)
    for p in range(HEADS_PER_GROUP):
        o_ref[:, p * HEAD_DIM:(p + 1) * HEAD_DIM] = (acc_ref[p] / l_ref[p]).astype(o_ref.dtype)


def _slc_attn(slopes, qkv, sel, blk_act, tq=256, tk=256):
    t = qkv.shape[0]
    tq = min(tq, t)
    tk = min(tk, t)
    nb = sel.shape[2]
    assert blk_act.shape[1] == t // tq and tq % tk == 0
    gw = HEADS_PER_GROUP * HEAD_DIM
    ks_col = (Q_W + 2 * KV_W) // HEAD_DIM
    vs_col = (Q_W + 3 * KV_W) // HEAD_DIM
    flags = jnp.max(blk_act.reshape(N_KV_GROUPS, t // tq, t // tk, tk // SLC_BLOCK), axis=-1)
    flags = (flags > 0.5).astype(jnp.int32).reshape(-1)
    return pl.pallas_call(
        functools.partial(_slc_kernel, tq=tq, tk=tk),
        out_shape=jax.ShapeDtypeStruct((t, Q_W), F32),
        grid_spec=pltpu.PrefetchScalarGridSpec(
            num_scalar_prefetch=2,
            grid=(N_KV_GROUPS, t // tq),
            in_specs=[pl.BlockSpec((tq, gw), lambda g, i, s, f: (i, g)),
                      pl.BlockSpec((t, HEAD_DIM), lambda g, i, s, f: (0, ks_col + g)),
                      pl.BlockSpec((t, HEAD_DIM), lambda g, i, s, f: (0, vs_col + g)),
                      pl.BlockSpec((1, tq, nb), lambda g, i, s, f: (g, i, 0))],
            out_specs=pl.BlockSpec((tq, gw), lambda g, i, s, f: (i, g)),
            scratch_shapes=[pltpu.VMEM((HEADS_PER_GROUP, tq, LANES), F32),
                            pltpu.VMEM((HEADS_PER_GROUP, tq, LANES), F32),
                            pltpu.VMEM((HEADS_PER_GROUP, tq, HEAD_DIM), F32)]),
        compiler_params=_cparams(("parallel", "parallel")),
        name="nsa_selected",
    )(slopes, flags, qkv, qkv, qkv, sel)


def _win_kernel(slopes_ref, q_ref, k0_ref, k1_ref, k2_ref, v0_ref, v1_ref, v2_ref, o_ref, *, tq):
    g = pl.program_id(0)
    i = pl.program_id(1)
    nprev = WINDOW // tq
    kw = (nprev + 1) * tq
    k = jnp.concatenate([k0_ref[...], k1_ref[...], k2_ref[...]], axis=0)
    v = jnp.concatenate([v0_ref[...], v1_ref[...], v2_ref[...]], axis=0)
    r = lax.broadcasted_iota(jnp.int32, (tq, kw), 0)
    c = lax.broadcasted_iota(jnp.int32, (tq, kw), 1)
    rel = WINDOW + r - c
    first_valid = (nprev - i) * tq
    inside = jnp.where(rel >= 0, jnp.where(rel < WINDOW, 1.0, 0.0), 0.0)
    neg = jnp.where(jnp.where(c >= first_valid, inside, 0.0) > 0.5, 0.0, NEG)
    colb = (lax.broadcasted_iota(jnp.int32, (1, kw), 1) - WINDOW).astype(F32)
    for p in range(HEADS_PER_GROUP):
        slope = slopes_ref[g * HEADS_PER_GROUP + p]
        q = q_ref[:, p * HEAD_DIM:(p + 1) * HEAD_DIM]
        s = _nt_dot(q, k) + slope * colb + neg
        e = jnp.exp(s - jnp.max(s, axis=1, keepdims=True))
        o = jnp.dot(e.astype(BF16), v, preferred_element_type=F32) / jnp.sum(e, axis=1, keepdims=True)
        o_ref[:, p * HEAD_DIM:(p + 1) * HEAD_DIM] = o.astype(o_ref.dtype)


def _win_attn(slopes, qkv, tq=256):
    t = qkv.shape[0]
    assert WINDOW % tq == 0 and WINDOW // tq == 2
    gw = HEADS_PER_GROUP * HEAD_DIM
    kw_col = (Q_W + 4 * KV_W) // HEAD_DIM
    vw_col = (Q_W + 5 * KV_W) // HEAD_DIM

    def kv_spec(col, back):
        return pl.BlockSpec((tq, HEAD_DIM), lambda g, i, s: (jnp.maximum(i - back, 0), col + g))

    return pl.pallas_call(
        functools.partial(_win_kernel, tq=tq),
        out_shape=jax.ShapeDtypeStruct((t, Q_W), F32),
        grid_spec=pltpu.PrefetchScalarGridSpec(
            num_scalar_prefetch=1,
            grid=(N_KV_GROUPS, t // tq),
            in_specs=[pl.BlockSpec((tq, gw), lambda g, i, s: (i, g)),
                      kv_spec(kw_col, 2), kv_spec(kw_col, 1), kv_spec(kw_col, 0),
                      kv_spec(vw_col, 2), kv_spec(vw_col, 1), kv_spec(vw_col, 0)],
            out_specs=pl.BlockSpec((tq, gw), lambda g, i, s: (i, g))),
        compiler_params=_cparams(("parallel", "parallel")),
        name="nsa_window",
    )(slopes, qkv, qkv, qkv, qkv, qkv, qkv, qkv)


def _gate_sum_kernel(oc_ref, os_ref, ow_ref, g_ref, o_ref):
    sg = _sigmoid(g_ref[...])
    for h in range(N_MIX_HEADS):
        sl = slice(h * HEAD_DIM, (h + 1) * HEAD_DIM)
        o = (oc_ref[:, sl] * sg[:, 3 * h:3 * h + 1]
             + os_ref[:, sl] * sg[:, 3 * h + 1:3 * h + 2]
             + ow_ref[:, sl] * sg[:, 3 * h + 2:3 * h + 3])
        o_ref[:, sl] = o.astype(o_ref.dtype)


def _gate_sum(o_cmp, o_slc, o_win, gates, tm=512):
    t = o_cmp.shape[0]
    tm = min(tm, t)
    big = pl.BlockSpec((tm, Q_W), lambda i: (i, 0))
    return pl.pallas_call(
        _gate_sum_kernel,
        out_shape=jax.ShapeDtypeStruct((t, Q_W), BF16),
        grid=(t // tm,),
        in_specs=[big, big, big, pl.BlockSpec((tm, GATE_W), lambda i: (i, 0))],
        out_specs=big,
        compiler_params=_cparams(("parallel",)),
        name="nsa_gate_sum",
    )(o_cmp, o_slc, o_win, gates)


def _mem_attn_kernel(q_ref, k_ref, v_ref, g_ref, o_ref):
    q = (_rms(q_ref[...], g_ref[0:1, :]) * SCALE).astype(BF16)
    k = _rms(k_ref[...], g_ref[1:2, :]).astype(BF16)
    s = _nt_dot(q, k)
    e = jnp.exp(s - jnp.max(s, axis=1, keepdims=True))
    p = e * (1.0 / jnp.sum(e, axis=1, keepdims=True))
    o_ref[...] = jnp.dot(p.astype(BF16), v_ref[...].astype(BF16),
                         preferred_element_type=F32).astype(o_ref.dtype)


def _mem_attn(qsrc, q_col, memkv, qk_gain, tq=512):
    t = qsrc.shape[0]
    tq = min(tq, t)
    nm = memkv.shape[0]
    gtab = jnp.zeros((8, HEAD_DIM), F32).at[0:2].set(qk_gain)
    return pl.pallas_call(
        _mem_attn_kernel,
        out_shape=jax.ShapeDtypeStruct((t, MEM_Q_W), BF16),
        grid=(N_MEM_HEADS, t // tq),
        in_specs=[pl.BlockSpec((tq, HEAD_DIM), lambda h, i: (i, q_col + h)),
                  pl.BlockSpec((nm, HEAD_DIM), lambda h, i: (0, h)),
                  pl.BlockSpec((nm, HEAD_DIM), lambda h, i: (0, N_MEM_HEADS + h)),
                  pl.BlockSpec((8, HEAD_DIM), lambda h, i: (0, 0))],
        out_specs=pl.BlockSpec((tq, HEAD_DIM), lambda h, i: (i, h)),
        compiler_params=_cparams(("parallel", "parallel")),
        name="mem_attn",
    )(qsrc, memkv, memkv, gtab)


def _stick_kernel(q_ref, k_ref, v_ref, o_ref, *, tq, hp):
    i = pl.program_id(1)
    r = lax.broadcasted_iota(jnp.int32, (tq, tq), 0)
    c = lax.broadcasted_iota(jnp.int32, (tq, tq), 1)
    later = jnp.where(r > c, 1.0, 0.0).astype(BF16)
    qs = [(q_ref[:, h * HEAD_DIM:(h + 1) * HEAD_DIM] * SCALE).astype(BF16) for h in range(hp)]

    def tile(h, k0, tail, acc, diagonal):
        k = k_ref[pl.ds(k0, tq), h * HEAD_DIM:(h + 1) * HEAD_DIM]
        v = v_ref[pl.ds(k0, tq), h * HEAD_DIM:(h + 1) * HEAD_DIM]
        z = _nt_dot(qs[h], k)
        sp = jnp.maximum(z, 0.0) + jnp.log(1.0 + jnp.exp(-jnp.abs(z)))
        log_keep = jnp.where(r > c, -sp, 0.0) if diagonal else -sp
        hi = log_keep.astype(BF16)
        lo = (log_keep - hi.astype(F32)).astype(BF16)
        after = (jnp.dot(hi, later, preferred_element_type=F32)
                 + jnp.dot(lo, later, preferred_element_type=F32) + tail)
        a = jnp.exp(z - sp + after)
        if diagonal:
            a = jnp.where(r > c, a, 0.0)
        acc = acc + jnp.dot(a.astype(BF16), v, preferred_element_type=F32)
        return tail + jnp.sum(log_keep, axis=1, keepdims=True), acc

    k_diag = pl.multiple_of(i * tq, tq)
    state = [tile(h, k_diag, jnp.zeros((tq, 1), F32), jnp.zeros((tq, HEAD_DIM), F32), True) for h in range(hp)]

    def live(st):
        return jnp.logical_and(st[0] <= i, st[1] > 0)

    def body(st):
        k0 = pl.multiple_of((i - st[0]) * tq, tq)
        new = [tile(h, k0, st[2 + 2 * h], st[3 + 2 * h], False) for h in range(hp)]
        top = functools.reduce(jnp.maximum, [jnp.max(tl) for tl, _ in new])
        return (st[0] + 1, (top > STICK_DEAD).astype(jnp.int32)) + tuple(x for pair in new for x in pair)

    st = lax.while_loop(live, body, (jnp.int32(1), jnp.int32(1)) + tuple(x for pair in state for x in pair))
    for h in range(hp):
        o_ref[:, h * HEAD_DIM:(h + 1) * HEAD_DIM] = st[3 + 2 * h].astype(o_ref.dtype)


def _stick_attn(pb, kvb, tq=256, hp=2):
    t = pb.shape[0]
    tq = min(tq, t)
    hw = hp * HEAD_DIM
    return pl.pallas_call(
        functools.partial(_stick_kernel, tq=tq, hp=hp),
        out_shape=jax.ShapeDtypeStruct((t, Q_W), BF16),
        grid=(N_MIX_HEADS // hp, t // tq),
        in_specs=[pl.BlockSpec((tq, hw), lambda h, i: (i, h)),
                  pl.BlockSpec((t, hw), lambda h, i: (0, h)),
                  pl.BlockSpec((t, hw), lambda h, i: (0, N_MIX_HEADS // hp + h))],
        out_specs=pl.BlockSpec((tq, hw), lambda h, i: (i, h)),
        compiler_params=_cparams(("parallel", "parallel")),
        name="stick_breaking",
    )(pb, kvb, kvb)


def _router_kernel(x_ref, g_ref, w_ref, b_ref, c_ref):
    h = _rms(x_ref[...], g_ref[...])
    logits = jnp.dot(h, w_ref[...], preferred_element_type=F32, precision=lax.Precision.HIGHEST)
    lane = lax.broadcasted_iota(jnp.int32, logits.shape, 1).astype(F32)
    biased = jnp.where(lane < N_EXPERTS, logits + b_ref[...], -3e38)
    m1 = jnp.max(biased, axis=1, keepdims=True)
    i1 = jnp.min(jnp.where(biased == m1, lane, float(LANES)), axis=1, keepdims=True)
    rest = jnp.where(lane == i1, -3e38, biased)
    m2 = jnp.max(rest, axis=1, keepdims=True)
    i2 = jnp.min(jnp.where(rest == m2, lane, float(LANES)), axis=1, keepdims=True)
    l1 = jnp.sum(jnp.where(lane == i1, logits, 0.0), axis=1, keepdims=True)
    l2 = jnp.sum(jnp.where(lane == i2, logits, 0.0), axis=1, keepdims=True)
    mx = jnp.maximum(l1, l2)
    e1 = jnp.exp(l1 - mx)
    e2 = jnp.exp(l2 - mx)
    inv = 1.0 / (e1 + e2)
    c_ref[...] = (jnp.where(lane == 0.0, i1, 0.0) + jnp.where(lane == 1.0, i2, 0.0)
                  + jnp.where(lane == 2.0, e1 * inv, 0.0) + jnp.where(lane == 3.0, e2 * inv, 0.0))


def _router(x, gain, router_w, router_b, tm=512):
    t, d = x.shape
    tm = min(tm, t)
    w = jnp.zeros((d, LANES), F32).at[:, :N_EXPERTS].set(router_w)
    b = jnp.zeros((1, LANES), F32).at[0, :N_EXPERTS].set(router_b)
    return pl.pallas_call(
        _router_kernel,
        out_shape=jax.ShapeDtypeStruct((t, LANES), F32),
        grid=(t // tm,),
        in_specs=[pl.BlockSpec((tm, d), lambda i: (i, 0)), pl.BlockSpec((1, d), lambda i: (0, 0)),
                  pl.BlockSpec((d, LANES), lambda i: (0, 0)), pl.BlockSpec((1, LANES), lambda i: (0, 0))],
        out_specs=pl.BlockSpec((tm, LANES), lambda i: (i, 0)),
        compiler_params=_cparams(("parallel",)),
        name="moe_router",
    )(x, gain.reshape(1, d), w, b)


def _route_plan(rt, sub, per_group):
    t = rt.shape[0]
    i32 = jnp.int32
    e_flat = rt[:, 0:2].astype(i32).reshape(-1)
    onehot = (e_flat[:, None] == jnp.arange(N_EXPERTS, dtype=i32)[None, :]).astype(i32)
    csum = jnp.cumsum(onehot, axis=0)
    rank = jnp.sum(csum * onehot, axis=1) - 1
    tiles_e = (csum[-1] + sub - 1) // sub
    tile_end = jnp.cumsum(tiles_e)
    tile_start = tile_end - tiles_e
    dest = tile_start[e_flat] * sub + rank
    groups_e = (tiles_e + per_group - 1) // per_group
    group_end = jnp.cumsum(groups_e)
    n_groups = group_end[-1]
    max_groups = (2 * t // sub + N_EXPERTS) // per_group + N_EXPERTS
    gidx = jnp.arange(max_groups, dtype=i32)
    gcl = jnp.minimum(gidx, n_groups - 1)
    g_expert = jnp.sum((gcl[:, None] >= group_end[None, :]).astype(i32), axis=1)
    k_in_e = gcl - (group_end - groups_e)[g_expert]
    g_start = (tile_start[g_expert] + k_in_e * per_group) * sub
    g_tiles = jnp.where(gidx < n_groups, jnp.clip(tiles_e[g_expert] - k_in_e * per_group, 0, per_group), 0)
    pad_lo = tile_start * sub + csum[-1]
    return (dest.astype(i32), pad_lo.astype(i32), (tile_end * sub).astype(i32),
            tile_end[-1].reshape(1).astype(i32), g_expert.astype(i32), g_start.astype(i32), g_tiles.astype(i32))


def _dispatch_kernel(dest_ref, plo_ref, phi_ref, nt_ref, x_hbm, o_hbm, zbuf, sem, *, sub):
    n_pairs = dest_ref.shape[0]
    n_all = o_hbm.shape[0] // sub
    zbuf[...] = jnp.zeros(zbuf.shape, zbuf.dtype)

    def put(p, c):
        pltpu.make_async_copy(x_hbm.at[pl.ds(p // 2, 1), :], o_hbm.at[pl.ds(dest_ref[p], 1), :], sem).start()
        return c

    lax.fori_loop(0, n_pairs, put, 0, unroll=8)

    def pad_start(r, c):
        pltpu.make_async_copy(zbuf.at[pl.ds(0, 1), :], o_hbm.at[pl.ds(r, 1), :], sem).start()
        return c

    def pad_wait(r, c):
        pltpu.make_async_copy(zbuf.at[pl.ds(0, 1), :], o_hbm.at[pl.ds(0, 1), :], sem).wait()
        return c

    def tail_start(r, c):
        pltpu.make_async_copy(zbuf, o_hbm.at[pl.ds(pl.multiple_of(r * sub, sub), sub), :], sem).start()
        return c

    def tail_wait(r, c):
        pltpu.make_async_copy(zbuf, o_hbm.at[pl.ds(0, sub), :], sem).wait()
        return c

    for e in range(N_EXPERTS):
        lax.fori_loop(plo_ref[e], phi_ref[e], pad_start, 0)
    lax.fori_loop(nt_ref[0], n_all, tail_start, 0)
    pltpu.make_async_copy(o_hbm.at[pl.ds(0, n_pairs), :], o_hbm.at[pl.ds(0, n_pairs), :], sem).wait()
    for e in range(N_EXPERTS):
        lax.fori_loop(plo_ref[e], phi_ref[e], pad_wait, 0)
    lax.fori_loop(nt_ref[0], n_all, tail_wait, 0)


def _dispatch(x, dest, pad_lo, pad_hi, n_tiles, sub, nt):
    t, d = x.shape
    assert nt * sub >= dest.shape[0]
    return pl.pallas_call(
        functools.partial(_dispatch_kernel, sub=sub),
        out_shape=jax.ShapeDtypeStruct((nt * sub, d), F32),
        grid_spec=pltpu.PrefetchScalarGridSpec(
            num_scalar_prefetch=4,
            grid=(1,),
            in_specs=[pl.BlockSpec(memory_space=pl.ANY)],
            out_specs=pl.BlockSpec(memory_space=pl.ANY),
            scratch_shapes=[pltpu.VMEM((sub, d), F32), pltpu.SemaphoreType.DMA(())]),
        compiler_params=_cparams(("arbitrary",)),
        name="moe_dispatch",
    )(dest, pad_lo, pad_hi, n_tiles, x)


def _experts_kernel(ge_ref, gs_ref, gn_ref, nt_ref, xg_hbm, gain_ref, wg_ref, wu_ref, wd_ref, y_hbm,
                    xbuf, yacc, stage, wgc, wuc, wdc, sem_in, sem_out, *, sub):
    s = pl.program_id(0)
    j = pl.program_id(1)
    n_sub = gn_ref[s]
    start = pl.multiple_of(gs_ref[s], sub)

    @pl.when(jnp.logical_and(s == 0, j == 0))
    def _():
        yacc[0:sub, :] = jnp.zeros((sub, yacc.shape[1]), F32)
        first, last = nt_ref[0], y_hbm.shape[0] // sub

        def put(r, c):
            pltpu.make_async_copy(yacc.at[pl.ds(0, sub), :],
                                  y_hbm.at[pl.ds(pl.multiple_of(r * sub, sub), sub), :], sem_out).start()
            return c

        def done(r, c):
            pltpu.make_async_copy(yacc.at[pl.ds(0, sub), :], y_hbm.at[pl.ds(0, sub), :], sem_out).wait()
            return c

        lax.fori_loop(first, last, put, 0)
        lax.fori_loop(first, last, done, 0)

    @pl.when(n_sub > 0)
    def _():
        @pl.when(j == 0)
        def _():
            def fetch(r, slot):
                pltpu.make_async_copy(xg_hbm.at[pl.ds(start + r * sub, sub), :], stage.at[slot],
                                      sem_in.at[slot]).start()

            def load(r, c):
                slot = r % 2

                @pl.when(r + 1 < n_sub)
                def _():
                    fetch(r + 1, 1 - slot)

                pltpu.make_async_copy(xg_hbm.at[pl.ds(0, sub), :], stage.at[slot], sem_in.at[slot]).wait()
                xbuf[pl.ds(pl.multiple_of(r * sub, sub), sub), :] = _rms(stage[slot], gain_ref[...]).astype(BF16)
                return c

            fetch(0, 0)
            yacc[...] = jnp.zeros(yacc.shape, F32)
            lax.fori_loop(0, n_sub, load, 0)

        wgc[...] = wg_ref[0].astype(BF16)
        wuc[...] = wu_ref[0].astype(BF16)
        wdc[...] = wd_ref[0].astype(BF16)

        def rows_at(r0, m):
            x = xbuf[pl.ds(r0, m), :]
            g = jnp.dot(x, wgc[...], preferred_element_type=F32)
            u = jnp.dot(x, wuc[...], preferred_element_type=F32)
            a = (g * _sigmoid(g) * u).astype(BF16)
            yacc[pl.ds(r0, m), :] += jnp.dot(a, wdc[...], preferred_element_type=F32)

        def tile_pair(r, c):
            rows_at(pl.multiple_of(r * 2 * sub, 2 * sub), 2 * sub)
            return c

        lax.fori_loop(0, n_sub // 2, tile_pair, 0)

        @pl.when(n_sub % 2 == 1)
        def _():
            rows_at(pl.multiple_of((n_sub - 1) * sub, sub), sub)

        @pl.when(j == pl.num_programs(1) - 1)
        def _():
            def put(r, c):
                r0 = pl.multiple_of(r * sub, sub)
                pltpu.make_async_copy(yacc.at[pl.ds(r0, sub), :], y_hbm.at[pl.ds(start + r0, sub), :],
                                      sem_out).start()
                return c

            def done(r, c):
                pltpu.make_async_copy(yacc.at[pl.ds(0, sub), :], y_hbm.at[pl.ds(0, sub), :], sem_out).wait()
                return c

            lax.fori_loop(0, n_sub, put, 0)
            lax.fori_loop(0, n_sub, done, 0)


def _moe_experts(xg, gain, w_gu, w_down, g_expert, g_start, g_tiles, n_tiles, *, sub, per_group, tf=256):
    p, d = xg.shape
    f = w_down.shape[1]
    nj = f // tf
    rows = sub * per_group
    n_groups = g_expert.shape[0]

    def jcl(s, j, gn):
        return jnp.where(gn[s] > 0, j, nj - 1)

    return pl.pallas_call(
        functools.partial(_experts_kernel, sub=sub),
        out_shape=jax.ShapeDtypeStruct((p, d), F32),
        grid_spec=pltpu.PrefetchScalarGridSpec(
            num_scalar_prefetch=4,
            grid=(n_groups, nj),
            in_specs=[pl.BlockSpec(memory_space=pl.ANY),
                      pl.BlockSpec((1, d), lambda s, j, ge, gs, gn, nt: (0, 0)),
                      pl.BlockSpec((1, d, tf), lambda s, j, ge, gs, gn, nt: (ge[s], 0, jcl(s, j, gn))),
                      pl.BlockSpec((1, d, tf), lambda s, j, ge, gs, gn, nt: (ge[s], 0, jcl(s, j, gn) + nj)),
                      pl.BlockSpec((1, tf, d), lambda s, j, ge, gs, gn, nt: (ge[s], jcl(s, j, gn), 0))],
            out_specs=pl.BlockSpec(memory_space=pl.ANY),
            scratch_shapes=[pltpu.VMEM((rows, d), BF16), pltpu.VMEM((rows, d), F32), pltpu.VMEM((2, sub, d), F32),
                            pltpu.VMEM((d, tf), BF16), pltpu.VMEM((d, tf), BF16), pltpu.VMEM((tf, d), BF16),
                            pltpu.SemaphoreType.DMA((2,)), pltpu.SemaphoreType.DMA(())]),
        compiler_params=pltpu.CompilerParams(dimension_semantics=("arbitrary", "arbitrary"),
                                             vmem_limit_bytes=MOE_VMEM_LIMIT),
        name="moe_experts",
    )(g_expert, g_start, g_tiles, n_tiles, xg, gain.reshape(1, d), w_gu, w_gu, w_down)


def _moe_combine_kernel(dest_ref, x_ref, rt_ref, y_hbm, o_ref, ybuf, sems, *, tm):
    i = pl.program_id(0)

    def gather(tile, slot):
        base = tile * tm

        def issue(r, c):
            for k in range(2):
                pltpu.make_async_copy(y_hbm.at[pl.ds(dest_ref[2 * (base + r) + k], 1), :],
                                      ybuf.at[slot, pl.ds(k * tm + r, 1), :], sems.at[slot]).start()
            return c

        lax.fori_loop(0, tm, issue, 0, unroll=4)

    @pl.when(i == 0)
    def _():
        gather(0, 0)

    slot = i % 2

    @pl.when(i + 1 < pl.num_programs(0))
    def _():
        gather(i + 1, 1 - slot)

    pltpu.make_async_copy(y_hbm.at[pl.ds(0, 2 * tm), :], ybuf.at[slot], sems.at[slot]).wait()
    w = rt_ref[...]
    o_ref[...] = (x_ref[...] + w[:, 2:3] * ybuf[slot, 0:tm, :] + w[:, 3:4] * ybuf[slot, tm:2 * tm, :])


def _moe_combine(x, rt, y, dest, tm=256):
    t, d = x.shape
    tm = min(tm, t)
    return pl.pallas_call(
        functools.partial(_moe_combine_kernel, tm=tm),
        out_shape=jax.ShapeDtypeStruct((t, d), F32),
        grid_spec=pltpu.PrefetchScalarGridSpec(
            num_scalar_prefetch=1,
            grid=(t // tm,),
            in_specs=[pl.BlockSpec((tm, d), lambda i, dst: (i, 0)),
                      pl.BlockSpec((tm, LANES), lambda i, dst: (i, 0)),
                      pl.BlockSpec(memory_space=pl.ANY)],
            out_specs=pl.BlockSpec((tm, d), lambda i, dst: (i, 0)),
            scratch_shapes=[pltpu.VMEM((2, 2 * tm, d), F32), pltpu.SemaphoreType.DMA((2,))]),
        compiler_params=_cparams(("arbitrary",)),
        name="moe_combine",
    )(dest, x, rt, y)


def _moe_block(xs, gain, router_w, router_b, w_gu, w_down):
    t = xs.shape[0]
    sub = min(MOE_SUB, t)
    nt = 2 * t // sub + N_EXPERTS
    rt = _router(xs, gain, router_w, router_b)
    dest, pad_lo, pad_hi, n_tiles, g_expert, g_start, g_tiles = _route_plan(rt, sub, MOE_TILES_PER_GROUP)
    xg = _dispatch(xs, dest, pad_lo, pad_hi, n_tiles, sub, nt)
    y = _moe_experts(xg, gain, w_gu, w_down, g_expert, g_start, g_tiles, n_tiles,
                     sub=sub, per_group=MOE_TILES_PER_GROUP)
    return _moe_combine(xs, rt, y, dest)


def _alibi_slopes():
    return jnp.asarray(np.array([2.0 ** (-8.0 * (i + 1) / N_MIX_HEADS) for i in range(N_MIX_HEADS)],
                                dtype=np.float32))


def _importance_map(ncp, nb):
    ratio = SLC_BLOCK // CMP_STRIDE
    c = np.arange(ncp)[:, None]
    b = np.arange(nb)[None, :]
    m = (c >= ratio * b - 1) & (c <= ratio * b + ratio - 1) & (c < ncp - 1)
    return jnp.asarray(m.astype(np.float32), dtype=BF16)


def _compress_blocks(p, t):
    n = t // CMP_STRIDE
    z = p.reshape(n, CMP_STRIDE, 2, N_KV_GROUPS, HEAD_DIM).transpose(2, 3, 0, 1, 4)
    z = z.reshape(2, N_KV_GROUPS, n, CMP_STRIDE * HEAD_DIM)
    nxt = jnp.concatenate([z[:, :, 1:], jnp.zeros_like(z[:, :, :1])], axis=2)
    return jnp.concatenate([z, nxt], axis=3)


def _nsa_mix(p, slopes, q_gain, k_gain, cmp_w, cmp_pos, slc_tq=256):
    t = p.shape[0]
    qkv = _nsa_prep(p, q_gain, k_gain)
    blocks = _compress_blocks(p[:, Q_W:Q_W + 2 * KV_W], t)
    kvc = _compress(blocks, cmp_w, cmp_pos.reshape(2, -1), k_gain[0])
    imap = _importance_map(t // CMP_STRIDE, t // SLC_BLOCK)
    o_cmp, score = _cmp_attn(slopes, qkv, kvc, imap)
    sel, blk_act = _topk_select(score, sub=slc_tq)
    o_slc = _slc_attn(slopes, qkv, sel, blk_act[:, :, 0, :], tq=slc_tq)
    o_win = _win_attn(slopes, qkv)
    gates = p[:, Q_W + 6 * KV_W:Q_W + 6 * KV_W + GATE_W]
    return _gate_sum(o_cmp, o_slc, o_win, gates)


def kernel(x, mem, attn_norm, mem_norm, ffn_norm, w_in_a, cmp_w, cmp_pos, nsa_q_norm, nsa_k_norm, w_in_b,
           kv_norm_b, w_kv_b, w_mem_kv, mem_qk_norm, w_out, ffn_w_gu, ffn_w_down, router_w, router_b,
           moe_w_gu, moe_w_down):
    b, t, d = x.shape
    assert b == 1
    xs = x[0]
    mems = mem[0]
    slopes = _alibi_slopes()

    def half(w):
        return w.astype(BF16)

    p = _norm_matmul(xs, attn_norm[0], half(w_in_a[0]), tm=1024, tn=640, name="in_proj_a")
    mix = _nsa_mix(p, slopes, nsa_q_norm[0], nsa_k_norm[0], cmp_w[0], cmp_pos[0])
    memkv = _norm_matmul(mems, mem_norm[0], w_mem_kv[0], tm=256, tn=512, name="mem_kv")
    qm = p[:, Q_W + 6 * KV_W + GATE_W:]
    mo = _mem_attn(qm, 0, memkv, mem_qk_norm[0])
    xs = _out_proj(mix, mo, half(w_out[0]), xs)
    act = _swiglu_up(xs, ffn_norm[0], half(ffn_w_gu[0]))
    xs = _matmul(act, half(ffn_w_down[0]), xs, tm=1024, tn=1024, tk=512, name="ffn_down")

    kvb = _norm_matmul(xs, kv_norm_b, half(w_kv_b), tm=1024, tn=512, out_dtype=BF16, name="kv_proj_b")
    pb = _norm_matmul(xs, attn_norm[1], half(w_in_b[0]), tm=1024, tn=512, name="in_proj_b")
    mix = _stick_attn(pb, kvb)
    memkv = _norm_matmul(mems, mem_norm[1], w_mem_kv[1], tm=256, tn=512, name="mem_kv")
    mo = _mem_attn(pb, N_MIX_HEADS, memkv, mem_qk_norm[1])
    xs = _out_proj(mix, mo, half(w_out[1]), xs)
    xs = _moe_block(xs, ffn_norm[1], router_w[0], router_b[0], moe_w_gu[0], moe_w_down[0])
    return xs[None]
```

```python
import functools

import numpy as np
import jax
import jax.numpy as jnp
from jax import lax
from jax.experimental import pallas as pl
from jax.experimental.pallas import tpu as pltpu

HEAD_DIM = 128
N_MEM_HEADS = 4
N_MIX_HEADS = 12
N_KV_GROUPS = 3
HEADS_PER_GROUP = 4
CMP_STRIDE = 16
CMP_BLOCK = 32
SLC_BLOCK = 64
SLC_TOPK = 16
WINDOW = 512
N_EXPERTS = 8
Q_W = N_MIX_HEADS * HEAD_DIM
KV_W = N_KV_GROUPS * HEAD_DIM
GATE_W = 3 * N_MIX_HEADS
MEM_Q_W = N_MEM_HEADS * HEAD_DIM
NEG = -1e30
BIG = 1e6
EPS = 1e-6
SCALE = HEAD_DIM ** -0.5
STICK_DEAD = -104.0
LANES = 128
VMEM_LIMIT = 48 * 1024 * 1024
MOE_VMEM_LIMIT = 56 * 1024 * 1024
MOE_SUB = 256
MOE_TILES_PER_GROUP = 9

F32 = jnp.float32
BF16 = jnp.bfloat16


def _cparams(sem):
    return pltpu.CompilerParams(dimension_semantics=sem, vmem_limit_bytes=VMEM_LIMIT)


def _nt_dot(a, b):
    return lax.dot_general(a, b, (((1,), (1,)), ((), ())), preferred_element_type=F32)


def _rms(x, gain):
    return x * lax.rsqrt(jnp.mean(x * x, axis=-1, keepdims=True) + EPS) * gain


def _sigmoid(x):
    return 1.0 / (1.0 + jnp.exp(-x))


def _rmsnorm_cast_kernel(x_ref, g_ref, o_ref):
    o_ref[...] = _rms(x_ref[...], g_ref[...]).astype(o_ref.dtype)


def _rmsnorm_cast(x, gain, tm=512):
    m, d = x.shape
    tm = min(tm, m)
    return pl.pallas_call(
        _rmsnorm_cast_kernel,
        out_shape=jax.ShapeDtypeStruct((m, d), BF16),
        grid=(m // tm,),
        in_specs=[pl.BlockSpec((tm, d), lambda i: (i, 0)), pl.BlockSpec((1, d), lambda i: (0, 0))],
        out_specs=pl.BlockSpec((tm, d), lambda i: (i, 0)),
        compiler_params=_cparams(("parallel",)),
        name="rmsnorm_cast",
    )(x, gain.reshape(1, d))


def _norm_matmul_kernel(x_ref, g_ref, w_ref, o_ref, xn_ref):
    @pl.when(pl.program_id(1) == 0)
    def _():
        xn_ref[...] = _rms(x_ref[...], g_ref[...]).astype(BF16)

    o_ref[...] = jnp.dot(xn_ref[...], w_ref[...].astype(BF16), preferred_element_type=F32).astype(o_ref.dtype)


def _norm_matmul(x, gain, w, *, tm, tn, out_dtype=F32, name="norm_matmul"):
    m, d = x.shape
    n = w.shape[1]
    tm = min(tm, m)
    return pl.pallas_call(
        _norm_matmul_kernel,
        out_shape=jax.ShapeDtypeStruct((m, n), out_dtype),
        grid=(m // tm, pl.cdiv(n, tn)),
        in_specs=[pl.BlockSpec((tm, d), lambda i, j: (i, 0)),
                  pl.BlockSpec((1, d), lambda i, j: (0, 0)),
                  pl.BlockSpec((d, tn), lambda i, j: (0, j))],
        out_specs=pl.BlockSpec((tm, tn), lambda i, j: (i, j)),
        scratch_shapes=[pltpu.VMEM((tm, d), BF16)],
        compiler_params=_cparams(("parallel", "arbitrary")),
        name=name,
    )(x, gain.reshape(1, d), w)


def _matmul_kernel(*refs, nk, has_res):
    if has_res:
        a_ref, w_ref, r_ref, o_ref = refs[:4]
    else:
        a_ref, w_ref, o_ref = refs[:3]
        r_ref = None
    part = jnp.dot(a_ref[...], w_ref[...].astype(BF16), preferred_element_type=F32)
    if nk == 1:
        if has_res:
            part = part + r_ref[...]
        o_ref[...] = part.astype(o_ref.dtype)
        return
    acc_ref = refs[-1]
    k = pl.program_id(2)

    @pl.when(k == 0)
    def _():
        acc_ref[...] = part

    @pl.when(k > 0)
    def _():
        acc_ref[...] += part

    @pl.when(k == nk - 1)
    def _():
        r = acc_ref[...]
        if has_res:
            r = r + r_ref[...]
        o_ref[...] = r.astype(o_ref.dtype)


def _matmul(a, w, res=None, *, tm, tn, tk=None, out_dtype=F32, name="matmul"):
    m, kdim = a.shape
    n = w.shape[1]
    tm = min(tm, m)
    tk = kdim if tk is None else tk
    nk = kdim // tk
    in_specs = [pl.BlockSpec((tm, tk), lambda i, j, k: (i, k)),
                pl.BlockSpec((tk, tn), lambda i, j, k: (k, j))]
    args = [a, w]
    if res is not None:
        in_specs.append(pl.BlockSpec((tm, tn), lambda i, j, k: (i, j)))
        args.append(res)
    scratch = [pltpu.VMEM((tm, tn), F32)] if nk > 1 else []
    return pl.pallas_call(
        functools.partial(_matmul_kernel, nk=nk, has_res=res is not None),
        out_shape=jax.ShapeDtypeStruct((m, n), out_dtype),
        grid=(m // tm, pl.cdiv(n, tn), nk),
        in_specs=in_specs,
        out_specs=pl.BlockSpec((tm, tn), lambda i, j, k: (i, j)),
        scratch_shapes=scratch,
        compiler_params=_cparams(("parallel", "parallel", "arbitrary")),
        name=name,
    )(*args)


def _out_proj_kernel(mix_ref, mo_ref, w1_ref, w2_ref, r_ref, o_ref):
    o_ref[...] = (jnp.dot(mix_ref[...], w1_ref[...].astype(BF16), preferred_element_type=F32)
                  + jnp.dot(mo_ref[...], w2_ref[...].astype(BF16), preferred_element_type=F32)
                  + r_ref[...])


def _out_proj(mix, mo, w, res, tm=1024, tn=512):
    t = mix.shape[0]
    tm = min(tm, t)
    n = w.shape[1]
    assert Q_W % MEM_Q_W == 0
    return pl.pallas_call(
        _out_proj_kernel,
        out_shape=jax.ShapeDtypeStruct((t, n), F32),
        grid=(t // tm, n // tn),
        in_specs=[pl.BlockSpec((tm, Q_W), lambda i, j: (i, 0)),
                  pl.BlockSpec((tm, MEM_Q_W), lambda i, j: (i, 0)),
                  pl.BlockSpec((Q_W, tn), lambda i, j: (0, j)),
                  pl.BlockSpec((MEM_Q_W, tn), lambda i, j: (Q_W // MEM_Q_W, j)),
                  pl.BlockSpec((tm, tn), lambda i, j: (i, j))],
        out_specs=pl.BlockSpec((tm, tn), lambda i, j: (i, j)),
        compiler_params=_cparams(("parallel", "parallel")),
        name="out_proj",
    )(mix, mo, w, w, res)


def _swiglu_kernel(x_ref, gn_ref, wg_ref, wu_ref, o_ref, xn_ref):
    @pl.when(pl.program_id(1) == 0)
    def _():
        xn_ref[...] = _rms(x_ref[...], gn_ref[...]).astype(BF16)

    a = xn_ref[...]
    g = jnp.dot(a, wg_ref[...].astype(BF16), preferred_element_type=F32)
    u = jnp.dot(a, wu_ref[...].astype(BF16), preferred_element_type=F32)
    o_ref[...] = (g * _sigmoid(g) * u).astype(o_ref.dtype)


def _swiglu_up(x, gain, w_gu, *, tm=1024, tn=256):
    m, d = x.shape
    f = w_gu.shape[1] // 2
    tm = min(tm, m)
    nj = f // tn
    return pl.pallas_call(
        _swiglu_kernel,
        out_shape=jax.ShapeDtypeStruct((m, f), BF16),
        grid=(m // tm, nj),
        in_specs=[pl.BlockSpec((tm, d), lambda i, j: (i, 0)),
                  pl.BlockSpec((1, d), lambda i, j: (0, 0)),
                  pl.BlockSpec((d, tn), lambda i, j: (0, j)),
                  pl.BlockSpec((d, tn), lambda i, j: (0, j + nj))],
        out_specs=pl.BlockSpec((tm, tn), lambda i, j: (i, j)),
        scratch_shapes=[pltpu.VMEM((tm, d), BF16)],
        compiler_params=_cparams(("parallel", "arbitrary")),
        name="swiglu_up",
    )(x, gain.reshape(1, d), w_gu, w_gu)


def _nsa_prep_kernel(p_ref, gtab_ref, o_ref):
    for c in range((Q_W + 6 * KV_W) // HEAD_DIM):
        x = p_ref[:, c * HEAD_DIM:(c + 1) * HEAD_DIM]
        if c < N_MIX_HEADS:
            y = _rms(x, gtab_ref[0:1, :]) * SCALE
        elif 18 <= c < 21:
            y = _rms(x, gtab_ref[1:2, :])
        elif 24 <= c < 27:
            y = _rms(x, gtab_ref[2:3, :])
        else:
            y = x
        o_ref[:, c * HEAD_DIM:(c + 1) * HEAD_DIM] = y.astype(o_ref.dtype)


def _nsa_prep(p, q_gain, k_gain, tm=256):
    t = p.shape[0]
    tm = min(tm, t)
    w = Q_W + 6 * KV_W
    gtab = jnp.zeros((8, HEAD_DIM), F32).at[0].set(q_gain).at[1].set(k_gain[1]).at[2].set(k_gain[2])
    return pl.pallas_call(
        _nsa_prep_kernel,
        out_shape=jax.ShapeDtypeStruct((t, w), BF16),
        grid=(t // tm,),
        in_specs=[pl.BlockSpec((tm, w), lambda i: (i, 0)), pl.BlockSpec((8, HEAD_DIM), lambda i: (0, 0))],
        out_specs=pl.BlockSpec((tm, w), lambda i: (i, 0)),
        compiler_params=_cparams(("parallel",)),
        name="nsa_prep",
    )(p, gtab)


def _compress_kernel(b_ref, pos_ref, w_ref, g_ref, o_ref):
    blk = (b_ref[0, 0] + pos_ref[0]).astype(BF16)
    y = jnp.dot(blk, w_ref[0].astype(BF16), preferred_element_type=F32)
    is_k = pl.program_id(0) == 0
    o_ref[0, 0] = jnp.where(is_k, _rms(y, g_ref[...]), y).astype(o_ref.dtype)


def _compress(blocks, cmp_w, cmp_pos, k_gain0):
    _, g, ncp, bw = blocks.shape
    return pl.pallas_call(
        _compress_kernel,
        out_shape=jax.ShapeDtypeStruct((2, g, ncp, HEAD_DIM), BF16),
        grid=(2, g),
        in_specs=[pl.BlockSpec((1, 1, ncp, bw), lambda s, gi: (s, gi, 0, 0)),
                  pl.BlockSpec((1, 1, bw), lambda s, gi: (s, 0, 0)),
                  pl.BlockSpec((1, bw, HEAD_DIM), lambda s, gi: (s, 0, 0)),
                  pl.BlockSpec((1, HEAD_DIM), lambda s, gi: (0, 0))],
        out_specs=pl.BlockSpec((1, 1, ncp, HEAD_DIM), lambda s, gi: (s, gi, 0, 0)),
        compiler_params=_cparams(("parallel", "parallel")),
        name="nsa_compress",
    )(blocks, cmp_pos.reshape(2, 1, bw), cmp_w, k_gain0.reshape(1, HEAD_DIM))


def _cmp_kernel(slopes_ref, q_ref, k_ref, v_ref, imap_ref, o_ref, score_ref, *, tq, ncp):
    g = pl.program_id(0)
    t0 = pl.program_id(1) * tq
    row = lax.broadcasted_iota(jnp.int32, (tq, ncp), 0) + t0
    col = lax.broadcasted_iota(jnp.int32, (tq, ncp), 1)
    c_end = col * CMP_STRIDE + (CMP_BLOCK - 1)
    neg = jnp.where(row >= c_end, 0.0, NEG)
    colb = (lax.broadcasted_iota(jnp.int32, (1, ncp), 1) * CMP_STRIDE + (CMP_BLOCK - 1) - t0).astype(F32)
    has_key = (lax.broadcasted_iota(jnp.int32, (tq, 1), 0) + t0 >= CMP_BLOCK - 1).astype(F32)
    k = k_ref[0, 0]
    v = v_ref[0, 0]
    psum = jnp.zeros((tq, ncp), F32)
    for p in range(HEADS_PER_GROUP):
        slope = slopes_ref[g * HEADS_PER_GROUP + p]
        q = q_ref[:, p * HEAD_DIM:(p + 1) * HEAD_DIM]
        s = _nt_dot(q, k) + slope * colb + neg
        e = jnp.exp(s - jnp.max(s, axis=1, keepdims=True))
        pr = e * (has_key / jnp.sum(e, axis=1, keepdims=True))
        psum = psum + pr
        o_ref[:, p * HEAD_DIM:(p + 1) * HEAD_DIM] = jnp.dot(
            pr.astype(BF16), v, preferred_element_type=F32).astype(o_ref.dtype)

    imap = imap_ref[...]
    hi = psum.astype(BF16)
    r1 = psum - hi.astype(F32)
    mid = r1.astype(BF16)
    lo = (r1 - mid.astype(F32)).astype(BF16)
    imp = (jnp.dot(hi, imap, preferred_element_type=F32)
           + jnp.dot(mid, imap, preferred_element_type=F32)
           + jnp.dot(lo, imap, preferred_element_type=F32))
    nb = imp.shape[1]
    t = lax.broadcasted_iota(jnp.int32, (tq, nb), 0) + t0
    blk = lax.broadcasted_iota(jnp.int32, (tq, nb), 1)
    score_ref[0] = jnp.where(blk == t // SLC_BLOCK, 2 * BIG,
                             jnp.where(blk == 0, BIG,
                                       jnp.where(blk * SLC_BLOCK <= t, imp, -BIG)))


def _cmp_attn(slopes, qkv, kvc, imap, tq=256):
    t = qkv.shape[0]
    tq = min(tq, t)
    ncp = kvc.shape[2]
    nb = t // SLC_BLOCK
    gw = HEADS_PER_GROUP * HEAD_DIM
    return pl.pallas_call(
        functools.partial(_cmp_kernel, tq=tq, ncp=ncp),
        out_shape=(jax.ShapeDtypeStruct((t, Q_W), F32),
                   jax.ShapeDtypeStruct((N_KV_GROUPS, t, nb), F32)),
        grid_spec=pltpu.PrefetchScalarGridSpec(
            num_scalar_prefetch=1,
            grid=(N_KV_GROUPS, t // tq),
            in_specs=[pl.BlockSpec((tq, gw), lambda g, i, s: (i, g)),
                      pl.BlockSpec((1, 1, ncp, HEAD_DIM), lambda g, i, s: (0, g, 0, 0)),
                      pl.BlockSpec((1, 1, ncp, HEAD_DIM), lambda g, i, s: (1, g, 0, 0)),
                      pl.BlockSpec((ncp, nb), lambda g, i, s: (0, 0))],
            out_specs=(pl.BlockSpec((tq, gw), lambda g, i, s: (i, g)),
                       pl.BlockSpec((1, tq, nb), lambda g, i, s: (g, i, 0)))),
        compiler_params=_cparams(("parallel", "parallel")),
        name="nsa_cmp",
    )(slopes, qkv, kvc, kvc, imap)


def _topk_kernel(score_ref, sel_ref, act_ref, *, tm, sub):
    score = score_ref[0]
    nb = score.shape[1]
    blk_f = lax.broadcasted_iota(jnp.int32, (tm, nb), 1).astype(F32)

    def pick_one(_, carry):
        work, sel = carry
        m = jnp.max(work, axis=1, keepdims=True)
        first = jnp.min(jnp.where(work == m, blk_f, float(nb)), axis=1, keepdims=True)
        pick = blk_f == first
        return jnp.where(pick, -3e38, work), jnp.where(pick, 1.0, sel)

    _, sel = lax.fori_loop(0, min(SLC_TOPK, nb), pick_one, (score, jnp.zeros((tm, nb), F32)))
    sel_ref[0] = sel.astype(sel_ref.dtype)
    for a in range(tm // sub):
        act_ref[0, a] = jnp.broadcast_to(jnp.max(sel[a * sub:(a + 1) * sub], axis=0, keepdims=True),
                                         act_ref.shape[2:])


def _topk_select(score, sub, tm=1024):
    g, t, nb = score.shape
    tm = min(tm, t)
    sub = min(sub, tm)
    return pl.pallas_call(
        functools.partial(_topk_kernel, tm=tm, sub=sub),
        out_shape=(jax.ShapeDtypeStruct((g, t, nb), BF16),
                   jax.ShapeDtypeStruct((g, t // sub, 8, nb), F32)),
        grid=(g, t // tm),
        in_specs=[pl.BlockSpec((1, tm, nb), lambda gi, i: (gi, i, 0))],
        out_specs=(pl.BlockSpec((1, tm, nb), lambda gi, i: (gi, i, 0)),
                   pl.BlockSpec((1, tm // sub, 8, nb), lambda gi, i: (gi, i, 0, 0))),
        compiler_params=_cparams(("parallel", "parallel")),
        name="nsa_topk",
    )(score)


def _slc_kernel(slopes_ref, flags_ref, q_ref, k_ref, v_ref, sel_ref, o_ref, m_ref, l_ref, acc_ref, *, tq, tk):
    g = pl.program_id(0)
    i = pl.program_id(1)
    t0 = i * tq
    flag_base = (g * pl.num_programs(1) + i) * (k_ref.shape[0] // tk)
    nb = sel_ref.shape[2]
    m_ref[...] = jnp.full(m_ref.shape, NEG, F32)
    l_ref[...] = jnp.zeros(l_ref.shape, F32)
    acc_ref[...] = jnp.zeros(acc_ref.shape, F32)
    sel = sel_ref[0]
    bpt = tk // SLC_BLOCK

    def body(j, carry):
        @pl.when(flags_ref[flag_base + j] > 0)
        def _():
            tile(j)
        return carry

    def tile(j):
        k0 = pl.multiple_of(j * tk, tk)
        k = k_ref[pl.ds(k0, tk), :]
        v = v_ref[pl.ds(k0, tk), :]
        eb = lax.broadcasted_iota(jnp.int32, (nb, tk), 0)
        ec = lax.broadcasted_iota(jnp.int32, (nb, tk), 1)
        expand = jnp.where(eb == ec // SLC_BLOCK + j * bpt, 1.0, 0.0).astype(BF16)
        selx = jnp.dot(sel, expand, preferred_element_type=F32)
        rel = (t0 - k0) + lax.broadcasted_iota(jnp.int32, (tq, tk), 0) \
            - lax.broadcasted_iota(jnp.int32, (tq, tk), 1)
        neg = jnp.where(jnp.where(rel >= 0, selx, 0.0) > 0.5, 0.0, NEG)
        colb = (lax.broadcasted_iota(jnp.int32, (1, tk), 1) + (k0 - t0)).astype(F32)
        for p in range(HEADS_PER_GROUP):
            slope = slopes_ref[g * HEADS_PER_GROUP + p]
            q = q_ref[:, p * HEAD_DIM:(p + 1) * HEAD_DIM]
            s = _nt_dot(q, k) + slope * colb + neg
            m_old = m_ref[p]
            m_new = jnp.maximum(m_old, jnp.max(s, axis=1, keepdims=True))
            alpha = jnp.exp(m_old - m_new)
            e = jnp.exp(s - jnp.tile(m_new, (1, tk // LANES)))
            l_ref[p] = alpha * l_ref[p] + jnp.sum(e, axis=1, keepdims=True)
            acc_ref[p] = alpha * acc_ref[p] + jnp.dot(e.astype(BF16), v, preferred_element_type=F32)
            m_ref[p] = m_new

    n_tiles = (t0 + tq - 1) // tk + 1
    lax.fori_loop(0, n_tiles, body, 0)
    for p in range(HEADS_PER_GROUP):
        o_ref[:, p * HEAD_DIM:(p + 1) * HEAD_DIM] = (acc_ref[p] / l_ref[p]).astype(o_ref.dtype)


def _slc_attn(slopes, qkv, sel, blk_act, tq=256, tk=256):
    t = qkv.shape[0]
    tq = min(tq, t)
    tk = min(tk, t)
    nb = sel.shape[2]
    assert blk_act.shape[1] == t // tq and tq % tk == 0
    gw = HEADS_PER_GROUP * HEAD_DIM
    ks_col = (Q_W + 2 * KV_W) // HEAD_DIM
    vs_col = (Q_W + 3 * KV_W) // HEAD_DIM
    flags = jnp.max(blk_act.reshape(N_KV_GROUPS, t // tq, t // tk, tk // SLC_BLOCK), axis=-1)
    flags = (flags > 0.5).astype(jnp.int32).reshape(-1)
    return pl.pallas_call(
        functools.partial(_slc_kernel, tq=tq, tk=tk),
        out_shape=jax.ShapeDtypeStruct((t, Q_W), F32),
        grid_spec=pltpu.PrefetchScalarGridSpec(
            num_scalar_prefetch=2,
            grid=(N_KV_GROUPS, t // tq),
            in_specs=[pl.BlockSpec((tq, gw), lambda g, i, s, f: (i, g)),
                      pl.BlockSpec((t, HEAD_DIM), lambda g, i, s, f: (0, ks_col + g)),
                      pl.BlockSpec((t, HEAD_DIM), lambda g, i, s, f: (0, vs_col + g)),
                      pl.BlockSpec((1, tq, nb), lambda g, i, s, f: (g, i, 0))],
            out_specs=pl.BlockSpec((tq, gw), lambda g, i, s, f: (i, g)),
            scratch_shapes=[pltpu.VMEM((HEADS_PER_GROUP, tq, LANES), F32),
                            pltpu.VMEM((HEADS_PER_GROUP, tq, LANES), F32),
                            pltpu.VMEM((HEADS_PER_GROUP, tq, HEAD_DIM), F32)]),
        compiler_params=_cparams(("parallel", "parallel")),
        name="nsa_selected",
    )(slopes, flags, qkv, qkv, qkv, sel)


def _win_kernel(slopes_ref, q_ref, k0_ref, k1_ref, k2_ref, v0_ref, v1_ref, v2_ref, o_ref, *, tq):
    g = pl.program_id(0)
    i = pl.program_id(1)
    nprev = WINDOW // tq
    kw = (nprev + 1) * tq
    k = jnp.concatenate([k0_ref[...], k1_ref[...], k2_ref[...]], axis=0)
    v = jnp.concatenate([v0_ref[...], v1_ref[...], v2_ref[...]], axis=0)
    r = lax.broadcasted_iota(jnp.int32, (tq, kw), 0)
    c = lax.broadcasted_iota(jnp.int32, (tq, kw), 1)
    rel = WINDOW + r - c
    first_valid = (nprev - i) * tq
    inside = jnp.where(rel >= 0, jnp.where(rel < WINDOW, 1.0, 0.0), 0.0)
    neg = jnp.where(jnp.where(c >= first_valid, inside, 0.0) > 0.5, 0.0, NEG)
    colb = (lax.broadcasted_iota(jnp.int32, (1, kw), 1) - WINDOW).astype(F32)
    for p in range(HEADS_PER_GROUP):
        slope = slopes_ref[g * HEADS_PER_GROUP + p]
        q = q_ref[:, p * HEAD_DIM:(p + 1) * HEAD_DIM]
        s = _nt_dot(q, k) + slope * colb + neg
        e = jnp.exp(s - jnp.max(s, axis=1, keepdims=True))
        o = jnp.dot(e.astype(BF16), v, preferred_element_type=F32) / jnp.sum(e, axis=1, keepdims=True)
        o_ref[:, p * HEAD_DIM:(p + 1) * HEAD_DIM] = o.astype(o_ref.dtype)


def _win_attn(slopes, qkv, tq=256):
    t = qkv.shape[0]
    assert WINDOW % tq == 0 and WINDOW // tq == 2
    gw = HEADS_PER_GROUP * HEAD_DIM
    kw_col = (Q_W + 4 * KV_W) // HEAD_DIM
    vw_col = (Q_W + 5 * KV_W) // HEAD_DIM

    def kv_spec(col, back):
        return pl.BlockSpec((tq, HEAD_DIM), lambda g, i, s: (jnp.maximum(i - back, 0), col + g))

    return pl.pallas_call(
        functools.partial(_win_kernel, tq=tq),
        out_shape=jax.ShapeDtypeStruct((t, Q_W), F32),
        grid_spec=pltpu.PrefetchScalarGridSpec(
            num_scalar_prefetch=1,
            grid=(N_KV_GROUPS, t // tq),
            in_specs=[pl.BlockSpec((tq, gw), lambda g, i, s: (i, g)),
                      kv_spec(kw_col, 2), kv_spec(kw_col, 1), kv_spec(kw_col, 0),
                      kv_spec(vw_col, 2), kv_spec(vw_col, 1), kv_spec(vw_col, 0)],
            out_specs=pl.BlockSpec((tq, gw), lambda g, i, s: (i, g))),
        compiler_params=_cparams(("parallel", "parallel")),
        name="nsa_window",
    )(slopes, qkv, qkv, qkv, qkv, qkv, qkv, qkv)


def _gate_sum_kernel(oc_ref, os_ref, ow_ref, g_ref, o_ref):
    sg = _sigmoid(g_ref[...])
    for h in range(N_MIX_HEADS):
        sl = slice(h * HEAD_DIM, (h + 1) * HEAD_DIM)
        o = (oc_ref[:, sl] * sg[:, 3 * h:3 * h + 1]
             + os_ref[:, sl] * sg[:, 3 * h + 1:3 * h + 2]
             + ow_ref[:, sl] * sg[:, 3 * h + 2:3 * h + 3])
        o_ref[:, sl] = o.astype(o_ref.dtype)


def _gate_sum(o_cmp, o_slc, o_win, gates, tm=512):
    t = o_cmp.shape[0]
    tm = min(tm, t)
    big = pl.BlockSpec((tm, Q_W), lambda i: (i, 0))
    return pl.pallas_call(
        _gate_sum_kernel,
        out_shape=jax.ShapeDtypeStruct((t, Q_W), BF16),
        grid=(t // tm,),
        in_specs=[big, big, big, pl.BlockSpec((tm, GATE_W), lambda i: (i, 0))],
        out_specs=big,
        compiler_params=_cparams(("parallel",)),
        name="nsa_gate_sum",
    )(o_cmp, o_slc, o_win, gates)


def _mem_attn_kernel(q_ref, k_ref, v_ref, g_ref, o_ref):
    q = (_rms(q_ref[...], g_ref[0:1, :]) * SCALE).astype(BF16)
    k = _rms(k_ref[...], g_ref[1:2, :]).astype(BF16)
    s = _nt_dot(q, k)
    e = jnp.exp(s - jnp.max(s, axis=1, keepdims=True))
    p = e * (1.0 / jnp.sum(e, axis=1, keepdims=True))
    o_ref[...] = jnp.dot(p.astype(BF16), v_ref[...].astype(BF16),
                         preferred_element_type=F32).astype(o_ref.dtype)


def _mem_attn(qsrc, q_col, memkv, qk_gain, tq=512):
    t = qsrc.shape[0]
    tq = min(tq, t)
    nm = memkv.shape[0]
    gtab = jnp.zeros((8, HEAD_DIM), F32).at[0:2].set(qk_gain)
    return pl.pallas_call(
        _mem_attn_kernel,
        out_shape=jax.ShapeDtypeStruct((t, MEM_Q_W), BF16),
        grid=(N_MEM_HEADS, t // tq),
        in_specs=[pl.BlockSpec((tq, HEAD_DIM), lambda h, i: (i, q_col + h)),
                  pl.BlockSpec((nm, HEAD_DIM), lambda h, i: (0, h)),
                  pl.BlockSpec((nm, HEAD_DIM), lambda h, i: (0, N_MEM_HEADS + h)),
                  pl.BlockSpec((8, HEAD_DIM), lambda h, i: (0, 0))],
        out_specs=pl.BlockSpec((tq, HEAD_DIM), lambda h, i: (i, h)),
        compiler_params=_cparams(("parallel", "parallel")),
        name="mem_attn",
    )(qsrc, memkv, memkv, gtab)


def _stick_kernel(q_ref, k_ref, v_ref, o_ref, *, tq, hp):
    i = pl.program_id(1)
    r = lax.broadcasted_iota(jnp.int32, (tq, tq), 0)
    c = lax.broadcasted_iota(jnp.int32, (tq, tq), 1)
    later = jnp.where(r > c, 1.0, 0.0).astype(BF16)
    qs = [(q_ref[:, h * HEAD_DIM:(h + 1) * HEAD_DIM] * SCALE).astype(BF16) for h in range(hp)]

    def tile(h, k0, tail, acc, diagonal):
        k = k_ref[pl.ds(k0, tq), h * HEAD_DIM:(h + 1) * HEAD_DIM]
        v = v_ref[pl.ds(k0, tq), h * HEAD_DIM:(h + 1) * HEAD_DIM]
        z = _nt_dot(qs[h], k)
        sp = jnp.maximum(z, 0.0) + jnp.log(1.0 + jnp.exp(-jnp.abs(z)))
        log_keep = jnp.where(r > c, -sp, 0.0) if diagonal else -sp
        hi = log_keep.astype(BF16)
        lo = (log_keep - hi.astype(F32)).astype(BF16)
        after = (jnp.dot(hi, later, preferred_element_type=F32)
                 + jnp.dot(lo, later, preferred_element_type=F32) + tail)
        a = jnp.exp(z - sp + after)
        if diagonal:
            a = jnp.where(r > c, a, 0.0)
        acc = acc + jnp.dot(a.astype(BF16), v, preferred_element_type=F32)
        return tail + jnp.sum(log_keep, axis=1, keepdims=True), acc

    k_diag = pl.multiple_of(i * tq, tq)
    state = [tile(h, k_diag, jnp.zeros((tq, 1), F32), jnp.zeros((tq, HEAD_DIM), F32), True) for h in range(hp)]

    def live(st):
        return jnp.logical_and(st[0] <= i, st[1] > 0)

    def body(st):
        k0 = pl.multiple_of((i - st[0]) * tq, tq)
        new = [tile(h, k0, st[2 + 2 * h], st[3 + 2 * h], False) for h in range(hp)]
        top = functools.reduce(jnp.maximum, [jnp.max(tl) for tl, _ in new])
        return (st[0] + 1, (top > STICK_DEAD).astype(jnp.int32)) + tuple(x for pair in new for x in pair)

    st = lax.while_loop(live, body, (jnp.int32(1), jnp.int32(1)) + tuple(x for pair in state for x in pair))
    for h in range(hp):
        o_ref[:, h * HEAD_DIM:(h + 1) * HEAD_DIM] = st[3 + 2 * h].astype(o_ref.dtype)


def _stick_attn(pb, kvb, tq=256, hp=2):
    t = pb.shape[0]
    tq = min(tq, t)
    hw = hp * HEAD_DIM
    return pl.pallas_call(
        functools.partial(_stick_kernel, tq=tq, hp=hp),
        out_shape=jax.ShapeDtypeStruct((t, Q_W), BF16),
        grid=(N_MIX_HEADS // hp, t // tq),
        in_specs=[pl.BlockSpec((tq, hw), lambda h, i: (i, h)),
                  pl.BlockSpec((t, hw), lambda h, i: (0, h)),
                  pl.BlockSpec((t, hw), lambda h, i: (0, N_MIX_HEADS // hp + h))],
        out_specs=pl.BlockSpec((tq, hw), lambda h, i: (i, h)),
        compiler_params=_cparams(("parallel", "parallel")),
        name="stick_breaking",
    )(pb, kvb, kvb)


def _router_kernel(x_ref, g_ref, w_ref, b_ref, c_ref):
    h = _rms(x_ref[...], g_ref[...])
    logits = jnp.dot(h, w_ref[...], preferred_element_type=F32, precision=lax.Precision.HIGHEST)
    lane = lax.broadcasted_iota(jnp.int32, logits.shape, 1).astype(F32)
    biased = jnp.where(lane < N_EXPERTS, logits + b_ref[...], -3e38)
    m1 = jnp.max(biased, axis=1, keepdims=True)
    i1 = jnp.min(jnp.where(biased == m1, lane, float(LANES)), axis=1, keepdims=True)
    rest = jnp.where(lane == i1, -3e38, biased)
    m2 = jnp.max(rest, axis=1, keepdims=True)
    i2 = jnp.min(jnp.where(rest == m2, lane, float(LANES)), axis=1, keepdims=True)
    l1 = jnp.sum(jnp.where(lane == i1, logits, 0.0), axis=1, keepdims=True)
    l2 = jnp.sum(jnp.where(lane == i2, logits, 0.0), axis=1, keepdims=True)
    mx = jnp.maximum(l1, l2)
    e1 = jnp.exp(l1 - mx)
    e2 = jnp.exp(l2 - mx)
    inv = 1.0 / (e1 + e2)
    c_ref[...] = (jnp.where(lane == 0.0, i1, 0.0) + jnp.where(lane == 1.0, i2, 0.0)
                  + jnp.where(lane == 2.0, e1 * inv, 0.0) + jnp.where(lane == 3.0, e2 * inv, 0.0))


def _router(x, gain, router_w, router_b, tm=512):
    t, d = x.shape
    tm = min(tm, t)
    w = jnp.zeros((d, LANES), F32).at[:, :N_EXPERTS].set(router_w)
    b = jnp.zeros((1, LANES), F32).at[0, :N_EXPERTS].set(router_b)
    return pl.pallas_call(
        _router_kernel,
        out_shape=jax.ShapeDtypeStruct((t, LANES), F32),
        grid=(t // tm,),
        in_specs=[pl.BlockSpec((tm, d), lambda i: (i, 0)), pl.BlockSpec((1, d), lambda i: (0, 0)),
                  pl.BlockSpec((d, LANES), lambda i: (0, 0)), pl.BlockSpec((1, LANES), lambda i: (0, 0))],
        out_specs=pl.BlockSpec((tm, LANES), lambda i: (i, 0)),
        compiler_params=_cparams(("parallel",)),
        name="moe_router",
    )(x, gain.reshape(1, d), w, b)


def _route_plan(rt, sub, per_group):
    t = rt.shape[0]
    i32 = jnp.int32
    e_flat = rt[:, 0:2].astype(i32).reshape(-1)
    onehot = (e_flat[:, None] == jnp.arange(N_EXPERTS, dtype=i32)[None, :]).astype(i32)
    csum = jnp.cumsum(onehot, axis=0)
    rank = jnp.sum(csum * onehot, axis=1) - 1
    tiles_e = (csum[-1] + sub - 1) // sub
    tile_end = jnp.cumsum(tiles_e)
    tile_start = tile_end - tiles_e
    dest = tile_start[e_flat] * sub + rank
    groups_e = (tiles_e + per_group - 1) // per_group
    group_end = jnp.cumsum(groups_e)
    n_groups = group_end[-1]
    max_groups = (2 * t // sub + N_EXPERTS) // per_group + N_EXPERTS
    gidx = jnp.arange(max_groups, dtype=i32)
    gcl = jnp.minimum(gidx, n_groups - 1)
    g_expert = jnp.sum((gcl[:, None] >= group_end[None, :]).astype(i32), axis=1)
    k_in_e = gcl - (group_end - groups_e)[g_expert]
    g_start = (tile_start[g_expert] + k_in_e * per_group) * sub
    g_tiles = jnp.where(gidx < n_groups, jnp.clip(tiles_e[g_expert] - k_in_e * per_group, 0, per_group), 0)
    return (dest.astype(i32), tile_end[-1].reshape(1).astype(i32),
            g_expert.astype(i32), g_start.astype(i32), g_tiles.astype(i32))


def _dispatch_kernel(dest_ref, nt_ref, x_hbm, g_ref, o_ref, tok_ref, xbuf, sems, *, tm):
    n = pl.program_id(0)
    n_tiles = nt_ref[0]

    def gather(tile, slot):
        base = tile * tm

        def issue(r, c):
            pltpu.make_async_copy(x_hbm.at[pl.ds(tok_ref[base + r], 1), :],
                                  xbuf.at[slot, pl.ds(r, 1), :], sems.at[slot]).start()
            return c

        lax.fori_loop(0, tm, issue, 0, unroll=8)

    @pl.when(n == 0)
    def _():
        def clear(r, c):
            tok_ref[r] = 0
            return c

        lax.fori_loop(0, tok_ref.shape[0], clear, 0, unroll=8)

        def fill(q, c):
            rows = [dest_ref[q * 8 + u] for u in range(8)]
            for u in range(8):
                tok_ref[rows[u]] = q * 4 + u // 2
            return c

        lax.fori_loop(0, dest_ref.shape[0] // 8, fill, 0)
        gather(0, 0)

    slot = n % 2

    @pl.when(n + 1 < n_tiles)
    def _():
        gather(n + 1, 1 - slot)

    @pl.when(n < n_tiles)
    def _():
        pltpu.make_async_copy(x_hbm.at[pl.ds(0, tm), :], xbuf.at[slot], sems.at[slot]).wait()
        o_ref[...] = _rms(xbuf[slot], g_ref[...]).astype(o_ref.dtype)

    @pl.when(n >= n_tiles)
    def _():
        o_ref[...] = jnp.zeros(o_ref.shape, o_ref.dtype)


def _dispatch(x, gain, dest, n_tiles, tm, nt):
    t, d = x.shape
    return pl.pallas_call(
        functools.partial(_dispatch_kernel, tm=tm),
        out_shape=jax.ShapeDtypeStruct((nt * tm, d), BF16),
        grid_spec=pltpu.PrefetchScalarGridSpec(
            num_scalar_prefetch=2,
            grid=(nt,),
            in_specs=[pl.BlockSpec(memory_space=pl.ANY), pl.BlockSpec((1, d), lambda n, dst, ntl: (0, 0))],
            out_specs=pl.BlockSpec((tm, d), lambda n, dst, ntl: (n, 0)),
            scratch_shapes=[pltpu.SMEM((nt * tm,), jnp.int32), pltpu.VMEM((2, tm, d), F32),
                            pltpu.SemaphoreType.DMA((2,))]),
        compiler_params=_cparams(("arbitrary",)),
        name="moe_dispatch",
    )(dest, n_tiles, x, gain.reshape(1, d))


def _experts_kernel(ge_ref, gs_ref, gn_ref, nt_ref, xg_hbm, wg_ref, wu_ref, wd_ref, y_hbm,
                    xbuf, yacc, wgc, wuc, wdc, sem_in, sem_out, *, sub):
    s = pl.program_id(0)
    j = pl.program_id(1)
    n_sub = gn_ref[s]
    start = pl.multiple_of(gs_ref[s], sub)

    @pl.when(jnp.logical_and(s == 0, j == 0))
    def _():
        yacc[0:sub, :] = jnp.zeros((sub, yacc.shape[1]), F32)
        first, last = nt_ref[0], y_hbm.shape[0] // sub

        def put(r, c):
            pltpu.make_async_copy(yacc.at[pl.ds(0, sub), :],
                                  y_hbm.at[pl.ds(pl.multiple_of(r * sub, sub), sub), :], sem_out).start()
            return c

        def done(r, c):
            pltpu.make_async_copy(yacc.at[pl.ds(0, sub), :], y_hbm.at[pl.ds(0, sub), :], sem_out).wait()
            return c

        lax.fori_loop(first, last, put, 0)
        lax.fori_loop(first, last, done, 0)

    @pl.when(n_sub > 0)
    def _():
        @pl.when(j == 0)
        def _():
            def fetch(r, c):
                r0 = pl.multiple_of(r * sub, sub)
                pltpu.make_async_copy(xg_hbm.at[pl.ds(start + r0, sub), :], xbuf.at[pl.ds(r0, sub), :],
                                      sem_in).start()
                return c

            def fetched(r, c):
                pltpu.make_async_copy(xg_hbm.at[pl.ds(0, sub), :], xbuf.at[pl.ds(0, sub), :], sem_in).wait()
                return c

            lax.fori_loop(0, n_sub, fetch, 0)
            yacc[...] = jnp.zeros(yacc.shape, F32)
            lax.fori_loop(0, n_sub, fetched, 0)

        wgc[...] = wg_ref[0].astype(BF16)
        wuc[...] = wu_ref[0].astype(BF16)
        wdc[...] = wd_ref[0].astype(BF16)

        def rows_at(r0, m):
            x = xbuf[pl.ds(r0, m), :]
            g = jnp.dot(x, wgc[...], preferred_element_type=F32)
            u = jnp.dot(x, wuc[...], preferred_element_type=F32)
            a = (g * _sigmoid(g) * u).astype(BF16)
            yacc[pl.ds(r0, m), :] += jnp.dot(a, wdc[...], preferred_element_type=F32)

        def tile_pair(r, c):
            rows_at(pl.multiple_of(r * 2 * sub, 2 * sub), 2 * sub)
            return c

        lax.fori_loop(0, n_sub // 2, tile_pair, 0)

        @pl.when(n_sub % 2 == 1)
        def _():
            rows_at(pl.multiple_of((n_sub - 1) * sub, sub), sub)

        @pl.when(j == pl.num_programs(1) - 1)
        def _():
            def put(r, c):
                r0 = pl.multiple_of(r * sub, sub)
                pltpu.make_async_copy(yacc.at[pl.ds(r0, sub), :], y_hbm.at[pl.ds(start + r0, sub), :],
                                      sem_out).start()
                return c

            def done(r, c):
                pltpu.make_async_copy(yacc.at[pl.ds(0, sub), :], y_hbm.at[pl.ds(0, sub), :], sem_out).wait()
                return c

            lax.fori_loop(0, n_sub, put, 0)
            lax.fori_loop(0, n_sub, done, 0)


def _moe_experts(xg, w_gu, w_down, g_expert, g_start, g_tiles, n_tiles, *, sub, per_group, tf=256):
    p, d = xg.shape
    f = w_down.shape[1]
    nj = f // tf
    rows = sub * per_group
    n_groups = g_expert.shape[0]

    def jcl(s, j, gn):
        return jnp.where(gn[s] > 0, j, nj - 1)

    return pl.pallas_call(
        functools.partial(_experts_kernel, sub=sub),
        out_shape=jax.ShapeDtypeStruct((p, d), F32),
        grid_spec=pltpu.PrefetchScalarGridSpec(
            num_scalar_prefetch=4,
            grid=(n_groups, nj),
            in_specs=[pl.BlockSpec(memory_space=pl.ANY),
                      pl.BlockSpec((1, d, tf), lambda s, j, ge, gs, gn, nt: (ge[s], 0, jcl(s, j, gn))),
                      pl.BlockSpec((1, d, tf), lambda s, j, ge, gs, gn, nt: (ge[s], 0, jcl(s, j, gn) + nj)),
                      pl.BlockSpec((1, tf, d), lambda s, j, ge, gs, gn, nt: (ge[s], jcl(s, j, gn), 0))],
            out_specs=pl.BlockSpec(memory_space=pl.ANY),
            scratch_shapes=[pltpu.VMEM((rows, d), BF16), pltpu.VMEM((rows, d), F32),
                            pltpu.VMEM((d, tf), BF16), pltpu.VMEM((d, tf), BF16), pltpu.VMEM((tf, d), BF16),
                            pltpu.SemaphoreType.DMA(()), pltpu.SemaphoreType.DMA(())]),
        compiler_params=pltpu.CompilerParams(dimension_semantics=("arbitrary", "arbitrary"),
                                             vmem_limit_bytes=MOE_VMEM_LIMIT),
        name="moe_experts",
    )(g_expert, g_start, g_tiles, n_tiles, xg, w_gu, w_gu, w_down)


def _moe_combine_kernel(dest_ref, x_ref, rt_ref, y_hbm, o_ref, ybuf, sems, *, tm):
    i = pl.program_id(0)

    def gather(tile, slot):
        base = tile * tm

        def issue(r, c):
            for k in range(2):
                pltpu.make_async_copy(y_hbm.at[pl.ds(dest_ref[2 * (base + r) + k], 1), :],
                                      ybuf.at[slot, pl.ds(k * tm + r, 1), :], sems.at[slot]).start()
            return c

        lax.fori_loop(0, tm, issue, 0, unroll=4)

    @pl.when(i == 0)
    def _():
        gather(0, 0)

    slot = i % 2

    @pl.when(i + 1 < pl.num_programs(0))
    def _():
        gather(i + 1, 1 - slot)

    pltpu.make_async_copy(y_hbm.at[pl.ds(0, 2 * tm), :], ybuf.at[slot], sems.at[slot]).wait()
    w = rt_ref[...]
    o_ref[...] = (x_ref[...] + w[:, 2:3] * ybuf[slot, 0:tm, :] + w[:, 3:4] * ybuf[slot, tm:2 * tm, :])


def _moe_combine(x, rt, y, dest, tm=256):
    t, d = x.shape
    tm = min(tm, t)
    return pl.pallas_call(
        functools.partial(_moe_combine_kernel, tm=tm),
        out_shape=jax.ShapeDtypeStruct((t, d), F32),
        grid_spec=pltpu.PrefetchScalarGridSpec(
            num_scalar_prefetch=1,
            grid=(t // tm,),
            in_specs=[pl.BlockSpec((tm, d), lambda i, dst: (i, 0)),
                      pl.BlockSpec((tm, LANES), lambda i, dst: (i, 0)),
                      pl.BlockSpec(memory_space=pl.ANY)],
            out_specs=pl.BlockSpec((tm, d), lambda i, dst: (i, 0)),
            scratch_shapes=[pltpu.VMEM((2, 2 * tm, d), F32), pltpu.SemaphoreType.DMA((2,))]),
        compiler_params=_cparams(("arbitrary",)),
        name="moe_combine",
    )(dest, x, rt, y)


def _moe_block(xs, gain, router_w, router_b, w_gu, w_down):
    t = xs.shape[0]
    sub = min(MOE_SUB, t)
    nt = 2 * t // sub + N_EXPERTS
    rt = _router(xs, gain, router_w, router_b)
    dest, n_tiles, g_expert, g_start, g_tiles = _route_plan(rt, sub, MOE_TILES_PER_GROUP)
    xg = _dispatch(xs, gain, dest, n_tiles, sub, nt)
    y = _moe_experts(xg, w_gu, w_down, g_expert, g_start, g_tiles, n_tiles,
                     sub=sub, per_group=MOE_TILES_PER_GROUP)
    return _moe_combine(xs, rt, y, dest)


def _alibi_slopes():
    return jnp.asarray(np.array([2.0 ** (-8.0 * (i + 1) / N_MIX_HEADS) for i in range(N_MIX_HEADS)],
                                dtype=np.float32))


def _importance_map(ncp, nb):
    ratio = SLC_BLOCK // CMP_STRIDE
    c = np.arange(ncp)[:, None]
    b = np.arange(nb)[None, :]
    m = (c >= ratio * b - 1) & (c <= ratio * b + ratio - 1) & (c < ncp - 1)
    return jnp.asarray(m.astype(np.float32), dtype=BF16)


def _compress_blocks(p, t):
    n = t // CMP_STRIDE
    z = p.reshape(n, CMP_STRIDE, 2, N_KV_GROUPS, HEAD_DIM).transpose(2, 3, 0, 1, 4)
    z = z.reshape(2, N_KV_GROUPS, n, CMP_STRIDE * HEAD_DIM)
    nxt = jnp.concatenate([z[:, :, 1:], jnp.zeros_like(z[:, :, :1])], axis=2)
    return jnp.concatenate([z, nxt], axis=3)


def _nsa_mix(p, slopes, q_gain, k_gain, cmp_w, cmp_pos, slc_tq=256):
    t = p.shape[0]
    qkv = _nsa_prep(p, q_gain, k_gain)
    blocks = _compress_blocks(p[:, Q_W:Q_W + 2 * KV_W], t)
    kvc = _compress(blocks, cmp_w, cmp_pos.reshape(2, -1), k_gain[0])
    imap = _importance_map(t // CMP_STRIDE, t // SLC_BLOCK)
    o_cmp, score = _cmp_attn(slopes, qkv, kvc, imap)
    sel, blk_act = _topk_select(score, sub=slc_tq)
    o_slc = _slc_attn(slopes, qkv, sel, blk_act[:, :, 0, :], tq=slc_tq)
    o_win = _win_attn(slopes, qkv)
    gates = p[:, Q_W + 6 * KV_W:Q_W + 6 * KV_W + GATE_W]
    return _gate_sum(o_cmp, o_slc, o_win, gates)


def kernel(x, mem, attn_norm, mem_norm, ffn_norm, w_in_a, cmp_w, cmp_pos, nsa_q_norm, nsa_k_norm, w_in_b,
           kv_norm_b, w_kv_b, w_mem_kv, mem_qk_norm, w_out, ffn_w_gu, ffn_w_down, router_w, router_b,
           moe_w_gu, moe_w_down):
    b, t, d = x.shape
    assert b == 1
    xs = x[0]
    mems = mem[0]
    slopes = _alibi_slopes()

    p = _norm_matmul(xs, attn_norm[0], w_in_a[0], tm=1024, tn=640, name="in_proj_a")
    mix = _nsa_mix(p, slopes, nsa_q_norm[0], nsa_k_norm[0], cmp_w[0], cmp_pos[0])
    memkv = _norm_matmul(mems, mem_norm[0], w_mem_kv[0], tm=256, tn=512, name="mem_kv")
    qm = p[:, Q_W + 6 * KV_W + GATE_W:]
    mo = _mem_attn(qm, 0, memkv, mem_qk_norm[0])
    xs = _out_proj(mix, mo, w_out[0], xs)
    act = _swiglu_up(xs, ffn_norm[0], ffn_w_gu[0])
    xs = _matmul(act, ffn_w_down[0], xs, tm=1024, tn=1024, tk=512, name="ffn_down")

    kvb = _norm_matmul(xs, kv_norm_b, w_kv_b, tm=1024, tn=512, out_dtype=BF16, name="kv_proj_b")
    pb = _norm_matmul(xs, attn_norm[1], w_in_b[0], tm=1024, tn=512, name="in_proj_b")
    mix = _stick_attn(pb, kvb)
    memkv = _norm_matmul(mems, mem_norm[1], w_mem_kv[1], tm=256, tn=512, name="mem_kv")
    mo = _mem_attn(pb, N_MIX_HEADS, memkv, mem_qk_norm[1])
    xs = _out_proj(mix, mo, w_out[1], xs)
    xs = _moe_block(xs, ffn_norm[1], router_w[0], router_b[0], moe_w_gu[0], moe_w_down[0])
    return xs[None]
```

```python
import functools

import numpy as np
import jax
import jax.numpy as jnp
from jax import lax
from jax.experimental import pallas as pl
from jax.experimental.pallas import tpu as pltpu

HEAD_DIM = 128
N_MEM_HEADS = 4
N_MIX_HEADS = 12
N_KV_GROUPS = 3
HEADS_PER_GROUP = 4
CMP_STRIDE = 16
CMP_BLOCK = 32
SLC_BLOCK = 64
SLC_TOPK = 16
WINDOW = 512
N_EXPERTS = 8
Q_W = N_MIX_HEADS * HEAD_DIM
KV_W = N_KV_GROUPS * HEAD_DIM
GATE_W = 3 * N_MIX_HEADS
MEM_Q_W = N_MEM_HEADS * HEAD_DIM
NEG = -1e30
BIG = 1e6
EPS = 1e-6
SCALE = HEAD_DIM ** -0.5
LOG2E = 1.4426950408889634
STICK_DEAD = -151.0
LANES = 128
VMEM_LIMIT = 48 * 1024 * 1024
MOE_VMEM_LIMIT = 56 * 1024 * 1024
MOE_SUB = 256
MOE_TILES_PER_GROUP = 9

F32 = jnp.float32
BF16 = jnp.bfloat16


def _cparams(sem):
    return pltpu.CompilerParams(dimension_semantics=sem, vmem_limit_bytes=VMEM_LIMIT)


def _nt_dot(a, b):
    return lax.dot_general(a, b, (((1,), (1,)), ((), ())), preferred_element_type=F32)


def _rms(x, gain):
    return x * lax.rsqrt(jnp.mean(x * x, axis=-1, keepdims=True) + EPS) * gain


def _sigmoid(x):
    return 1.0 / (1.0 + jnp.exp(-x))


def _rmsnorm_cast_kernel(x_ref, g_ref, o_ref):
    o_ref[...] = _rms(x_ref[...], g_ref[...]).astype(o_ref.dtype)


def _rmsnorm_cast(x, gain, tm=512):
    m, d = x.shape
    tm = min(tm, m)
    return pl.pallas_call(
        _rmsnorm_cast_kernel,
        out_shape=jax.ShapeDtypeStruct((m, d), BF16),
        grid=(m // tm,),
        in_specs=[pl.BlockSpec((tm, d), lambda i: (i, 0)), pl.BlockSpec((1, d), lambda i: (0, 0))],
        out_specs=pl.BlockSpec((tm, d), lambda i: (i, 0)),
        compiler_params=_cparams(("parallel",)),
        name="rmsnorm_cast",
    )(x, gain.reshape(1, d))


def _norm_matmul_kernel(x_ref, g_ref, w_ref, o_ref, xn_ref):
    @pl.when(pl.program_id(1) == 0)
    def _():
        xn_ref[...] = _rms(x_ref[...], g_ref[...]).astype(BF16)

    o_ref[...] = jnp.dot(xn_ref[...], w_ref[...].astype(BF16), preferred_element_type=F32).astype(o_ref.dtype)


def _norm_matmul(x, gain, w, *, tm, tn, out_dtype=F32, name="norm_matmul"):
    m, d = x.shape
    n = w.shape[1]
    tm = min(tm, m)
    return pl.pallas_call(
        _norm_matmul_kernel,
        out_shape=jax.ShapeDtypeStruct((m, n), out_dtype),
        grid=(m // tm, pl.cdiv(n, tn)),
        in_specs=[pl.BlockSpec((tm, d), lambda i, j: (i, 0)),
                  pl.BlockSpec((1, d), lambda i, j: (0, 0)),
                  pl.BlockSpec((d, tn), lambda i, j: (0, j))],
        out_specs=pl.BlockSpec((tm, tn), lambda i, j: (i, j)),
        scratch_shapes=[pltpu.VMEM((tm, d), BF16)],
        compiler_params=_cparams(("parallel", "arbitrary")),
        name=name,
    )(x, gain.reshape(1, d), w)


def _matmul_kernel(*refs, nk, has_res):
    if has_res:
        a_ref, w_ref, r_ref, o_ref = refs[:4]
    else:
        a_ref, w_ref, o_ref = refs[:3]
        r_ref = None
    part = jnp.dot(a_ref[...], w_ref[...].astype(BF16), preferred_element_type=F32)
    if nk == 1:
        if has_res:
            part = part + r_ref[...]
        o_ref[...] = part.astype(o_ref.dtype)
        return
    acc_ref = refs[-1]
    k = pl.program_id(2)

    @pl.when(k == 0)
    def _():
        acc_ref[...] = part

    @pl.when(k > 0)
    def _():
        acc_ref[...] += part

    @pl.when(k == nk - 1)
    def _():
        r = acc_ref[...]
        if has_res:
            r = r + r_ref[...]
        o_ref[...] = r.astype(o_ref.dtype)


def _matmul(a, w, res=None, *, tm, tn, tk=None, out_dtype=F32, name="matmul"):
    m, kdim = a.shape
    n = w.shape[1]
    tm = min(tm, m)
    tk = kdim if tk is None else tk
    nk = kdim // tk
    in_specs = [pl.BlockSpec((tm, tk), lambda i, j, k: (i, k)),
                pl.BlockSpec((tk, tn), lambda i, j, k: (k, j))]
    args = [a, w]
    if res is not None:
        in_specs.append(pl.BlockSpec((tm, tn), lambda i, j, k: (i, j)))
        args.append(res)
    scratch = [pltpu.VMEM((tm, tn), F32)] if nk > 1 else []
    return pl.pallas_call(
        functools.partial(_matmul_kernel, nk=nk, has_res=res is not None),
        out_shape=jax.ShapeDtypeStruct((m, n), out_dtype),
        grid=(m // tm, pl.cdiv(n, tn), nk),
        in_specs=in_specs,
        out_specs=pl.BlockSpec((tm, tn), lambda i, j, k: (i, j)),
        scratch_shapes=scratch,
        compiler_params=_cparams(("parallel", "parallel", "arbitrary")),
        name=name,
    )(*args)


def _out_proj_kernel(mix_ref, mo_ref, w1_ref, w2_ref, r_ref, o_ref):
    o_ref[...] = (jnp.dot(mix_ref[...], w1_ref[...].astype(BF16), preferred_element_type=F32)
                  + jnp.dot(mo_ref[...], w2_ref[...].astype(BF16), preferred_element_type=F32)
                  + r_ref[...])


def _out_proj(mix, mo, w, res, tm=1024, tn=512):
    t = mix.shape[0]
    tm = min(tm, t)
    n = w.shape[1]
    assert Q_W % MEM_Q_W == 0
    return pl.pallas_call(
        _out_proj_kernel,
        out_shape=jax.ShapeDtypeStruct((t, n), F32),
        grid=(t // tm, n // tn),
        in_specs=[pl.BlockSpec((tm, Q_W), lambda i, j: (i, 0)),
                  pl.BlockSpec((tm, MEM_Q_W), lambda i, j: (i, 0)),
                  pl.BlockSpec((Q_W, tn), lambda i, j: (0, j)),
                  pl.BlockSpec((MEM_Q_W, tn), lambda i, j: (Q_W // MEM_Q_W, j)),
                  pl.BlockSpec((tm, tn), lambda i, j: (i, j))],
        out_specs=pl.BlockSpec((tm, tn), lambda i, j: (i, j)),
        compiler_params=_cparams(("parallel", "parallel")),
        name="out_proj",
    )(mix, mo, w, w, res)


def _swiglu_kernel(x_ref, gn_ref, wg_ref, wu_ref, o_ref, xn_ref):
    @pl.when(pl.program_id(1) == 0)
    def _():
        xn_ref[...] = _rms(x_ref[...], gn_ref[...]).astype(BF16)

    a = xn_ref[...]
    g = jnp.dot(a, wg_ref[...].astype(BF16), preferred_element_type=F32)
    u = jnp.dot(a, wu_ref[...].astype(BF16), preferred_element_type=F32)
    o_ref[...] = (g * _sigmoid(g) * u).astype(o_ref.dtype)


def _swiglu_up(x, gain, w_gu, *, tm=1024, tn=256):
    m, d = x.shape
    f = w_gu.shape[1] // 2
    tm = min(tm, m)
    nj = f // tn
    return pl.pallas_call(
        _swiglu_kernel,
        out_shape=jax.ShapeDtypeStruct((m, f), BF16),
        grid=(m // tm, nj),
        in_specs=[pl.BlockSpec((tm, d), lambda i, j: (i, 0)),
                  pl.BlockSpec((1, d), lambda i, j: (0, 0)),
                  pl.BlockSpec((d, tn), lambda i, j: (0, j)),
                  pl.BlockSpec((d, tn), lambda i, j: (0, j + nj))],
        out_specs=pl.BlockSpec((tm, tn), lambda i, j: (i, j)),
        scratch_shapes=[pltpu.VMEM((tm, d), BF16)],
        compiler_params=_cparams(("parallel", "arbitrary")),
        name="swiglu_up",
    )(x, gain.reshape(1, d), w_gu, w_gu)


def _nsa_prep_kernel(p_ref, gtab_ref, o_ref):
    for c in range((Q_W + 6 * KV_W) // HEAD_DIM):
        x = p_ref[:, c * HEAD_DIM:(c + 1) * HEAD_DIM]
        if c < N_MIX_HEADS:
            y = _rms(x, gtab_ref[0:1, :]) * (SCALE * LOG2E)
        elif 18 <= c < 21:
            y = _rms(x, gtab_ref[1:2, :])
        elif 24 <= c < 27:
            y = _rms(x, gtab_ref[2:3, :])
        else:
            y = x
        o_ref[:, c * HEAD_DIM:(c + 1) * HEAD_DIM] = y.astype(o_ref.dtype)


def _nsa_prep(p, q_gain, k_gain, tm=256):
    t = p.shape[0]
    tm = min(tm, t)
    w = Q_W + 6 * KV_W
    gtab = jnp.zeros((8, HEAD_DIM), F32).at[0].set(q_gain).at[1].set(k_gain[1]).at[2].set(k_gain[2])
    return pl.pallas_call(
        _nsa_prep_kernel,
        out_shape=jax.ShapeDtypeStruct((t, w), BF16),
        grid=(t // tm,),
        in_specs=[pl.BlockSpec((tm, w), lambda i: (i, 0)), pl.BlockSpec((8, HEAD_DIM), lambda i: (0, 0))],
        out_specs=pl.BlockSpec((tm, w), lambda i: (i, 0)),
        compiler_params=_cparams(("parallel",)),
        name="nsa_prep",
    )(p, gtab)


def _compress_kernel(p_ref, pos_ref, w_ref, g_ref, o_ref):
    ncp = o_ref.shape[2]
    first = jnp.zeros((ncp, HEAD_DIM), F32)
    second = jnp.zeros((ncp, HEAD_DIM), F32)
    for r in range(CMP_STRIDE):
        rows = p_ref[pl.ds(r, ncp, stride=CMP_STRIDE), :]
        lo = (rows + pos_ref[0, r:r + 1, :]).astype(BF16)
        hi = (rows + pos_ref[0, CMP_STRIDE + r:CMP_STRIDE + r + 1, :]).astype(BF16)
        first += jnp.dot(lo, w_ref[0, r * HEAD_DIM:(r + 1) * HEAD_DIM, :].astype(BF16),
                         preferred_element_type=F32)
        second += jnp.dot(hi, w_ref[0, (CMP_STRIDE + r) * HEAD_DIM:(CMP_STRIDE + r + 1) * HEAD_DIM, :].astype(BF16),
                          preferred_element_type=F32)
    y = first + jnp.concatenate([second[1:], jnp.zeros((1, HEAD_DIM), F32)], axis=0)
    is_k = pl.program_id(0) == 0
    o_ref[0, 0] = jnp.where(is_k, _rms(y, g_ref[...]), y).astype(o_ref.dtype)


def _compress(p, cmp_w, cmp_pos, k_gain0):
    t = p.shape[0]
    ncp = t // CMP_STRIDE
    col0 = Q_W // HEAD_DIM
    return pl.pallas_call(
        _compress_kernel,
        out_shape=jax.ShapeDtypeStruct((2, N_KV_GROUPS, ncp, HEAD_DIM), BF16),
        grid=(2, N_KV_GROUPS),
        in_specs=[pl.BlockSpec((t, HEAD_DIM), lambda s, gi: (0, col0 + N_KV_GROUPS * s + gi)),
                  pl.BlockSpec((1, CMP_BLOCK, HEAD_DIM), lambda s, gi: (s, 0, 0)),
                  pl.BlockSpec((1, CMP_BLOCK * HEAD_DIM, HEAD_DIM), lambda s, gi: (s, 0, 0)),
                  pl.BlockSpec((1, HEAD_DIM), lambda s, gi: (0, 0))],
        out_specs=pl.BlockSpec((1, 1, ncp, HEAD_DIM), lambda s, gi: (s, gi, 0, 0)),
        compiler_params=_cparams(("parallel", "parallel")),
        name="nsa_compress",
    )(p, cmp_pos, cmp_w, k_gain0.reshape(1, HEAD_DIM))


def _cmp_kernel(slopes_ref, q_ref, k_ref, v_ref, imap_ref, o_ref, score_ref, *, tq, ncp):
    g = pl.program_id(0)
    t0 = pl.program_id(1) * tq
    row = lax.broadcasted_iota(jnp.int32, (tq, ncp), 0) + t0
    col = lax.broadcasted_iota(jnp.int32, (tq, ncp), 1)
    c_end = col * CMP_STRIDE + (CMP_BLOCK - 1)
    neg = jnp.where(row >= c_end, 0.0, NEG)
    colb = (lax.broadcasted_iota(jnp.int32, (1, ncp), 1) * CMP_STRIDE + (CMP_BLOCK - 1) - t0).astype(F32)
    has_key = (lax.broadcasted_iota(jnp.int32, (tq, 1), 0) + t0 >= CMP_BLOCK - 1).astype(F32)
    k = k_ref[0, 0]
    v = v_ref[0, 0]
    psum = jnp.zeros((tq, ncp), F32)
    for p in range(HEADS_PER_GROUP):
        slope = slopes_ref[g * HEADS_PER_GROUP + p]
        q = q_ref[:, p * HEAD_DIM:(p + 1) * HEAD_DIM]
        s = _nt_dot(q, k) + slope * colb + neg
        e = jnp.exp2(s - jnp.max(s, axis=1, keepdims=True))
        pr = e * (has_key / jnp.sum(e, axis=1, keepdims=True))
        psum = psum + pr
        o_ref[:, p * HEAD_DIM:(p + 1) * HEAD_DIM] = jnp.dot(
            pr.astype(BF16), v, preferred_element_type=F32).astype(o_ref.dtype)

    imap = imap_ref[...]
    hi = psum.astype(BF16)
    r1 = psum - hi.astype(F32)
    mid = r1.astype(BF16)
    lo = (r1 - mid.astype(F32)).astype(BF16)
    imp = (jnp.dot(hi, imap, preferred_element_type=F32)
           + jnp.dot(mid, imap, preferred_element_type=F32)
           + jnp.dot(lo, imap, preferred_element_type=F32))
    nb = imp.shape[1]
    t = lax.broadcasted_iota(jnp.int32, (tq, nb), 0) + t0
    blk = lax.broadcasted_iota(jnp.int32, (tq, nb), 1)
    score_ref[0] = jnp.where(blk == t // SLC_BLOCK, 2 * BIG,
                             jnp.where(blk == 0, BIG,
                                       jnp.where(blk * SLC_BLOCK <= t, imp, -BIG)))


def _cmp_attn(slopes, qkv, kvc, imap, tq=256):
    t = qkv.shape[0]
    tq = min(tq, t)
    ncp = kvc.shape[2]
    nb = t // SLC_BLOCK
    gw = HEADS_PER_GROUP * HEAD_DIM
    return pl.pallas_call(
        functools.partial(_cmp_kernel, tq=tq, ncp=ncp),
        out_shape=(jax.ShapeDtypeStruct((t, Q_W), F32),
                   jax.ShapeDtypeStruct((N_KV_GROUPS, t, nb), F32)),
        grid_spec=pltpu.PrefetchScalarGridSpec(
            num_scalar_prefetch=1,
            grid=(N_KV_GROUPS, t // tq),
            in_specs=[pl.BlockSpec((tq, gw), lambda g, i, s: (i, g)),
                      pl.BlockSpec((1, 1, ncp, HEAD_DIM), lambda g, i, s: (0, g, 0, 0)),
                      pl.BlockSpec((1, 1, ncp, HEAD_DIM), lambda g, i, s: (1, g, 0, 0)),
                      pl.BlockSpec((ncp, nb), lambda g, i, s: (0, 0))],
            out_specs=(pl.BlockSpec((tq, gw), lambda g, i, s: (i, g)),
                       pl.BlockSpec((1, tq, nb), lambda g, i, s: (g, i, 0)))),
        compiler_params=_cparams(("parallel", "parallel")),
        name="nsa_cmp",
    )(slopes, qkv, kvc, kvc, imap)


def _topk_kernel(score_ref, sel_ref, act_ref, *, tm, sub):
    score = score_ref[0]
    nb = score.shape[1]
    blk_f = lax.broadcasted_iota(jnp.int32, (tm, nb), 1).astype(F32)

    def pick_one(_, carry):
        work, sel = carry
        m = jnp.max(work, axis=1, keepdims=True)
        first = jnp.min(jnp.where(work == m, blk_f, float(nb)), axis=1, keepdims=True)
        pick = blk_f == first
        return jnp.where(pick, -3e38, work), jnp.where(pick, 1.0, sel)

    forced = score >= 0.5 * BIG
    start = (jnp.where(forced, -3e38, score), jnp.where(forced, 1.0, 0.0))
    _, sel = lax.fori_loop(0, max(min(SLC_TOPK, nb) - 2, 0), pick_one, start)
    sel_ref[0] = sel.astype(sel_ref.dtype)
    for a in range(tm // sub):
        act_ref[0, a] = jnp.broadcast_to(jnp.max(sel[a * sub:(a + 1) * sub], axis=0, keepdims=True),
                                         act_ref.shape[2:])


def _topk_select(score, sub, tm=1024):
    g, t, nb = score.shape
    tm = min(tm, t)
    sub = min(sub, tm)
    return pl.pallas_call(
        functools.partial(_topk_kernel, tm=tm, sub=sub),
        out_shape=(jax.ShapeDtypeStruct((g, t, nb), BF16),
                   jax.ShapeDtypeStruct((g, t // sub, 8, nb), F32)),
        grid=(g, t // tm),
        in_specs=[pl.BlockSpec((1, tm, nb), lambda gi, i: (gi, i, 0))],
        out_specs=(pl.BlockSpec((1, tm, nb), lambda gi, i: (gi, i, 0)),
                   pl.BlockSpec((1, tm // sub, 8, nb), lambda gi, i: (gi, i, 0, 0))),
        compiler_params=_cparams(("parallel", "parallel")),
        name="nsa_topk",
    )(score)


def _slc_kernel(slopes_ref, flags_ref, q_ref, k_ref, v_ref, sel_ref, o_ref, m_ref, l_ref, acc_ref, *, tq, tk):
    g = pl.program_id(0)
    i = pl.program_id(1)
    t0 = i * tq
    flag_base = (g * pl.num_programs(1) + i) * (k_ref.shape[0] // tk)
    nb = sel_ref.shape[2]
    m_ref[...] = jnp.full(m_ref.shape, NEG, F32)
    l_ref[...] = jnp.zeros(l_ref.shape, F32)
    acc_ref[...] = jnp.zeros(acc_ref.shape, F32)
    sel = sel_ref[0]
    bpt = tk // SLC_BLOCK

    def body(j, carry):
        @pl.when(flags_ref[flag_base + j] > 0)
        def _():
            tile(j, False)
        return carry

    def tile(j, diagonal):
        k0 = pl.multiple_of(j * tk, tk)
        k = k_ref[pl.ds(k0, tk), :]
        v = v_ref[pl.ds(k0, tk), :]
        eb = lax.broadcasted_iota(jnp.int32, (nb, tk), 0)
        ec = lax.broadcasted_iota(jnp.int32, (nb, tk), 1)
        expand = jnp.where(eb == ec // SLC_BLOCK + j * bpt, 1.0, 0.0).astype(BF16)
        selx = jnp.dot(sel, expand, preferred_element_type=F32)
        if diagonal:
            later = lax.broadcasted_iota(jnp.int32, (tq, tk), 0) < lax.broadcasted_iota(jnp.int32, (tq, tk), 1)
            selx = jnp.where(later, 0.0, selx)
        neg = jnp.where(selx > 0.5, 0.0, NEG)
        colb = (lax.broadcasted_iota(jnp.int32, (1, tk), 1) + (k0 - t0)).astype(F32)
        for p in range(HEADS_PER_GROUP):
            slope = slopes_ref[g * HEADS_PER_GROUP + p]
            q = q_ref[:, p * HEAD_DIM:(p + 1) * HEAD_DIM]
            s = _nt_dot(q, k) + slope * colb + neg
            m_old = m_ref[p]
            m_new = jnp.maximum(m_old, jnp.max(s, axis=1, keepdims=True))
            alpha = jnp.exp2(m_old - m_new)
            e = jnp.exp2(s - jnp.tile(m_new, (1, tk // LANES)))
            l_ref[p] = alpha * l_ref[p] + jnp.sum(e, axis=1, keepdims=True)
            acc_ref[p] = alpha * acc_ref[p] + jnp.dot(e.astype(BF16), v, preferred_element_type=F32)
            m_ref[p] = m_new

    lax.fori_loop(0, i, body, 0)
    tile(i, True)
    for p in range(HEADS_PER_GROUP):
        o_ref[:, p * HEAD_DIM:(p + 1) * HEAD_DIM] = (acc_ref[p] / l_ref[p]).astype(o_ref.dtype)


def _slc_attn(slopes, qkv, sel, blk_act, tq=256, tk=256):
    t = qkv.shape[0]
    tq = min(tq, t)
    tk = min(tk, t)
    nb = sel.shape[2]
    assert blk_act.shape[1] == t // tq and tq == tk
    gw = HEADS_PER_GROUP * HEAD_DIM
    ks_col = (Q_W + 2 * KV_W) // HEAD_DIM
    vs_col = (Q_W + 3 * KV_W) // HEAD_DIM
    flags = jnp.max(blk_act.reshape(N_KV_GROUPS, t // tq, t // tk, tk // SLC_BLOCK), axis=-1)
    flags = (flags > 0.5).astype(jnp.int32).reshape(-1)
    return pl.pallas_call(
        functools.partial(_slc_kernel, tq=tq, tk=tk),
        out_shape=jax.ShapeDtypeStruct((t, Q_W), F32),
        grid_spec=pltpu.PrefetchScalarGridSpec(
            num_scalar_prefetch=2,
            grid=(N_KV_GROUPS, t // tq),
            in_specs=[pl.BlockSpec((tq, gw), lambda g, i, s, f: (i, g)),
                      pl.BlockSpec((t, HEAD_DIM), lambda g, i, s, f: (0, ks_col + g)),
                      pl.BlockSpec((t, HEAD_DIM), lambda g, i, s, f: (0, vs_col + g)),
                      pl.BlockSpec((1, tq, nb), lambda g, i, s, f: (g, i, 0))],
            out_specs=pl.BlockSpec((tq, gw), lambda g, i, s, f: (i, g)),
            scratch_shapes=[pltpu.VMEM((HEADS_PER_GROUP, tq, LANES), F32),
                            pltpu.VMEM((HEADS_PER_GROUP, tq, LANES), F32),
                            pltpu.VMEM((HEADS_PER_GROUP, tq, HEAD_DIM), F32)]),
        compiler_params=_cparams(("parallel", "parallel")),
        name="nsa_selected",
    )(slopes, flags, qkv, qkv, qkv, sel)


def _win_kernel(slopes_ref, q_ref, k0_ref, k1_ref, k2_ref, v0_ref, v1_ref, v2_ref, o_ref, *, tq):
    g = pl.program_id(0)
    i = pl.program_id(1)
    nprev = WINDOW // tq
    kw = (nprev + 1) * tq
    k = jnp.concatenate([k0_ref[...], k1_ref[...], k2_ref[...]], axis=0)
    v = jnp.concatenate([v0_ref[...], v1_ref[...], v2_ref[...]], axis=0)
    r = lax.broadcasted_iota(jnp.int32, (tq, kw), 0)
    c = lax.broadcasted_iota(jnp.int32, (tq, kw), 1)
    rel = WINDOW + r - c
    first_valid = (nprev - i) * tq
    inside = jnp.where(rel >= 0, jnp.where(rel < WINDOW, 1.0, 0.0), 0.0)
    neg = jnp.where(jnp.where(c >= first_valid, inside, 0.0) > 0.5, 0.0, NEG)
    colb = (lax.broadcasted_iota(jnp.int32, (1, kw), 1) - WINDOW).astype(F32)
    for p in range(HEADS_PER_GROUP):
        slope = slopes_ref[g * HEADS_PER_GROUP + p]
        q = q_ref[:, p * HEAD_DIM:(p + 1) * HEAD_DIM]
        s = _nt_dot(q, k) + slope * colb + neg
        e = jnp.exp2(s - jnp.max(s, axis=1, keepdims=True))
        o = jnp.dot(e.astype(BF16), v, preferred_element_type=F32) / jnp.sum(e, axis=1, keepdims=True)
        o_ref[:, p * HEAD_DIM:(p + 1) * HEAD_DIM] = o.astype(o_ref.dtype)


def _win_attn(slopes, qkv, tq=256):
    t = qkv.shape[0]
    assert WINDOW % tq == 0 and WINDOW // tq == 2
    gw = HEADS_PER_GROUP * HEAD_DIM
    kw_col = (Q_W + 4 * KV_W) // HEAD_DIM
    vw_col = (Q_W + 5 * KV_W) // HEAD_DIM

    def kv_spec(col, back):
        return pl.BlockSpec((tq, HEAD_DIM), lambda g, i, s: (jnp.maximum(i - back, 0), col + g))

    return pl.pallas_call(
        functools.partial(_win_kernel, tq=tq),
        out_shape=jax.ShapeDtypeStruct((t, Q_W), F32),
        grid_spec=pltpu.PrefetchScalarGridSpec(
            num_scalar_prefetch=1,
            grid=(N_KV_GROUPS, t // tq),
            in_specs=[pl.BlockSpec((tq, gw), lambda g, i, s: (i, g)),
                      kv_spec(kw_col, 2), kv_spec(kw_col, 1), kv_spec(kw_col, 0),
                      kv_spec(vw_col, 2), kv_spec(vw_col, 1), kv_spec(vw_col, 0)],
            out_specs=pl.BlockSpec((tq, gw), lambda g, i, s: (i, g))),
        compiler_params=_cparams(("parallel", "parallel")),
        name="nsa_window",
    )(slopes, qkv, qkv, qkv, qkv, qkv, qkv, qkv)


def _gate_sum_kernel(oc_ref, os_ref, ow_ref, g_ref, o_ref):
    sg = _sigmoid(g_ref[...])
    for h in range(N_MIX_HEADS):
        sl = slice(h * HEAD_DIM, (h + 1) * HEAD_DIM)
        o = (oc_ref[:, sl] * sg[:, 3 * h:3 * h + 1]
             + os_ref[:, sl] * sg[:, 3 * h + 1:3 * h + 2]
             + ow_ref[:, sl] * sg[:, 3 * h + 2:3 * h + 3])
        o_ref[:, sl] = o.astype(o_ref.dtype)


def _gate_sum(o_cmp, o_slc, o_win, gates, tm=512):
    t = o_cmp.shape[0]
    tm = min(tm, t)
    big = pl.BlockSpec((tm, Q_W), lambda i: (i, 0))
    return pl.pallas_call(
        _gate_sum_kernel,
        out_shape=jax.ShapeDtypeStruct((t, Q_W), BF16),
        grid=(t // tm,),
        in_specs=[big, big, big, pl.BlockSpec((tm, GATE_W), lambda i: (i, 0))],
        out_specs=big,
        compiler_params=_cparams(("parallel",)),
        name="nsa_gate_sum",
    )(o_cmp, o_slc, o_win, gates)


def _mem_attn_kernel(q_ref, k_ref, v_ref, g_ref, o_ref):
    q = (_rms(q_ref[...], g_ref[0:1, :]) * SCALE).astype(BF16)
    k = _rms(k_ref[...], g_ref[1:2, :]).astype(BF16)
    s = _nt_dot(q, k)
    e = jnp.exp(s - jnp.max(s, axis=1, keepdims=True))
    p = e * (1.0 / jnp.sum(e, axis=1, keepdims=True))
    o_ref[...] = jnp.dot(p.astype(BF16), v_ref[...].astype(BF16),
                         preferred_element_type=F32).astype(o_ref.dtype)


def _mem_attn(qsrc, q_col, memkv, qk_gain, tq=512):
    t = qsrc.shape[0]
    tq = min(tq, t)
    nm = memkv.shape[0]
    gtab = jnp.zeros((8, HEAD_DIM), F32).at[0:2].set(qk_gain)
    return pl.pallas_call(
        _mem_attn_kernel,
        out_shape=jax.ShapeDtypeStruct((t, MEM_Q_W), BF16),
        grid=(N_MEM_HEADS, t // tq),
        in_specs=[pl.BlockSpec((tq, HEAD_DIM), lambda h, i: (i, q_col + h)),
                  pl.BlockSpec((nm, HEAD_DIM), lambda h, i: (0, h)),
                  pl.BlockSpec((nm, HEAD_DIM), lambda h, i: (0, N_MEM_HEADS + h)),
                  pl.BlockSpec((8, HEAD_DIM), lambda h, i: (0, 0))],
        out_specs=pl.BlockSpec((tq, HEAD_DIM), lambda h, i: (i, h)),
        compiler_params=_cparams(("parallel", "parallel")),
        name="mem_attn",
    )(qsrc, memkv, memkv, gtab)


def _stick_kernel(q_ref, k_ref, v_ref, o_ref, *, tq, hp):
    i = pl.program_id(1)
    r = lax.broadcasted_iota(jnp.int32, (tq, tq), 0)
    c = lax.broadcasted_iota(jnp.int32, (tq, tq), 1)
    later = jnp.where(r > c, 1.0, 0.0).astype(BF16)
    qs = [(q_ref[:, h * HEAD_DIM:(h + 1) * HEAD_DIM] * (SCALE * LOG2E)).astype(BF16) for h in range(hp)]

    def tile(h, k0, tail, acc, diagonal):
        k = k_ref[pl.ds(k0, tq), h * HEAD_DIM:(h + 1) * HEAD_DIM]
        v = v_ref[pl.ds(k0, tq), h * HEAD_DIM:(h + 1) * HEAD_DIM]
        z = _nt_dot(qs[h], k)
        sp = jnp.maximum(z, 0.0) + jnp.log2(1.0 + jnp.exp2(-jnp.abs(z)))
        log_keep = jnp.where(r > c, -sp, 0.0) if diagonal else -sp
        hi = log_keep.astype(BF16)
        lo = (log_keep - hi.astype(F32)).astype(BF16)
        after = (jnp.dot(hi, later, preferred_element_type=F32)
                 + jnp.dot(lo, later, preferred_element_type=F32) + tail)
        a = jnp.exp2(z - sp + after)
        if diagonal:
            a = jnp.where(r > c, a, 0.0)
        acc = acc + jnp.dot(a.astype(BF16), v, preferred_element_type=F32)
        return tail + jnp.sum(log_keep, axis=1, keepdims=True), acc

    k_diag = pl.multiple_of(i * tq, tq)
    state = [tile(h, k_diag, jnp.zeros((tq, 1), F32), jnp.zeros((tq, HEAD_DIM), F32), True) for h in range(hp)]

    def live(st):
        return jnp.logical_and(st[0] <= i, st[1] > 0)

    def body(st):
        k0 = pl.multiple_of((i - st[0]) * tq, tq)
        new = [tile(h, k0, st[2 + 2 * h], st[3 + 2 * h], False) for h in range(hp)]
        top = functools.reduce(jnp.maximum, [jnp.max(tl) for tl, _ in new])
        return (st[0] + 1, (top > STICK_DEAD).astype(jnp.int32)) + tuple(x for pair in new for x in pair)

    st = lax.while_loop(live, body, (jnp.int32(1), jnp.int32(1)) + tuple(x for pair in state for x in pair))
    for h in range(hp):
        o_ref[:, h * HEAD_DIM:(h + 1) * HEAD_DIM] = st[3 + 2 * h].astype(o_ref.dtype)


def _stick_attn(pb, kvb, tq=256, hp=2):
    t = pb.shape[0]
    tq = min(tq, t)
    hw = hp * HEAD_DIM
    return pl.pallas_call(
        functools.partial(_stick_kernel, tq=tq, hp=hp),
        out_shape=jax.ShapeDtypeStruct((t, Q_W), BF16),
        grid=(N_MIX_HEADS // hp, t // tq),
        in_specs=[pl.BlockSpec((tq, hw), lambda h, i: (i, h)),
                  pl.BlockSpec((t, hw), lambda h, i: (0, h)),
                  pl.BlockSpec((t, hw), lambda h, i: (0, N_MIX_HEADS // hp + h))],
        out_specs=pl.BlockSpec((tq, hw), lambda h, i: (i, h)),
        compiler_params=_cparams(("parallel", "parallel")),
        name="stick_breaking",
    )(pb, kvb, kvb)


def _router_kernel(x_ref, g_ref, w_ref, b_ref, c_ref):
    h = _rms(x_ref[...], g_ref[...])
    logits = jnp.dot(h, w_ref[...], preferred_element_type=F32, precision=lax.Precision.HIGHEST)
    lane = lax.broadcasted_iota(jnp.int32, logits.shape, 1).astype(F32)
    biased = jnp.where(lane < N_EXPERTS, logits + b_ref[...], -3e38)
    m1 = jnp.max(biased, axis=1, keepdims=True)
    i1 = jnp.min(jnp.where(biased == m1, lane, float(LANES)), axis=1, keepdims=True)
    rest = jnp.where(lane == i1, -3e38, biased)
    m2 = jnp.max(rest, axis=1, keepdims=True)
    i2 = jnp.min(jnp.where(rest == m2, lane, float(LANES)), axis=1, keepdims=True)
    l1 = jnp.sum(jnp.where(lane == i1, logits, 0.0), axis=1, keepdims=True)
    l2 = jnp.sum(jnp.where(lane == i2, logits, 0.0), axis=1, keepdims=True)
    mx = jnp.maximum(l1, l2)
    e1 = jnp.exp(l1 - mx)
    e2 = jnp.exp(l2 - mx)
    inv = 1.0 / (e1 + e2)
    c_ref[...] = (jnp.where(lane == 0.0, i1, 0.0) + jnp.where(lane == 1.0, i2, 0.0)
                  + jnp.where(lane == 2.0, e1 * inv, 0.0) + jnp.where(lane == 3.0, e2 * inv, 0.0))


def _router(x, gain, router_w, router_b, tm=512):
    t, d = x.shape
    tm = min(tm, t)
    w = jnp.zeros((d, LANES), F32).at[:, :N_EXPERTS].set(router_w)
    b = jnp.zeros((1, LANES), F32).at[0, :N_EXPERTS].set(router_b)
    return pl.pallas_call(
        _router_kernel,
        out_shape=jax.ShapeDtypeStruct((t, LANES), F32),
        grid=(t // tm,),
        in_specs=[pl.BlockSpec((tm, d), lambda i: (i, 0)), pl.BlockSpec((1, d), lambda i: (0, 0)),
                  pl.BlockSpec((d, LANES), lambda i: (0, 0)), pl.BlockSpec((1, LANES), lambda i: (0, 0))],
        out_specs=pl.BlockSpec((tm, LANES), lambda i: (i, 0)),
        compiler_params=_cparams(("parallel",)),
        name="moe_router",
    )(x, gain.reshape(1, d), w, b)


def _route_plan(rt, sub, per_group):
    t = rt.shape[0]
    i32 = jnp.int32
    e_flat = rt[:, 0:2].astype(i32).reshape(-1)
    onehot = (e_flat[:, None] == jnp.arange(N_EXPERTS, dtype=i32)[None, :]).astype(i32)
    csum = jnp.cumsum(onehot, axis=0)
    rank = jnp.sum(csum * onehot, axis=1) - 1
    tiles_e = (csum[-1] + sub - 1) // sub
    tile_end = jnp.cumsum(tiles_e)
    tile_start = tile_end - tiles_e
    dest = tile_start[e_flat] * sub + rank
    groups_e = (tiles_e + per_group - 1) // per_group
    group_end = jnp.cumsum(groups_e)
    n_groups = group_end[-1]
    max_groups = (2 * t // sub + N_EXPERTS) // per_group + N_EXPERTS
    gidx = jnp.arange(max_groups, dtype=i32)
    gcl = jnp.minimum(gidx, n_groups - 1)
    g_expert = jnp.sum((gcl[:, None] >= group_end[None, :]).astype(i32), axis=1)
    k_in_e = gcl - (group_end - groups_e)[g_expert]
    g_start = (tile_start[g_expert] + k_in_e * per_group) * sub
    g_tiles = jnp.where(gidx < n_groups, jnp.clip(tiles_e[g_expert] - k_in_e * per_group, 0, per_group), 0)
    return (dest.astype(i32), tile_end[-1].reshape(1).astype(i32),
            g_expert.astype(i32), g_start.astype(i32), g_tiles.astype(i32))


def _dispatch_kernel(dest_ref, nt_ref, x_hbm, g_ref, o_ref, tok_ref, xbuf, sems, *, tm):
    n = pl.program_id(0)
    n_tiles = nt_ref[0]

    def gather(tile, slot):
        base = tile * tm

        def issue(r, c):
            pltpu.make_async_copy(x_hbm.at[pl.ds(tok_ref[base + r], 1), :],
                                  xbuf.at[slot, pl.ds(r, 1), :], sems.at[slot]).start()
            return c

        lax.fori_loop(0, tm, issue, 0, unroll=8)

    @pl.when(n == 0)
    def _():
        def clear(r, c):
            tok_ref[r] = 0
            return c

        lax.fori_loop(0, tok_ref.shape[0], clear, 0, unroll=8)

        def fill(q, c):
            rows = [dest_ref[q * 8 + u] for u in range(8)]
            for u in range(8):
                tok_ref[rows[u]] = q * 4 + u // 2
            return c

        lax.fori_loop(0, dest_ref.shape[0] // 8, fill, 0)
        gather(0, 0)

    slot = n % 2

    @pl.when(n + 1 < n_tiles)
    def _():
        gather(n + 1, 1 - slot)

    @pl.when(n < n_tiles)
    def _():
        pltpu.make_async_copy(x_hbm.at[pl.ds(0, tm), :], xbuf.at[slot], sems.at[slot]).wait()
        o_ref[...] = _rms(xbuf[slot], g_ref[...]).astype(o_ref.dtype)

    @pl.when(n >= n_tiles)
    def _():
        o_ref[...] = jnp.zeros(o_ref.shape, o_ref.dtype)


def _dispatch(x, gain, dest, n_tiles, tm, nt):
    t, d = x.shape
    return pl.pallas_call(
        functools.partial(_dispatch_kernel, tm=tm),
        out_shape=jax.ShapeDtypeStruct((nt * tm, d), BF16),
        grid_spec=pltpu.PrefetchScalarGridSpec(
            num_scalar_prefetch=2,
            grid=(nt,),
            in_specs=[pl.BlockSpec(memory_space=pl.ANY), pl.BlockSpec((1, d), lambda n, dst, ntl: (0, 0))],
            out_specs=pl.BlockSpec((tm, d), lambda n, dst, ntl: (n, 0)),
            scratch_shapes=[pltpu.SMEM((nt * tm,), jnp.int32), pltpu.VMEM((2, tm, d), F32),
                            pltpu.SemaphoreType.DMA((2,))]),
        compiler_params=_cparams(("arbitrary",)),
        name="moe_dispatch",
    )(dest, n_tiles, x, gain.reshape(1, d))


def _experts_kernel(ge_ref, gs_ref, gn_ref, nt_ref, xg_hbm, wg_ref, wu_ref, wd_ref, y_hbm,
                    xbuf, yacc, wgc, wuc, wdc, sem_in, sem_out, *, sub):
    s = pl.program_id(0)
    j = pl.program_id(1)
    n_sub = gn_ref[s]
    start = pl.multiple_of(gs_ref[s], sub)

    @pl.when(jnp.logical_and(s == 0, j == 0))
    def _():
        yacc[0:sub, :] = jnp.zeros((sub, yacc.shape[1]), F32)
        first, last = nt_ref[0], y_hbm.shape[0] // sub

        def put(r, c):
            pltpu.make_async_copy(yacc.at[pl.ds(0, sub), :],
                                  y_hbm.at[pl.ds(pl.multiple_of(r * sub, sub), sub), :], sem_out).start()
            return c

        def done(r, c):
            pltpu.make_async_copy(yacc.at[pl.ds(0, sub), :], y_hbm.at[pl.ds(0, sub), :], sem_out).wait()
            return c

        lax.fori_loop(first, last, put, 0)
        lax.fori_loop(first, last, done, 0)

    @pl.when(n_sub > 0)
    def _():
        @pl.when(j == 0)
        def _():
            def fetch(r, c):
                r0 = pl.multiple_of(r * sub, sub)
                pltpu.make_async_copy(xg_hbm.at[pl.ds(start + r0, sub), :], xbuf.at[pl.ds(r0, sub), :],
                                      sem_in).start()
                return c

            def fetched(r, c):
                pltpu.make_async_copy(xg_hbm.at[pl.ds(0, sub), :], xbuf.at[pl.ds(0, sub), :], sem_in).wait()
                return c

            lax.fori_loop(0, n_sub, fetch, 0)
            yacc[...] = jnp.zeros(yacc.shape, F32)
            lax.fori_loop(0, n_sub, fetched, 0)

        wgc[...] = wg_ref[0].astype(BF16)
        wuc[...] = wu_ref[0].astype(BF16)
        wdc[...] = wd_ref[0].astype(BF16)

        def rows_at(r0, m):
            x = xbuf[pl.ds(r0, m), :]
            g = jnp.dot(x, wgc[...], preferred_element_type=F32)
            u = jnp.dot(x, wuc[...], preferred_element_type=F32)
            a = (g * _sigmoid(g) * u).astype(BF16)
            yacc[pl.ds(r0, m), :] += jnp.dot(a, wdc[...], preferred_element_type=F32)

        def tile_pair(r, c):
            rows_at(pl.multiple_of(r * 2 * sub, 2 * sub), 2 * sub)
            return c

        lax.fori_loop(0, n_sub // 2, tile_pair, 0)

        @pl.when(n_sub % 2 == 1)
        def _():
            rows_at(pl.multiple_of((n_sub - 1) * sub, sub), sub)

        @pl.when(j == pl.num_programs(1) - 1)
        def _():
            def put(r, c):
                r0 = pl.multiple_of(r * sub, sub)
                pltpu.make_async_copy(yacc.at[pl.ds(r0, sub), :], y_hbm.at[pl.ds(start + r0, sub), :],
                                      sem_out).start()
                return c

            def done(r, c):
                pltpu.make_async_copy(yacc.at[pl.ds(0, sub), :], y_hbm.at[pl.ds(0, sub), :], sem_out).wait()
                return c

            lax.fori_loop(0, n_sub, put, 0)
            lax.fori_loop(0, n_sub, done, 0)


def _moe_experts(xg, w_gu, w_down, g_expert, g_start, g_tiles, n_tiles, *, sub, per_group, tf=256):
    p, d = xg.shape
    f = w_down.shape[1]
    nj = f // tf
    rows = sub * per_group
    n_groups = g_expert.shape[0]

    def jcl(s, j, gn):
        return jnp.where(gn[s] > 0, j, nj - 1)

    return pl.pallas_call(
        functools.partial(_experts_kernel, sub=sub),
        out_shape=jax.ShapeDtypeStruct((p, d), F32),
        grid_spec=pltpu.PrefetchScalarGridSpec(
            num_scalar_prefetch=4,
            grid=(n_groups, nj),
            in_specs=[pl.BlockSpec(memory_space=pl.ANY),
                      pl.BlockSpec((1, d, tf), lambda s, j, ge, gs, gn, nt: (ge[s], 0, jcl(s, j, gn))),
                      pl.BlockSpec((1, d, tf), lambda s, j, ge, gs, gn, nt: (ge[s], 0, jcl(s, j, gn) + nj)),
                      pl.BlockSpec((1, tf, d), lambda s, j, ge, gs, gn, nt: (ge[s], jcl(s, j, gn), 0))],
            out_specs=pl.BlockSpec(memory_space=pl.ANY),
            scratch_shapes=[pltpu.VMEM((rows, d), BF16), pltpu.VMEM((rows, d), F32),
                            pltpu.VMEM((d, tf), BF16), pltpu.VMEM((d, tf), BF16), pltpu.VMEM((tf, d), BF16),
                            pltpu.SemaphoreType.DMA(()), pltpu.SemaphoreType.DMA(())]),
        compiler_params=pltpu.CompilerParams(dimension_semantics=("arbitrary", "arbitrary"),
                                             vmem_limit_bytes=MOE_VMEM_LIMIT),
        name="moe_experts",
    )(g_expert, g_start, g_tiles, n_tiles, xg, w_gu, w_gu, w_down)


def _moe_combine_kernel(dest_ref, x_ref, rt_ref, y_hbm, o_ref, ybuf, sems, *, tm):
    i = pl.program_id(0)

    def gather(tile, slot):
        base = tile * tm

        def issue(r, c):
            for k in range(2):
                pltpu.make_async_copy(y_hbm.at[pl.ds(dest_ref[2 * (base + r) + k], 1), :],
                                      ybuf.at[slot, pl.ds(k * tm + r, 1), :], sems.at[slot]).start()
            return c

        lax.fori_loop(0, tm, issue, 0, unroll=4)

    @pl.when(i == 0)
    def _():
        gather(0, 0)

    slot = i % 2

    @pl.when(i + 1 < pl.num_programs(0))
    def _():
        gather(i + 1, 1 - slot)

    pltpu.make_async_copy(y_hbm.at[pl.ds(0, 2 * tm), :], ybuf.at[slot], sems.at[slot]).wait()
    w = rt_ref[...]
    o_ref[...] = (x_ref[...] + w[:, 2:3] * ybuf[slot, 0:tm, :] + w[:, 3:4] * ybuf[slot, tm:2 * tm, :])


def _moe_combine(x, rt, y, dest, tm=256):
    t, d = x.shape
    tm = min(tm, t)
    return pl.pallas_call(
        functools.partial(_moe_combine_kernel, tm=tm),
        out_shape=jax.ShapeDtypeStruct((t, d), F32),
        grid_spec=pltpu.PrefetchScalarGridSpec(
            num_scalar_prefetch=1,
            grid=(t // tm,),
            in_specs=[pl.BlockSpec((tm, d), lambda i, dst: (i, 0)),
                      pl.BlockSpec((tm, LANES), lambda i, dst: (i, 0)),
                      pl.BlockSpec(memory_space=pl.ANY)],
            out_specs=pl.BlockSpec((tm, d), lambda i, dst: (i, 0)),
            scratch_shapes=[pltpu.VMEM((2, 2 * tm, d), F32), pltpu.SemaphoreType.DMA((2,))]),
        compiler_params=_cparams(("arbitrary",)),
        name="moe_combine",
    )(dest, x, rt, y)


def _moe_block(xs, gain, router_w, router_b, w_gu, w_down):
    t = xs.shape[0]
    sub = min(MOE_SUB, t)
    nt = 2 * t // sub + N_EXPERTS
    rt = _router(xs, gain, router_w, router_b)
    dest, n_tiles, g_expert, g_start, g_tiles = _route_plan(rt, sub, MOE_TILES_PER_GROUP)
    xg = _dispatch(xs, gain, dest, n_tiles, sub, nt)
    y = _moe_experts(xg, w_gu, w_down, g_expert, g_start, g_tiles, n_tiles,
                     sub=sub, per_group=MOE_TILES_PER_GROUP)
    return _moe_combine(xs, rt, y, dest)


def _alibi_slopes():
    slopes = np.array([2.0 ** (-8.0 * (i + 1) / N_MIX_HEADS) for i in range(N_MIX_HEADS)], dtype=np.float32)
    return jnp.asarray((slopes.astype(np.float64) * LOG2E).astype(np.float32))


def _importance_map(ncp, nb):
    ratio = SLC_BLOCK // CMP_STRIDE
    c = np.arange(ncp)[:, None]
    b = np.arange(nb)[None, :]
    m = (c >= ratio * b - 1) & (c <= ratio * b + ratio - 1) & (c < ncp - 1)
    return jnp.asarray(m.astype(np.float32), dtype=BF16)


def _nsa_mix(p, slopes, q_gain, k_gain, cmp_w, cmp_pos, slc_tq=256):
    t = p.shape[0]
    qkv = _nsa_prep(p, q_gain, k_gain)
    kvc = _compress(p, cmp_w, cmp_pos, k_gain[0])
    imap = _importance_map(t // CMP_STRIDE, t // SLC_BLOCK)
    o_cmp, score = _cmp_attn(slopes, qkv, kvc, imap)
    sel, blk_act = _topk_select(score, sub=slc_tq)
    o_slc = _slc_attn(slopes, qkv, sel, blk_act[:, :, 0, :], tq=slc_tq)
    o_win = _win_attn(slopes, qkv)
    gates = p[:, Q_W + 6 * KV_W:Q_W + 6 * KV_W + GATE_W]
    return _gate_sum(o_cmp, o_slc, o_win, gates)


def kernel(x, mem, attn_norm, mem_norm, ffn_norm, w_in_a, cmp_w, cmp_pos, nsa_q_norm, nsa_k_norm, w_in_b,
           kv_norm_b, w_kv_b, w_mem_kv, mem_qk_norm, w_out, ffn_w_gu, ffn_w_down, router_w, router_b,
           moe_w_gu, moe_w_down):
    b, t, d = x.shape
    assert b == 1
    xs = x[0]
    mems = mem[0]
    slopes = _alibi_slopes()

    p = _norm_matmul(xs, attn_norm[0], w_in_a[0], tm=1024, tn=896, name="in_proj_a")
    mix = _nsa_mix(p, slopes, nsa_q_norm[0], nsa_k_norm[0], cmp_w[0], cmp_pos[0])
    memkv = _norm_matmul(mems, mem_norm[0], w_mem_kv[0], tm=256, tn=512, name="mem_kv")
    qm = p[:, Q_W + 6 * KV_W + GATE_W:]
    mo = _mem_attn(qm, 0, memkv, mem_qk_norm[0])
    xs = _out_proj(mix, mo, w_out[0], xs)
    act = _swiglu_up(xs, ffn_norm[0], ffn_w_gu[0])
    xs = _matmul(act, ffn_w_down[0], xs, tm=1024, tn=1024, tk=1024, name="ffn_down")

    kvb = _norm_matmul(xs, kv_norm_b, w_kv_b, tm=1024, tn=512, out_dtype=BF16, name="kv_proj_b")
    pb = _norm_matmul(xs, attn_norm[1], w_in_b[0], tm=1024, tn=512, name="in_proj_b")
    mix = _stick_attn(pb, kvb)
    memkv = _norm_matmul(mems, mem_norm[1], w_mem_kv[1], tm=256, tn=512, name="mem_kv")
    mo = _mem_attn(pb, N_MIX_HEADS, memkv, mem_qk_norm[1])
    xs = _out_proj(mix, mo, w_out[1], xs)
    xs = _moe_block(xs, ffn_norm[1], router_w[0], router_b[0], moe_w_gu[0], moe_w_down[0])
    return xs[None]
```

```python
import functools

import numpy as np
import jax
import jax.numpy as jnp
from jax import lax
from jax.experimental import pallas as pl
from jax.experimental.pallas import tpu as pltpu

HEAD_DIM = 128
N_MEM_HEADS = 4
N_MIX_HEADS = 12
N_KV_GROUPS = 3
HEADS_PER_GROUP = 4
CMP_STRIDE = 16
CMP_BLOCK = 32
SLC_BLOCK = 64
SLC_TOPK = 16
WINDOW = 512
N_EXPERTS = 8
Q_W = N_MIX_HEADS * HEAD_DIM
KV_W = N_KV_GROUPS * HEAD_DIM
GATE_W = 3 * N_MIX_HEADS
MEM_Q_W = N_MEM_HEADS * HEAD_DIM
NEG = -1e30
BIG = 1e6
EPS = 1e-6
SCALE = HEAD_DIM ** -0.5
LOG2E = 1.4426950408889634
STICK_DEAD = -151.0
LANES = 128
VMEM_LIMIT = 48 * 1024 * 1024
MOE_VMEM_LIMIT = 56 * 1024 * 1024
WIDE_VMEM_LIMIT = 56 * 1024 * 1024
MOE_SUB = 256
MOE_TILES_PER_GROUP = 9

F32 = jnp.float32
BF16 = jnp.bfloat16


def _cparams(sem, vmem=VMEM_LIMIT):
    return pltpu.CompilerParams(dimension_semantics=sem, vmem_limit_bytes=vmem)


def _nt_dot(a, b):
    return lax.dot_general(a, b, (((1,), (1,)), ((), ())), preferred_element_type=F32)


def _rms(x, gain):
    return x * lax.rsqrt(jnp.mean(x * x, axis=-1, keepdims=True) + EPS) * gain


def _sigmoid(x):
    return 1.0 / (1.0 + jnp.exp(-x))


def _rmsnorm_cast_kernel(x_ref, g_ref, o_ref):
    o_ref[...] = _rms(x_ref[...], g_ref[...]).astype(o_ref.dtype)


def _rmsnorm_cast(x, gain, tm=512):
    m, d = x.shape
    tm = min(tm, m)
    return pl.pallas_call(
        _rmsnorm_cast_kernel,
        out_shape=jax.ShapeDtypeStruct((m, d), BF16),
        grid=(m // tm,),
        in_specs=[pl.BlockSpec((tm, d), lambda i: (i, 0)), pl.BlockSpec((1, d), lambda i: (0, 0))],
        out_specs=pl.BlockSpec((tm, d), lambda i: (i, 0)),
        compiler_params=_cparams(("parallel",)),
        name="rmsnorm_cast",
    )(x, gain.reshape(1, d))


def _norm_matmul_kernel(x_ref, g_ref, w_ref, o_ref, xn_ref):
    @pl.when(pl.program_id(1) == 0)
    def _():
        xn_ref[...] = _rms(x_ref[...], g_ref[...]).astype(BF16)

    o_ref[...] = jnp.dot(xn_ref[...], w_ref[...].astype(BF16), preferred_element_type=F32).astype(o_ref.dtype)


def _norm_matmul(x, gain, w, *, tm, tn, out_dtype=F32, name="norm_matmul"):
    m, d = x.shape
    n = w.shape[1]
    tm = min(tm, m)
    return pl.pallas_call(
        _norm_matmul_kernel,
        out_shape=jax.ShapeDtypeStruct((m, n), out_dtype),
        grid=(m // tm, pl.cdiv(n, tn)),
        in_specs=[pl.BlockSpec((tm, d), lambda i, j: (i, 0)),
                  pl.BlockSpec((1, d), lambda i, j: (0, 0)),
                  pl.BlockSpec((d, tn), lambda i, j: (0, j))],
        out_specs=pl.BlockSpec((tm, tn), lambda i, j: (i, j)),
        scratch_shapes=[pltpu.VMEM((tm, d), BF16)],
        compiler_params=_cparams(("parallel", "arbitrary"), WIDE_VMEM_LIMIT),
        name=name,
    )(x, gain.reshape(1, d), w)


def _matmul_kernel(*refs, nk, has_res):
    if has_res:
        a_ref, w_ref, r_ref, o_ref = refs[:4]
    else:
        a_ref, w_ref, o_ref = refs[:3]
        r_ref = None
    part = jnp.dot(a_ref[...], w_ref[...].astype(BF16), preferred_element_type=F32)
    if nk == 1:
        if has_res:
            part = part + r_ref[...]
        o_ref[...] = part.astype(o_ref.dtype)
        return
    acc_ref = refs[-1]
    k = pl.program_id(2)

    @pl.when(k == 0)
    def _():
        acc_ref[...] = part

    @pl.when(k > 0)
    def _():
        acc_ref[...] += part

    @pl.when(k == nk - 1)
    def _():
        r = acc_ref[...]
        if has_res:
            r = r + r_ref[...]
        o_ref[...] = r.astype(o_ref.dtype)


def _matmul(a, w, res=None, *, tm, tn, tk=None, out_dtype=F32, name="matmul"):
    m, kdim = a.shape
    n = w.shape[1]
    tm = min(tm, m)
    tk = kdim if tk is None else tk
    nk = kdim // tk
    in_specs = [pl.BlockSpec((tm, tk), lambda i, j, k: (i, k)),
                pl.BlockSpec((tk, tn), lambda i, j, k: (k, j))]
    args = [a, w]
    if res is not None:
        in_specs.append(pl.BlockSpec((tm, tn), lambda i, j, k: (i, j)))
        args.append(res)
    scratch = [pltpu.VMEM((tm, tn), F32)] if nk > 1 else []
    return pl.pallas_call(
        functools.partial(_matmul_kernel, nk=nk, has_res=res is not None),
        out_shape=jax.ShapeDtypeStruct((m, n), out_dtype),
        grid=(m // tm, pl.cdiv(n, tn), nk),
        in_specs=in_specs,
        out_specs=pl.BlockSpec((tm, tn), lambda i, j, k: (i, j)),
        scratch_shapes=scratch,
        compiler_params=_cparams(("parallel", "parallel", "arbitrary")),
        name=name,
    )(*args)


def _out_proj_kernel(mix_ref, mo_ref, w1_ref, w2_ref, r_ref, o_ref):
    o_ref[...] = (jnp.dot(mix_ref[...], w1_ref[...].astype(BF16), preferred_element_type=F32)
                  + jnp.dot(mo_ref[...], w2_ref[...].astype(BF16), preferred_element_type=F32)
                  + r_ref[...])


def _out_proj(mix, mo, w, res, tm=1024, tn=1024):
    t = mix.shape[0]
    tm = min(tm, t)
    n = w.shape[1]
    assert Q_W % MEM_Q_W == 0
    return pl.pallas_call(
        _out_proj_kernel,
        out_shape=jax.ShapeDtypeStruct((t, n), F32),
        grid=(t // tm, n // tn),
        in_specs=[pl.BlockSpec((tm, Q_W), lambda i, j: (i, 0)),
                  pl.BlockSpec((tm, MEM_Q_W), lambda i, j: (i, 0)),
                  pl.BlockSpec((Q_W, tn), lambda i, j: (0, j)),
                  pl.BlockSpec((MEM_Q_W, tn), lambda i, j: (Q_W // MEM_Q_W, j)),
                  pl.BlockSpec((tm, tn), lambda i, j: (i, j))],
        out_specs=pl.BlockSpec((tm, tn), lambda i, j: (i, j)),
        compiler_params=_cparams(("parallel", "parallel")),
        name="out_proj",
    )(mix, mo, w, w, res)


def _swiglu_kernel(x_ref, gn_ref, wg_ref, wu_ref, o_ref, xn_ref):
    @pl.when(pl.program_id(1) == 0)
    def _():
        xn_ref[...] = _rms(x_ref[...], gn_ref[...]).astype(BF16)

    a = xn_ref[...]
    g = jnp.dot(a, wg_ref[...].astype(BF16), preferred_element_type=F32)
    u = jnp.dot(a, wu_ref[...].astype(BF16), preferred_element_type=F32)
    o_ref[...] = (g * _sigmoid(g) * u).astype(o_ref.dtype)


def _swiglu_up(x, gain, w_gu, *, tm=1024, tn=512):
    m, d = x.shape
    f = w_gu.shape[1] // 2
    tm = min(tm, m)
    nj = f // tn
    return pl.pallas_call(
        _swiglu_kernel,
        out_shape=jax.ShapeDtypeStruct((m, f), BF16),
        grid=(m // tm, nj),
        in_specs=[pl.BlockSpec((tm, d), lambda i, j: (i, 0)),
                  pl.BlockSpec((1, d), lambda i, j: (0, 0)),
                  pl.BlockSpec((d, tn), lambda i, j: (0, j)),
                  pl.BlockSpec((d, tn), lambda i, j: (0, j + nj))],
        out_specs=pl.BlockSpec((tm, tn), lambda i, j: (i, j)),
        scratch_shapes=[pltpu.VMEM((tm, d), BF16)],
        compiler_params=_cparams(("parallel", "arbitrary"), WIDE_VMEM_LIMIT),
        name="swiglu_up",
    )(x, gain.reshape(1, d), w_gu, w_gu)


def _nsa_prep_kernel(p_ref, gtab_ref, o_ref):
    for c in range((Q_W + 6 * KV_W) // HEAD_DIM):
        x = p_ref[:, c * HEAD_DIM:(c + 1) * HEAD_DIM]
        if c < N_MIX_HEADS:
            y = _rms(x, gtab_ref[0:1, :]) * (SCALE * LOG2E)
        elif 18 <= c < 21:
            y = _rms(x, gtab_ref[1:2, :])
        elif 24 <= c < 27:
            y = _rms(x, gtab_ref[2:3, :])
        else:
            y = x
        o_ref[:, c * HEAD_DIM:(c + 1) * HEAD_DIM] = y.astype(o_ref.dtype)


def _nsa_prep(p, q_gain, k_gain, tm=256):
    t = p.shape[0]
    tm = min(tm, t)
    w = Q_W + 6 * KV_W
    gtab = jnp.zeros((8, HEAD_DIM), F32).at[0].set(q_gain).at[1].set(k_gain[1]).at[2].set(k_gain[2])
    return pl.pallas_call(
        _nsa_prep_kernel,
        out_shape=jax.ShapeDtypeStruct((t, w), BF16),
        grid=(t // tm,),
        in_specs=[pl.BlockSpec((tm, w), lambda i: (i, 0)), pl.BlockSpec((8, HEAD_DIM), lambda i: (0, 0))],
        out_specs=pl.BlockSpec((tm, w), lambda i: (i, 0)),
        compiler_params=_cparams(("parallel",)),
        name="nsa_prep",
    )(p, gtab)


def _compress_kernel(p_ref, pos_ref, w_ref, g_ref, o_ref):
    ncp = o_ref.shape[2]
    first = jnp.zeros((ncp, HEAD_DIM), F32)
    second = jnp.zeros((ncp, HEAD_DIM), F32)
    for r in range(CMP_STRIDE):
        rows = p_ref[pl.ds(r, ncp, stride=CMP_STRIDE), :]
        lo = (rows + pos_ref[0, r:r + 1, :]).astype(BF16)
        hi = (rows + pos_ref[0, CMP_STRIDE + r:CMP_STRIDE + r + 1, :]).astype(BF16)
        first += jnp.dot(lo, w_ref[0, r * HEAD_DIM:(r + 1) * HEAD_DIM, :].astype(BF16),
                         preferred_element_type=F32)
        second += jnp.dot(hi, w_ref[0, (CMP_STRIDE + r) * HEAD_DIM:(CMP_STRIDE + r + 1) * HEAD_DIM, :].astype(BF16),
                          preferred_element_type=F32)
    y = first + jnp.concatenate([second[1:], jnp.zeros((1, HEAD_DIM), F32)], axis=0)
    is_k = pl.program_id(0) == 0
    o_ref[0, 0] = jnp.where(is_k, _rms(y, g_ref[...]), y).astype(o_ref.dtype)


def _compress(p, cmp_w, cmp_pos, k_gain0):
    t = p.shape[0]
    ncp = t // CMP_STRIDE
    col0 = Q_W // HEAD_DIM
    return pl.pallas_call(
        _compress_kernel,
        out_shape=jax.ShapeDtypeStruct((2, N_KV_GROUPS, ncp, HEAD_DIM), BF16),
        grid=(2, N_KV_GROUPS),
        in_specs=[pl.BlockSpec((t, HEAD_DIM), lambda s, gi: (0, col0 + N_KV_GROUPS * s + gi)),
                  pl.BlockSpec((1, CMP_BLOCK, HEAD_DIM), lambda s, gi: (s, 0, 0)),
                  pl.BlockSpec((1, CMP_BLOCK * HEAD_DIM, HEAD_DIM), lambda s, gi: (s, 0, 0)),
                  pl.BlockSpec((1, HEAD_DIM), lambda s, gi: (0, 0))],
        out_specs=pl.BlockSpec((1, 1, ncp, HEAD_DIM), lambda s, gi: (s, gi, 0, 0)),
        compiler_params=_cparams(("parallel", "parallel")),
        name="nsa_compress",
    )(p, cmp_pos, cmp_w, k_gain0.reshape(1, HEAD_DIM))


def _cmp_kernel(slopes_ref, q_ref, k_ref, v_ref, imap_ref, o_ref, score_ref, *, tq, ncp):
    g = pl.program_id(0)
    t0 = pl.program_id(1) * tq
    row = lax.broadcasted_iota(jnp.int32, (tq, ncp), 0) + t0
    col = lax.broadcasted_iota(jnp.int32, (tq, ncp), 1)
    c_end = col * CMP_STRIDE + (CMP_BLOCK - 1)
    neg = jnp.where(row >= c_end, 0.0, NEG)
    colb = (lax.broadcasted_iota(jnp.int32, (1, ncp), 1) * CMP_STRIDE + (CMP_BLOCK - 1) - t0).astype(F32)
    has_key = (lax.broadcasted_iota(jnp.int32, (tq, 1), 0) + t0 >= CMP_BLOCK - 1).astype(F32)
    k = k_ref[0, 0]
    v = v_ref[0, 0]
    psum = jnp.zeros((tq, ncp), F32)
    for p in range(HEADS_PER_GROUP):
        slope = slopes_ref[g * HEADS_PER_GROUP + p]
        q = q_ref[:, p * HEAD_DIM:(p + 1) * HEAD_DIM]
        s = _nt_dot(q, k) + slope * colb + neg
        e = jnp.exp2(s - jnp.max(s, axis=1, keepdims=True))
        pr = e * (has_key / jnp.sum(e, axis=1, keepdims=True))
        psum = psum + pr
        o_ref[:, p * HEAD_DIM:(p + 1) * HEAD_DIM] = jnp.dot(
            pr.astype(BF16), v, preferred_element_type=F32).astype(o_ref.dtype)

    imap = imap_ref[...]
    hi = psum.astype(BF16)
    r1 = psum - hi.astype(F32)
    mid = r1.astype(BF16)
    lo = (r1 - mid.astype(F32)).astype(BF16)
    imp = (jnp.dot(hi, imap, preferred_element_type=F32)
           + jnp.dot(mid, imap, preferred_element_type=F32)
           + jnp.dot(lo, imap, preferred_element_type=F32))
    nb = imp.shape[1]
    t = lax.broadcasted_iota(jnp.int32, (tq, nb), 0) + t0
    blk = lax.broadcasted_iota(jnp.int32, (tq, nb), 1)
    score_ref[0] = jnp.where(blk == t // SLC_BLOCK, 2 * BIG,
                             jnp.where(blk == 0, BIG,
                                       jnp.where(blk * SLC_BLOCK <= t, imp, -BIG)))


def _cmp_attn(slopes, qkv, kvc, imap, tq=256):
    t = qkv.shape[0]
    tq = min(tq, t)
    ncp = kvc.shape[2]
    nb = t // SLC_BLOCK
    gw = HEADS_PER_GROUP * HEAD_DIM
    return pl.pallas_call(
        functools.partial(_cmp_kernel, tq=tq, ncp=ncp),
        out_shape=(jax.ShapeDtypeStruct((t, Q_W), F32),
                   jax.ShapeDtypeStruct((N_KV_GROUPS, t, nb), F32)),
        grid_spec=pltpu.PrefetchScalarGridSpec(
            num_scalar_prefetch=1,
            grid=(N_KV_GROUPS, t // tq),
            in_specs=[pl.BlockSpec((tq, gw), lambda g, i, s: (i, g)),
                      pl.BlockSpec((1, 1, ncp, HEAD_DIM), lambda g, i, s: (0, g, 0, 0)),
                      pl.BlockSpec((1, 1, ncp, HEAD_DIM), lambda g, i, s: (1, g, 0, 0)),
                      pl.BlockSpec((ncp, nb), lambda g, i, s: (0, 0))],
            out_specs=(pl.BlockSpec((tq, gw), lambda g, i, s: (i, g)),
                       pl.BlockSpec((1, tq, nb), lambda g, i, s: (g, i, 0)))),
        compiler_params=_cparams(("parallel", "parallel")),
        name="nsa_cmp",
    )(slopes, qkv, kvc, kvc, imap)


def _topk_kernel(score_ref, sel_ref, act_ref, *, tm, sub):
    score = score_ref[0]
    nb = score.shape[1]
    blk_f = lax.broadcasted_iota(jnp.int32, (tm, nb), 1).astype(F32)

    def pick_one(_, carry):
        work, sel = carry
        m = jnp.max(work, axis=1, keepdims=True)
        first = jnp.min(jnp.where(work == m, blk_f, float(nb)), axis=1, keepdims=True)
        pick = blk_f == first
        return jnp.where(pick, -3e38, work), jnp.where(pick, 1.0, sel)

    forced = score >= 0.5 * BIG
    start = (jnp.where(forced, -3e38, score), jnp.where(forced, 1.0, 0.0))
    _, sel = lax.fori_loop(0, max(min(SLC_TOPK, nb) - 2, 0), pick_one, start)
    sel_ref[0] = sel.astype(sel_ref.dtype)
    for a in range(tm // sub):
        act_ref[0, a] = jnp.broadcast_to(jnp.max(sel[a * sub:(a + 1) * sub], axis=0, keepdims=True),
                                         act_ref.shape[2:])


def _topk_select(score, sub, tm=1024):
    g, t, nb = score.shape
    tm = min(tm, t)
    sub = min(sub, tm)
    return pl.pallas_call(
        functools.partial(_topk_kernel, tm=tm, sub=sub),
        out_shape=(jax.ShapeDtypeStruct((g, t, nb), BF16),
                   jax.ShapeDtypeStruct((g, t // sub, 8, nb), F32)),
        grid=(g, t // tm),
        in_specs=[pl.BlockSpec((1, tm, nb), lambda gi, i: (gi, i, 0))],
        out_specs=(pl.BlockSpec((1, tm, nb), lambda gi, i: (gi, i, 0)),
                   pl.BlockSpec((1, tm // sub, 8, nb), lambda gi, i: (gi, i, 0, 0))),
        compiler_params=_cparams(("parallel", "parallel")),
        name="nsa_topk",
    )(score)


def _slc_kernel(slopes_ref, flags_ref, q_ref, k_ref, v_ref, sel_ref, o_ref, m_ref, l_ref, acc_ref, *, tq, tk):
    g = pl.program_id(0)
    i = pl.program_id(1)
    t0 = i * tq
    flag_base = (g * pl.num_programs(1) + i) * (k_ref.shape[0] // tk)
    nb = sel_ref.shape[2]
    m_ref[...] = jnp.full(m_ref.shape, NEG, F32)
    l_ref[...] = jnp.zeros(l_ref.shape, F32)
    acc_ref[...] = jnp.zeros(acc_ref.shape, F32)
    sel = sel_ref[0]
    bpt = tk // SLC_BLOCK

    def body(j, carry):
        @pl.when(flags_ref[flag_base + j] > 0)
        def _():
            tile(j, False)
        return carry

    def tile(j, diagonal):
        k0 = pl.multiple_of(j * tk, tk)
        k = k_ref[pl.ds(k0, tk), :]
        v = v_ref[pl.ds(k0, tk), :]
        eb = lax.broadcasted_iota(jnp.int32, (nb, tk), 0)
        ec = lax.broadcasted_iota(jnp.int32, (nb, tk), 1)
        expand = jnp.where(eb == ec // SLC_BLOCK + j * bpt, 1.0, 0.0).astype(BF16)
        selx = jnp.dot(sel, expand, preferred_element_type=F32)
        if diagonal:
            later = lax.broadcasted_iota(jnp.int32, (tq, tk), 0) < lax.broadcasted_iota(jnp.int32, (tq, tk), 1)
            selx = jnp.where(later, 0.0, selx)
        neg = jnp.where(selx > 0.5, 0.0, NEG)
        colb = (lax.broadcasted_iota(jnp.int32, (1, tk), 1) + (k0 - t0)).astype(F32)
        for p in range(HEADS_PER_GROUP):
            slope = slopes_ref[g * HEADS_PER_GROUP + p]
            q = q_ref[:, p * HEAD_DIM:(p + 1) * HEAD_DIM]
            s = _nt_dot(q, k) + slope * colb + neg
            m_old = m_ref[p]
            m_new = jnp.maximum(m_old, jnp.max(s, axis=1, keepdims=True))
            alpha = jnp.exp2(m_old - m_new)
            e = jnp.exp2(s - jnp.tile(m_new, (1, tk // LANES)))
            l_ref[p] = alpha * l_ref[p] + jnp.sum(e, axis=1, keepdims=True)
            acc_ref[p] = alpha * acc_ref[p] + jnp.dot(e.astype(BF16), v, preferred_element_type=F32)
            m_ref[p] = m_new

    lax.fori_loop(0, i, body, 0)
    tile(i, True)
    for p in range(HEADS_PER_GROUP):
        o_ref[:, p * HEAD_DIM:(p + 1) * HEAD_DIM] = (acc_ref[p] / l_ref[p]).astype(o_ref.dtype)


def _slc_attn(slopes, qkv, sel, blk_act, tq=256, tk=256):
    t = qkv.shape[0]
    tq = min(tq, t)
    tk = min(tk, t)
    nb = sel.shape[2]
    assert blk_act.shape[1] == t // tq and tq == tk
    gw = HEADS_PER_GROUP * HEAD_DIM
    ks_col = (Q_W + 2 * KV_W) // HEAD_DIM
    vs_col = (Q_W + 3 * KV_W) // HEAD_DIM
    flags = jnp.max(blk_act.reshape(N_KV_GROUPS, t // tq, t // tk, tk // SLC_BLOCK), axis=-1)
    flags = (flags > 0.5).astype(jnp.int32).reshape(-1)
    return pl.pallas_call(
        functools.partial(_slc_kernel, tq=tq, tk=tk),
        out_shape=jax.ShapeDtypeStruct((t, Q_W), F32),
        grid_spec=pltpu.PrefetchScalarGridSpec(
            num_scalar_prefetch=2,
            grid=(N_KV_GROUPS, t // tq),
            in_specs=[pl.BlockSpec((tq, gw), lambda g, i, s, f: (i, g)),
                      pl.BlockSpec((t, HEAD_DIM), lambda g, i, s, f: (0, ks_col + g)),
                      pl.BlockSpec((t, HEAD_DIM), lambda g, i, s, f: (0, vs_col + g)),
                      pl.BlockSpec((1, tq, nb), lambda g, i, s, f: (g, i, 0))],
            out_specs=pl.BlockSpec((tq, gw), lambda g, i, s, f: (i, g)),
            scratch_shapes=[pltpu.VMEM((HEADS_PER_GROUP, tq, LANES), F32),
                            pltpu.VMEM((HEADS_PER_GROUP, tq, LANES), F32),
                            pltpu.VMEM((HEADS_PER_GROUP, tq, HEAD_DIM), F32)]),
        compiler_params=_cparams(("parallel", "parallel")),
        name="nsa_selected",
    )(slopes, flags, qkv, qkv, qkv, sel)


def _win_kernel(slopes_ref, q_ref, k0_ref, k1_ref, k2_ref, v0_ref, v1_ref, v2_ref, o_ref, *, tq):
    g = pl.program_id(0)
    i = pl.program_id(1)
    nprev = WINDOW // tq
    kw = (nprev + 1) * tq
    k = jnp.concatenate([k0_ref[...], k1_ref[...], k2_ref[...]], axis=0)
    v = jnp.concatenate([v0_ref[...], v1_ref[...], v2_ref[...]], axis=0)
    r = lax.broadcasted_iota(jnp.int32, (tq, kw), 0)
    c = lax.broadcasted_iota(jnp.int32, (tq, kw), 1)
    rel = WINDOW + r - c
    first_valid = (nprev - i) * tq
    inside = jnp.where(rel >= 0, jnp.where(rel < WINDOW, 1.0, 0.0), 0.0)
    neg = jnp.where(jnp.where(c >= first_valid, inside, 0.0) > 0.5, 0.0, NEG)
    colb = (lax.broadcasted_iota(jnp.int32, (1, kw), 1) - WINDOW).astype(F32)
    for p in range(HEADS_PER_GROUP):
        slope = slopes_ref[g * HEADS_PER_GROUP + p]
        q = q_ref[:, p * HEAD_DIM:(p + 1) * HEAD_DIM]
        s = _nt_dot(q, k) + slope * colb + neg
        e = jnp.exp2(s - jnp.max(s, axis=1, keepdims=True))
        o = jnp.dot(e.astype(BF16), v, preferred_element_type=F32) / jnp.sum(e, axis=1, keepdims=True)
        o_ref[:, p * HEAD_DIM:(p + 1) * HEAD_DIM] = o.astype(o_ref.dtype)


def _win_attn(slopes, qkv, tq=256):
    t = qkv.shape[0]
    assert WINDOW % tq == 0 and WINDOW // tq == 2
    gw = HEADS_PER_GROUP * HEAD_DIM
    kw_col = (Q_W + 4 * KV_W) // HEAD_DIM
    vw_col = (Q_W + 5 * KV_W) // HEAD_DIM

    def kv_spec(col, back):
        return pl.BlockSpec((tq, HEAD_DIM), lambda g, i, s: (jnp.maximum(i - back, 0), col + g))

    return pl.pallas_call(
        functools.partial(_win_kernel, tq=tq),
        out_shape=jax.ShapeDtypeStruct((t, Q_W), F32),
        grid_spec=pltpu.PrefetchScalarGridSpec(
            num_scalar_prefetch=1,
            grid=(N_KV_GROUPS, t // tq),
            in_specs=[pl.BlockSpec((tq, gw), lambda g, i, s: (i, g)),
                      kv_spec(kw_col, 2), kv_spec(kw_col, 1), kv_spec(kw_col, 0),
                      kv_spec(vw_col, 2), kv_spec(vw_col, 1), kv_spec(vw_col, 0)],
            out_specs=pl.BlockSpec((tq, gw), lambda g, i, s: (i, g))),
        compiler_params=_cparams(("parallel", "parallel")),
        name="nsa_window",
    )(slopes, qkv, qkv, qkv, qkv, qkv, qkv, qkv)


def _gate_sum_kernel(oc_ref, os_ref, ow_ref, g_ref, o_ref):
    sg = _sigmoid(g_ref[...])
    for h in range(N_MIX_HEADS):
        sl = slice(h * HEAD_DIM, (h + 1) * HEAD_DIM)
        o = (oc_ref[:, sl] * sg[:, 3 * h:3 * h + 1]
             + os_ref[:, sl] * sg[:, 3 * h + 1:3 * h + 2]
             + ow_ref[:, sl] * sg[:, 3 * h + 2:3 * h + 3])
        o_ref[:, sl] = o.astype(o_ref.dtype)


def _gate_sum(o_cmp, o_slc, o_win, gates, tm=512):
    t = o_cmp.shape[0]
    tm = min(tm, t)
    big = pl.BlockSpec((tm, Q_W), lambda i: (i, 0))
    return pl.pallas_call(
        _gate_sum_kernel,
        out_shape=jax.ShapeDtypeStruct((t, Q_W), BF16),
        grid=(t // tm,),
        in_specs=[big, big, big, pl.BlockSpec((tm, GATE_W), lambda i: (i, 0))],
        out_specs=big,
        compiler_params=_cparams(("parallel",)),
        name="nsa_gate_sum",
    )(o_cmp, o_slc, o_win, gates)


def _mem_attn_kernel(q_ref, k_ref, v_ref, g_ref, o_ref):
    q = (_rms(q_ref[...], g_ref[0:1, :]) * SCALE).astype(BF16)
    k = _rms(k_ref[...], g_ref[1:2, :]).astype(BF16)
    s = _nt_dot(q, k)
    e = jnp.exp(s - jnp.max(s, axis=1, keepdims=True))
    p = e * (1.0 / jnp.sum(e, axis=1, keepdims=True))
    o_ref[...] = jnp.dot(p.astype(BF16), v_ref[...].astype(BF16),
                         preferred_element_type=F32).astype(o_ref.dtype)


def _mem_attn(qsrc, q_col, memkv, qk_gain, tq=2048):
    t = qsrc.shape[0]
    tq = min(tq, t)
    nm = memkv.shape[0]
    gtab = jnp.zeros((8, HEAD_DIM), F32).at[0:2].set(qk_gain)
    return pl.pallas_call(
        _mem_attn_kernel,
        out_shape=jax.ShapeDtypeStruct((t, MEM_Q_W), BF16),
        grid=(N_MEM_HEADS, t // tq),
        in_specs=[pl.BlockSpec((tq, HEAD_DIM), lambda h, i: (i, q_col + h)),
                  pl.BlockSpec((nm, HEAD_DIM), lambda h, i: (0, h)),
                  pl.BlockSpec((nm, HEAD_DIM), lambda h, i: (0, N_MEM_HEADS + h)),
                  pl.BlockSpec((8, HEAD_DIM), lambda h, i: (0, 0))],
        out_specs=pl.BlockSpec((tq, HEAD_DIM), lambda h, i: (i, h)),
        compiler_params=_cparams(("parallel", "parallel")),
        name="mem_attn",
    )(qsrc, memkv, memkv, gtab)


def _stick_kernel(q_ref, k_ref, v_ref, o_ref, *, tq, hp):
    i = pl.program_id(1)
    r = lax.broadcasted_iota(jnp.int32, (tq, tq), 0)
    c = lax.broadcasted_iota(jnp.int32, (tq, tq), 1)
    later = jnp.where(r > c, 1.0, 0.0).astype(BF16)
    qs = [(q_ref[:, h * HEAD_DIM:(h + 1) * HEAD_DIM] * (SCALE * LOG2E)).astype(BF16) for h in range(hp)]

    def tile(h, k0, tail, acc, diagonal):
        k = k_ref[pl.ds(k0, tq), h * HEAD_DIM:(h + 1) * HEAD_DIM]
        v = v_ref[pl.ds(k0, tq), h * HEAD_DIM:(h + 1) * HEAD_DIM]
        z = _nt_dot(qs[h], k)
        sp = jnp.maximum(z, 0.0) + jnp.log2(1.0 + jnp.exp2(-jnp.abs(z)))
        log_keep = jnp.where(r > c, -sp, 0.0) if diagonal else -sp
        hi = log_keep.astype(BF16)
        lo = (log_keep - hi.astype(F32)).astype(BF16)
        after = (jnp.dot(hi, later, preferred_element_type=F32)
                 + jnp.dot(lo, later, preferred_element_type=F32) + tail)
        a = jnp.exp2(z - sp + after)
        if diagonal:
            a = jnp.where(r > c, a, 0.0)
        acc = acc + jnp.dot(a.astype(BF16), v, preferred_element_type=F32)
        return tail + jnp.sum(log_keep, axis=1, keepdims=True), acc

    k_diag = pl.multiple_of(i * tq, tq)
    state = [tile(h, k_diag, jnp.zeros((tq, 1), F32), jnp.zeros((tq, HEAD_DIM), F32), True) for h in range(hp)]

    def live(st):
        return jnp.logical_and(st[0] <= i, st[1] > 0)

    def body(st):
        k0 = pl.multiple_of((i - st[0]) * tq, tq)
        new = [tile(h, k0, st[2 + 2 * h], st[3 + 2 * h], False) for h in range(hp)]
        top = functools.reduce(jnp.maximum, [jnp.max(tl) for tl, _ in new])
        return (st[0] + 1, (top > STICK_DEAD).astype(jnp.int32)) + tuple(x for pair in new for x in pair)

    st = lax.while_loop(live, body, (jnp.int32(1), jnp.int32(1)) + tuple(x for pair in state for x in pair))
    for h in range(hp):
        o_ref[:, h * HEAD_DIM:(h + 1) * HEAD_DIM] = st[3 + 2 * h].astype(o_ref.dtype)


def _stick_attn(pb, kvb, tq=256, hp=2):
    t = pb.shape[0]
    tq = min(tq, t)
    hw = hp * HEAD_DIM
    return pl.pallas_call(
        functools.partial(_stick_kernel, tq=tq, hp=hp),
        out_shape=jax.ShapeDtypeStruct((t, Q_W), BF16),
        grid=(N_MIX_HEADS // hp, t // tq),
        in_specs=[pl.BlockSpec((tq, hw), lambda h, i: (i, h)),
                  pl.BlockSpec((t, hw), lambda h, i: (0, h)),
                  pl.BlockSpec((t, hw), lambda h, i: (0, N_MIX_HEADS // hp + h))],
        out_specs=pl.BlockSpec((tq, hw), lambda h, i: (i, h)),
        compiler_params=_cparams(("parallel", "parallel")),
        name="stick_breaking",
    )(pb, kvb, kvb)


def _router_kernel(x_ref, g_ref, w_ref, b_ref, c_ref):
    h = _rms(x_ref[...], g_ref[...])
    logits = jnp.dot(h, w_ref[...], preferred_element_type=F32, precision=lax.Precision.HIGHEST)
    lane = lax.broadcasted_iota(jnp.int32, logits.shape, 1).astype(F32)
    biased = jnp.where(lane < N_EXPERTS, logits + b_ref[...], -3e38)
    m1 = jnp.max(biased, axis=1, keepdims=True)
    i1 = jnp.min(jnp.where(biased == m1, lane, float(LANES)), axis=1, keepdims=True)
    rest = jnp.where(lane == i1, -3e38, biased)
    m2 = jnp.max(rest, axis=1, keepdims=True)
    i2 = jnp.min(jnp.where(rest == m2, lane, float(LANES)), axis=1, keepdims=True)
    l1 = jnp.sum(jnp.where(lane == i1, logits, 0.0), axis=1, keepdims=True)
    l2 = jnp.sum(jnp.where(lane == i2, logits, 0.0), axis=1, keepdims=True)
    mx = jnp.maximum(l1, l2)
    e1 = jnp.exp(l1 - mx)
    e2 = jnp.exp(l2 - mx)
    inv = 1.0 / (e1 + e2)
    c_ref[...] = (jnp.where(lane == 0.0, i1, 0.0) + jnp.where(lane == 1.0, i2, 0.0)
                  + jnp.where(lane == 2.0, e1 * inv, 0.0) + jnp.where(lane == 3.0, e2 * inv, 0.0))


def _router(x, gain, router_w, router_b, tm=512):
    t, d = x.shape
    tm = min(tm, t)
    w = jnp.zeros((d, LANES), F32).at[:, :N_EXPERTS].set(router_w)
    b = jnp.zeros((1, LANES), F32).at[0, :N_EXPERTS].set(router_b)
    return pl.pallas_call(
        _router_kernel,
        out_shape=jax.ShapeDtypeStruct((t, LANES), F32),
        grid=(t // tm,),
        in_specs=[pl.BlockSpec((tm, d), lambda i: (i, 0)), pl.BlockSpec((1, d), lambda i: (0, 0)),
                  pl.BlockSpec((d, LANES), lambda i: (0, 0)), pl.BlockSpec((1, LANES), lambda i: (0, 0))],
        out_specs=pl.BlockSpec((tm, LANES), lambda i: (i, 0)),
        compiler_params=_cparams(("parallel",)),
        name="moe_router",
    )(x, gain.reshape(1, d), w, b)


def _route_plan(rt, sub, per_group):
    t = rt.shape[0]
    i32 = jnp.int32
    e_flat = rt[:, 0:2].astype(i32).reshape(-1)
    onehot = (e_flat[:, None] == jnp.arange(N_EXPERTS, dtype=i32)[None, :]).astype(i32)
    csum = jnp.cumsum(onehot, axis=0)
    rank = jnp.sum(csum * onehot, axis=1) - 1
    tiles_e = (csum[-1] + sub - 1) // sub
    tile_end = jnp.cumsum(tiles_e)
    tile_start = tile_end - tiles_e
    dest = tile_start[e_flat] * sub + rank
    groups_e = (tiles_e + per_group - 1) // per_group
    group_end = jnp.cumsum(groups_e)
    n_groups = group_end[-1]
    max_groups = (2 * t // sub + N_EXPERTS) // per_group + N_EXPERTS
    gidx = jnp.arange(max_groups, dtype=i32)
    gcl = jnp.minimum(gidx, n_groups - 1)
    g_expert = jnp.sum((gcl[:, None] >= group_end[None, :]).astype(i32), axis=1)
    k_in_e = gcl - (group_end - groups_e)[g_expert]
    g_start = (tile_start[g_expert] + k_in_e * per_group) * sub
    g_tiles = jnp.where(gidx < n_groups, jnp.clip(tiles_e[g_expert] - k_in_e * per_group, 0, per_group), 0)
    return (dest.astype(i32), tile_end[-1].reshape(1).astype(i32),
            g_expert.astype(i32), g_start.astype(i32), g_tiles.astype(i32))


def _dispatch_kernel(dest_ref, nt_ref, x_hbm, g_ref, o_ref, tok_ref, xbuf, sems, *, tm):
    n = pl.program_id(0)
    n_tiles = nt_ref[0]

    def gather(tile, slot):
        base = tile * tm

        def issue(r, c):
            pltpu.make_async_copy(x_hbm.at[pl.ds(tok_ref[base + r], 1), :],
                                  xbuf.at[slot, pl.ds(r, 1), :], sems.at[slot]).start()
            return c

        lax.fori_loop(0, tm, issue, 0, unroll=8)

    @pl.when(n == 0)
    def _():
        def clear(r, c):
            tok_ref[r] = 0
            return c

        lax.fori_loop(0, tok_ref.shape[0], clear, 0, unroll=8)

        def fill(q, c):
            rows = [dest_ref[q * 8 + u] for u in range(8)]
            for u in range(8):
                tok_ref[rows[u]] = q * 4 + u // 2
            return c

        lax.fori_loop(0, dest_ref.shape[0] // 8, fill, 0)
        gather(0, 0)

    slot = n % 2

    @pl.when(n + 1 < n_tiles)
    def _():
        gather(n + 1, 1 - slot)

    @pl.when(n < n_tiles)
    def _():
        pltpu.make_async_copy(x_hbm.at[pl.ds(0, tm), :], xbuf.at[slot], sems.at[slot]).wait()
        o_ref[...] = _rms(xbuf[slot], g_ref[...]).astype(o_ref.dtype)

    @pl.when(n >= n_tiles)
    def _():
        o_ref[...] = jnp.zeros(o_ref.shape, o_ref.dtype)


def _dispatch(x, gain, dest, n_tiles, tm, nt):
    t, d = x.shape
    return pl.pallas_call(
        functools.partial(_dispatch_kernel, tm=tm),
        out_shape=jax.ShapeDtypeStruct((nt * tm, d), BF16),
        grid_spec=pltpu.PrefetchScalarGridSpec(
            num_scalar_prefetch=2,
            grid=(nt,),
            in_specs=[pl.BlockSpec(memory_space=pl.ANY), pl.BlockSpec((1, d), lambda n, dst, ntl: (0, 0))],
            out_specs=pl.BlockSpec((tm, d), lambda n, dst, ntl: (n, 0)),
            scratch_shapes=[pltpu.SMEM((nt * tm,), jnp.int32), pltpu.VMEM((2, tm, d), F32),
                            pltpu.SemaphoreType.DMA((2,))]),
        compiler_params=_cparams(("arbitrary",)),
        name="moe_dispatch",
    )(dest, n_tiles, x, gain.reshape(1, d))


def _experts_kernel(ge_ref, gs_ref, gn_ref, nt_ref, xg_hbm, wg_ref, wu_ref, wd_ref, y_hbm,
                    xbuf, yacc, wgc, wuc, wdc, sem_in, sem_out, *, sub):
    s = pl.program_id(0)
    j = pl.program_id(1)
    n_sub = gn_ref[s]
    start = pl.multiple_of(gs_ref[s], sub)

    @pl.when(jnp.logical_and(s == 0, j == 0))
    def _():
        yacc[0:sub, :] = jnp.zeros((sub, yacc.shape[1]), F32)
        first, last = nt_ref[0], y_hbm.shape[0] // sub

        def put(r, c):
            pltpu.make_async_copy(yacc.at[pl.ds(0, sub), :],
                                  y_hbm.at[pl.ds(pl.multiple_of(r * sub, sub), sub), :], sem_out).start()
            return c

        def done(r, c):
            pltpu.make_async_copy(yacc.at[pl.ds(0, sub), :], y_hbm.at[pl.ds(0, sub), :], sem_out).wait()
            return c

        lax.fori_loop(first, last, put, 0)
        lax.fori_loop(first, last, done, 0)

    @pl.when(n_sub > 0)
    def _():
        @pl.when(j == 0)
        def _():
            def fetch(r, c):
                r0 = pl.multiple_of(r * sub, sub)
                pltpu.make_async_copy(xg_hbm.at[pl.ds(start + r0, sub), :], xbuf.at[pl.ds(r0, sub), :],
                                      sem_in).start()
                return c

            def fetched(r, c):
                pltpu.make_async_copy(xg_hbm.at[pl.ds(0, sub), :], xbuf.at[pl.ds(0, sub), :], sem_in).wait()
                return c

            lax.fori_loop(0, n_sub, fetch, 0)
            yacc[...] = jnp.zeros(yacc.shape, F32)
            lax.fori_loop(0, n_sub, fetched, 0)

        wgc[...] = wg_ref[0].astype(BF16)
        wuc[...] = wu_ref[0].astype(BF16)
        wdc[...] = wd_ref[0].astype(BF16)

        def rows_at(r0, m):
            x = xbuf[pl.ds(r0, m), :]
            g = jnp.dot(x, wgc[...], preferred_element_type=F32)
            u = jnp.dot(x, wuc[...], preferred_element_type=F32)
            a = (g * _sigmoid(g) * u).astype(BF16)
            yacc[pl.ds(r0, m), :] += jnp.dot(a, wdc[...], preferred_element_type=F32)

        def tile_pair(r, c):
            rows_at(pl.multiple_of(r * 2 * sub, 2 * sub), 2 * sub)
            return c

        lax.fori_loop(0, n_sub // 2, tile_pair, 0)

        @pl.when(n_sub % 2 == 1)
        def _():
            rows_at(pl.multiple_of((n_sub - 1) * sub, sub), sub)

        @pl.when(j == pl.num_programs(1) - 1)
        def _():
            def put(r, c):
                r0 = pl.multiple_of(r * sub, sub)
                pltpu.make_async_copy(yacc.at[pl.ds(r0, sub), :], y_hbm.at[pl.ds(start + r0, sub), :],
                                      sem_out).start()
                return c

            def done(r, c):
                pltpu.make_async_copy(yacc.at[pl.ds(0, sub), :], y_hbm.at[pl.ds(0, sub), :], sem_out).wait()
                return c

            lax.fori_loop(0, n_sub, put, 0)
            lax.fori_loop(0, n_sub, done, 0)


def _moe_experts(xg, w_gu, w_down, g_expert, g_start, g_tiles, n_tiles, *, sub, per_group, tf=256):
    p, d = xg.shape
    f = w_down.shape[1]
    nj = f // tf
    rows = sub * per_group
    n_groups = g_expert.shape[0]

    def jcl(s, j, gn):
        return jnp.where(gn[s] > 0, j, nj - 1)

    return pl.pallas_call(
        functools.partial(_experts_kernel, sub=sub),
        out_shape=jax.ShapeDtypeStruct((p, d), F32),
        grid_spec=pltpu.PrefetchScalarGridSpec(
            num_scalar_prefetch=4,
            grid=(n_groups, nj),
            in_specs=[pl.BlockSpec(memory_space=pl.ANY),
                      pl.BlockSpec((1, d, tf), lambda s, j, ge, gs, gn, nt: (ge[s], 0, jcl(s, j, gn))),
                      pl.BlockSpec((1, d, tf), lambda s, j, ge, gs, gn, nt: (ge[s], 0, jcl(s, j, gn) + nj)),
                      pl.BlockSpec((1, tf, d), lambda s, j, ge, gs, gn, nt: (ge[s], jcl(s, j, gn), 0))],
            out_specs=pl.BlockSpec(memory_space=pl.ANY),
            scratch_shapes=[pltpu.VMEM((rows, d), BF16), pltpu.VMEM((rows, d), F32),
                            pltpu.VMEM((d, tf), BF16), pltpu.VMEM((d, tf), BF16), pltpu.VMEM((tf, d), BF16),
                            pltpu.SemaphoreType.DMA(()), pltpu.SemaphoreType.DMA(())]),
        compiler_params=pltpu.CompilerParams(dimension_semantics=("arbitrary", "arbitrary"),
                                             vmem_limit_bytes=MOE_VMEM_LIMIT),
        name="moe_experts",
    )(g_expert, g_start, g_tiles, n_tiles, xg, w_gu, w_gu, w_down)


def _moe_combine_kernel(dest_ref, x_ref, rt_ref, y_hbm, o_ref, ybuf, sems, *, tm):
    i = pl.program_id(0)

    def gather(tile, slot):
        base = tile * tm

        def issue(r, c):
            for k in range(2):
                pltpu.make_async_copy(y_hbm.at[pl.ds(dest_ref[2 * (base + r) + k], 1), :],
                                      ybuf.at[slot, pl.ds(k * tm + r, 1), :], sems.at[slot]).start()
            return c

        lax.fori_loop(0, tm, issue, 0, unroll=4)

    @pl.when(i == 0)
    def _():
        gather(0, 0)

    slot = i % 2

    @pl.when(i + 1 < pl.num_programs(0))
    def _():
        gather(i + 1, 1 - slot)

    pltpu.make_async_copy(y_hbm.at[pl.ds(0, 2 * tm), :], ybuf.at[slot], sems.at[slot]).wait()
    w = rt_ref[...]
    o_ref[...] = (x_ref[...] + w[:, 2:3] * ybuf[slot, 0:tm, :] + w[:, 3:4] * ybuf[slot, tm:2 * tm, :])


def _moe_combine(x, rt, y, dest, tm=256):
    t, d = x.shape
    tm = min(tm, t)
    return pl.pallas_call(
        functools.partial(_moe_combine_kernel, tm=tm),
        out_shape=jax.ShapeDtypeStruct((t, d), F32),
        grid_spec=pltpu.PrefetchScalarGridSpec(
            num_scalar_prefetch=1,
            grid=(t // tm,),
            in_specs=[pl.BlockSpec((tm, d), lambda i, dst: (i, 0)),
                      pl.BlockSpec((tm, LANES), lambda i, dst: (i, 0)),
                      pl.BlockSpec(memory_space=pl.ANY)],
            out_specs=pl.BlockSpec((tm, d), lambda i, dst: (i, 0)),
            scratch_shapes=[pltpu.VMEM((2, 2 * tm, d), F32), pltpu.SemaphoreType.DMA((2,))]),
        compiler_params=_cparams(("arbitrary",)),
        name="moe_combine",
    )(dest, x, rt, y)


def _moe_block(xs, gain, router_w, router_b, w_gu, w_down):
    t = xs.shape[0]
    sub = min(MOE_SUB, t)
    nt = 2 * t // sub + N_EXPERTS
    rt = _router(xs, gain, router_w, router_b)
    dest, n_tiles, g_expert, g_start, g_tiles = _route_plan(rt, sub, MOE_TILES_PER_GROUP)
    xg = _dispatch(xs, gain, dest, n_tiles, sub, nt)
    y = _moe_experts(xg, w_gu, w_down, g_expert, g_start, g_tiles, n_tiles,
                     sub=sub, per_group=MOE_TILES_PER_GROUP)
    return _moe_combine(xs, rt, y, dest)


def _alibi_slopes():
    slopes = np.array([2.0 ** (-8.0 * (i + 1) / N_MIX_HEADS) for i in range(N_MIX_HEADS)], dtype=np.float32)
    return jnp.asarray((slopes.astype(np.float64) * LOG2E).astype(np.float32))


def _importance_map(ncp, nb):
    ratio = SLC_BLOCK // CMP_STRIDE
    c = np.arange(ncp)[:, None]
    b = np.arange(nb)[None, :]
    m = (c >= ratio * b - 1) & (c <= ratio * b + ratio - 1) & (c < ncp - 1)
    return jnp.asarray(m.astype(np.float32), dtype=BF16)


def _nsa_mix(p, slopes, q_gain, k_gain, cmp_w, cmp_pos, slc_tq=256):
    t = p.shape[0]
    qkv = _nsa_prep(p, q_gain, k_gain)
    kvc = _compress(p, cmp_w, cmp_pos, k_gain[0])
    imap = _importance_map(t // CMP_STRIDE, t // SLC_BLOCK)
    o_cmp, score = _cmp_attn(slopes, qkv, kvc, imap)
    sel, blk_act = _topk_select(score, sub=slc_tq)
    o_slc = _slc_attn(slopes, qkv, sel, blk_act[:, :, 0, :], tq=slc_tq)
    o_win = _win_attn(slopes, qkv)
    gates = p[:, Q_W + 6 * KV_W:Q_W + 6 * KV_W + GATE_W]
    return _gate_sum(o_cmp, o_slc, o_win, gates)


def kernel(x, mem, attn_norm, mem_norm, ffn_norm, w_in_a, cmp_w, cmp_pos, nsa_q_norm, nsa_k_norm, w_in_b,
           kv_norm_b, w_kv_b, w_mem_kv, mem_qk_norm, w_out, ffn_w_gu, ffn_w_down, router_w, router_b,
           moe_w_gu, moe_w_down):
    b, t, d = x.shape
    assert b == 1
    xs = x[0]
    mems = mem[0]
    slopes = _alibi_slopes()

    p = _norm_matmul(xs, attn_norm[0], w_in_a[0], tm=1024, tn=896, name="in_proj_a")
    mix = _nsa_mix(p, slopes, nsa_q_norm[0], nsa_k_norm[0], cmp_w[0], cmp_pos[0])
    memkv = _norm_matmul(mems, mem_norm[0], w_mem_kv[0], tm=256, tn=512, name="mem_kv")
    qm = p[:, Q_W + 6 * KV_W + GATE_W:]
    mo = _mem_attn(qm, 0, memkv, mem_qk_norm[0])
    xs = _out_proj(mix, mo, w_out[0], xs)
    act = _swiglu_up(xs, ffn_norm[0], ffn_w_gu[0])
    xs = _matmul(act, ffn_w_down[0], xs, tm=1024, tn=1024, tk=1024, name="ffn_down")

    kvb = _norm_matmul(xs, kv_norm_b, w_kv_b, tm=1024, tn=1024, out_dtype=BF16, name="kv_proj_b")
    pb = _norm_matmul(xs, attn_norm[1], w_in_b[0], tm=1024, tn=1024, name="in_proj_b")
    mix = _stick_attn(pb, kvb)
    memkv = _norm_matmul(mems, mem_norm[1], w_mem_kv[1], tm=256, tn=512, name="mem_kv")
    mo = _mem_attn(pb, N_MIX_HEADS, memkv, mem_qk_norm[1])
    xs = _out_proj(mix, mo, w_out[1], xs)
    xs = _moe_block(xs, ffn_norm[1], router_w[0], router_b[0], moe_w_gu[0], moe_w_down[0])
    return xs[None]
```

```python
import functools

import numpy as np
import jax
import jax.numpy as jnp
from jax import lax
from jax.experimental import pallas as pl
from jax.experimental.pallas import tpu as pltpu

HEAD_DIM = 128
N_MEM_HEADS = 4
N_MIX_HEADS = 12
N_KV_GROUPS = 3
HEADS_PER_GROUP = 4
CMP_STRIDE = 16
CMP_BLOCK = 32
SLC_BLOCK = 64
SLC_TOPK = 16
WINDOW = 512
N_EXPERTS = 8
Q_W = N_MIX_HEADS * HEAD_DIM
KV_W = N_KV_GROUPS * HEAD_DIM
GATE_W = 3 * N_MIX_HEADS
MEM_Q_W = N_MEM_HEADS * HEAD_DIM
NEG = -1e30
BIG = 1e6
EPS = 1e-6
SCALE = HEAD_DIM ** -0.5
LOG2E = 1.4426950408889634
STICK_DEAD = -151.0
LANES = 128
VMEM_LIMIT = 48 * 1024 * 1024
MOE_VMEM_LIMIT = 56 * 1024 * 1024
WIDE_VMEM_LIMIT = 56 * 1024 * 1024
MOE_SUB = 256
MOE_TILES_PER_GROUP = 9

F32 = jnp.float32
BF16 = jnp.bfloat16


def _cparams(sem, vmem=VMEM_LIMIT):
    return pltpu.CompilerParams(dimension_semantics=sem, vmem_limit_bytes=vmem)


def _nt_dot(a, b):
    return lax.dot_general(a, b, (((1,), (1,)), ((), ())), preferred_element_type=F32)


def _rms(x, gain):
    return x * lax.rsqrt(jnp.mean(x * x, axis=-1, keepdims=True) + EPS) * gain


def _sigmoid(x):
    return 1.0 / (1.0 + jnp.exp(-x))


def _rmsnorm_cast_kernel(x_ref, g_ref, o_ref):
    o_ref[...] = _rms(x_ref[...], g_ref[...]).astype(o_ref.dtype)


def _rmsnorm_cast(x, gain, tm=512):
    m, d = x.shape
    tm = min(tm, m)
    return pl.pallas_call(
        _rmsnorm_cast_kernel,
        out_shape=jax.ShapeDtypeStruct((m, d), BF16),
        grid=(m // tm,),
        in_specs=[pl.BlockSpec((tm, d), lambda i: (i, 0)), pl.BlockSpec((1, d), lambda i: (0, 0))],
        out_specs=pl.BlockSpec((tm, d), lambda i: (i, 0)),
        compiler_params=_cparams(("parallel",)),
        name="rmsnorm_cast",
    )(x, gain.reshape(1, d))


def _norm_matmul_kernel(x_ref, g_ref, w_ref, o_ref, xn_ref):
    @pl.when(pl.program_id(1) == 0)
    def _():
        xn_ref[...] = _rms(x_ref[...], g_ref[...]).astype(BF16)

    o_ref[...] = jnp.dot(xn_ref[...], w_ref[...].astype(BF16), preferred_element_type=F32).astype(o_ref.dtype)


def _norm_matmul(x, gain, w, *, tm, tn, out_dtype=F32, name="norm_matmul"):
    m, d = x.shape
    n = w.shape[1]
    tm = min(tm, m)
    return pl.pallas_call(
        _norm_matmul_kernel,
        out_shape=jax.ShapeDtypeStruct((m, n), out_dtype),
        grid=(m // tm, pl.cdiv(n, tn)),
        in_specs=[pl.BlockSpec((tm, d), lambda i, j: (i, 0)),
                  pl.BlockSpec((1, d), lambda i, j: (0, 0)),
                  pl.BlockSpec((d, tn), lambda i, j: (0, j))],
        out_specs=pl.BlockSpec((tm, tn), lambda i, j: (i, j)),
        scratch_shapes=[pltpu.VMEM((tm, d), BF16)],
        compiler_params=_cparams(("parallel", "arbitrary"), WIDE_VMEM_LIMIT),
        name=name,
    )(x, gain.reshape(1, d), w)


def _matmul_kernel(*refs, nk, has_res):
    if has_res:
        a_ref, w_ref, r_ref, o_ref = refs[:4]
    else:
        a_ref, w_ref, o_ref = refs[:3]
        r_ref = None
    part = jnp.dot(a_ref[...], w_ref[...].astype(BF16), preferred_element_type=F32)
    if nk == 1:
        if has_res:
            part = part + r_ref[...]
        o_ref[...] = part.astype(o_ref.dtype)
        return
    acc_ref = refs[-1]
    k = pl.program_id(2)

    @pl.when(k == 0)
    def _():
        acc_ref[...] = part

    @pl.when(k > 0)
    def _():
        acc_ref[...] += part

    @pl.when(k == nk - 1)
    def _():
        r = acc_ref[...]
        if has_res:
            r = r + r_ref[...]
        o_ref[...] = r.astype(o_ref.dtype)


def _matmul(a, w, res=None, *, tm, tn, tk=None, out_dtype=F32, name="matmul"):
    m, kdim = a.shape
    n = w.shape[1]
    tm = min(tm, m)
    tk = kdim if tk is None else tk
    nk = kdim // tk
    in_specs = [pl.BlockSpec((tm, tk), lambda i, j, k: (i, k)),
                pl.BlockSpec((tk, tn), lambda i, j, k: (k, j))]
    args = [a, w]
    if res is not None:
        in_specs.append(pl.BlockSpec((tm, tn), lambda i, j, k: (i, j)))
        args.append(res)
    scratch = [pltpu.VMEM((tm, tn), F32)] if nk > 1 else []
    return pl.pallas_call(
        functools.partial(_matmul_kernel, nk=nk, has_res=res is not None),
        out_shape=jax.ShapeDtypeStruct((m, n), out_dtype),
        grid=(m // tm, pl.cdiv(n, tn), nk),
        in_specs=in_specs,
        out_specs=pl.BlockSpec((tm, tn), lambda i, j, k: (i, j)),
        scratch_shapes=scratch,
        compiler_params=_cparams(("parallel", "parallel", "arbitrary")),
        name=name,
    )(*args)


def _out_proj_kernel(mix_ref, mo_ref, w1_ref, w2_ref, r_ref, o_ref):
    o_ref[...] = (jnp.dot(mix_ref[...], w1_ref[...].astype(BF16), preferred_element_type=F32)
                  + jnp.dot(mo_ref[...], w2_ref[...].astype(BF16), preferred_element_type=F32)
                  + r_ref[...])


def _out_proj(mix, mo, w, res, tm=1024, tn=1024):
    t = mix.shape[0]
    tm = min(tm, t)
    n = w.shape[1]
    assert Q_W % MEM_Q_W == 0
    return pl.pallas_call(
        _out_proj_kernel,
        out_shape=jax.ShapeDtypeStruct((t, n), F32),
        grid=(t // tm, n // tn),
        in_specs=[pl.BlockSpec((tm, Q_W), lambda i, j: (i, 0)),
                  pl.BlockSpec((tm, MEM_Q_W), lambda i, j: (i, 0)),
                  pl.BlockSpec((Q_W, tn), lambda i, j: (0, j)),
                  pl.BlockSpec((MEM_Q_W, tn), lambda i, j: (Q_W // MEM_Q_W, j)),
                  pl.BlockSpec((tm, tn), lambda i, j: (i, j))],
        out_specs=pl.BlockSpec((tm, tn), lambda i, j: (i, j)),
        compiler_params=_cparams(("parallel", "parallel")),
        name="out_proj",
    )(mix, mo, w, w, res)


def _swiglu_kernel(x_ref, gn_ref, wg_ref, wu_ref, o_ref, xn_ref):
    @pl.when(pl.program_id(1) == 0)
    def _():
        xn_ref[...] = _rms(x_ref[...], gn_ref[...]).astype(BF16)

    a = xn_ref[...]
    g = jnp.dot(a, wg_ref[...].astype(BF16), preferred_element_type=F32)
    u = jnp.dot(a, wu_ref[...].astype(BF16), preferred_element_type=F32)
    o_ref[...] = (g * _sigmoid(g) * u).astype(o_ref.dtype)


def _swiglu_up(x, gain, w_gu, *, tm=1024, tn=512):
    m, d = x.shape
    f = w_gu.shape[1] // 2
    tm = min(tm, m)
    nj = f // tn
    return pl.pallas_call(
        _swiglu_kernel,
        out_shape=jax.ShapeDtypeStruct((m, f), BF16),
        grid=(m // tm, nj),
        in_specs=[pl.BlockSpec((tm, d), lambda i, j: (i, 0)),
                  pl.BlockSpec((1, d), lambda i, j: (0, 0)),
                  pl.BlockSpec((d, tn), lambda i, j: (0, j)),
                  pl.BlockSpec((d, tn), lambda i, j: (0, j + nj))],
        out_specs=pl.BlockSpec((tm, tn), lambda i, j: (i, j)),
        scratch_shapes=[pltpu.VMEM((tm, d), BF16)],
        compiler_params=_cparams(("parallel", "arbitrary"), WIDE_VMEM_LIMIT),
        name="swiglu_up",
    )(x, gain.reshape(1, d), w_gu, w_gu)


def _nsa_prep_kernel(p_ref, gtab_ref, o_ref):
    for c in range((Q_W + 6 * KV_W) // HEAD_DIM):
        x = p_ref[:, c * HEAD_DIM:(c + 1) * HEAD_DIM]
        if c < N_MIX_HEADS:
            y = _rms(x, gtab_ref[0:1, :]) * (SCALE * LOG2E)
        elif 18 <= c < 21:
            y = _rms(x, gtab_ref[1:2, :])
        elif 24 <= c < 27:
            y = _rms(x, gtab_ref[2:3, :])
        else:
            y = x
        o_ref[:, c * HEAD_DIM:(c + 1) * HEAD_DIM] = y.astype(o_ref.dtype)


def _nsa_prep(p, q_gain, k_gain, tm=256):
    t = p.shape[0]
    tm = min(tm, t)
    w = Q_W + 6 * KV_W
    gtab = jnp.zeros((8, HEAD_DIM), F32).at[0].set(q_gain).at[1].set(k_gain[1]).at[2].set(k_gain[2])
    return pl.pallas_call(
        _nsa_prep_kernel,
        out_shape=jax.ShapeDtypeStruct((t, w), BF16),
        grid=(t // tm,),
        in_specs=[pl.BlockSpec((tm, w), lambda i: (i, 0)), pl.BlockSpec((8, HEAD_DIM), lambda i: (0, 0))],
        out_specs=pl.BlockSpec((tm, w), lambda i: (i, 0)),
        compiler_params=_cparams(("parallel",)),
        name="nsa_prep",
    )(p, gtab)


def _compress_kernel(p_ref, pos_ref, w_ref, g_ref, o_ref):
    ncp = o_ref.shape[2]
    first = jnp.zeros((ncp, HEAD_DIM), F32)
    second = jnp.zeros((ncp, HEAD_DIM), F32)
    for r in range(CMP_STRIDE):
        rows = p_ref[pl.ds(r, ncp, stride=CMP_STRIDE), :]
        lo = (rows + pos_ref[0, r:r + 1, :]).astype(BF16)
        hi = (rows + pos_ref[0, CMP_STRIDE + r:CMP_STRIDE + r + 1, :]).astype(BF16)
        first += jnp.dot(lo, w_ref[0, r * HEAD_DIM:(r + 1) * HEAD_DIM, :].astype(BF16),
                         preferred_element_type=F32)
        second += jnp.dot(hi, w_ref[0, (CMP_STRIDE + r) * HEAD_DIM:(CMP_STRIDE + r + 1) * HEAD_DIM, :].astype(BF16),
                          preferred_element_type=F32)
    y = first + jnp.concatenate([second[1:], jnp.zeros((1, HEAD_DIM), F32)], axis=0)
    is_k = pl.program_id(0) == 0
    o_ref[0, 0] = jnp.where(is_k, _rms(y, g_ref[...]), y).astype(o_ref.dtype)


def _compress(p, cmp_w, cmp_pos, k_gain0):
    t = p.shape[0]
    ncp = t // CMP_STRIDE
    col0 = Q_W // HEAD_DIM
    return pl.pallas_call(
        _compress_kernel,
        out_shape=jax.ShapeDtypeStruct((2, N_KV_GROUPS, ncp, HEAD_DIM), BF16),
        grid=(2, N_KV_GROUPS),
        in_specs=[pl.BlockSpec((t, HEAD_DIM), lambda s, gi: (0, col0 + N_KV_GROUPS * s + gi)),
                  pl.BlockSpec((1, CMP_BLOCK, HEAD_DIM), lambda s, gi: (s, 0, 0)),
                  pl.BlockSpec((1, CMP_BLOCK * HEAD_DIM, HEAD_DIM), lambda s, gi: (s, 0, 0)),
                  pl.BlockSpec((1, HEAD_DIM), lambda s, gi: (0, 0))],
        out_specs=pl.BlockSpec((1, 1, ncp, HEAD_DIM), lambda s, gi: (s, gi, 0, 0)),
        compiler_params=_cparams(("parallel", "parallel")),
        name="nsa_compress",
    )(p, cmp_pos, cmp_w, k_gain0.reshape(1, HEAD_DIM))


def _cmp_kernel(slopes_ref, q_ref, k_ref, v_ref, imap_ref, o_ref, score_ref, *, tq, ncp):
    g = pl.program_id(0)
    t0 = pl.program_id(1) * tq
    row = lax.broadcasted_iota(jnp.int32, (tq, ncp), 0) + t0
    col = lax.broadcasted_iota(jnp.int32, (tq, ncp), 1)
    c_end = col * CMP_STRIDE + (CMP_BLOCK - 1)
    neg = jnp.where(row >= c_end, 0.0, NEG)
    colb = (lax.broadcasted_iota(jnp.int32, (1, ncp), 1) * CMP_STRIDE + (CMP_BLOCK - 1) - t0).astype(F32)
    has_key = (lax.broadcasted_iota(jnp.int32, (tq, 1), 0) + t0 >= CMP_BLOCK - 1).astype(F32)
    k = k_ref[0, 0]
    v = v_ref[0, 0]
    psum = jnp.zeros((tq, ncp), F32)
    for p in range(HEADS_PER_GROUP):
        slope = slopes_ref[g * HEADS_PER_GROUP + p]
        q = q_ref[:, p * HEAD_DIM:(p + 1) * HEAD_DIM]
        s = _nt_dot(q, k) + slope * colb + neg
        e = jnp.exp2(s - jnp.max(s, axis=1, keepdims=True))
        pr = e * (has_key / jnp.sum(e, axis=1, keepdims=True))
        psum = psum + pr
        o_ref[:, p * HEAD_DIM:(p + 1) * HEAD_DIM] = jnp.dot(
            pr.astype(BF16), v, preferred_element_type=F32).astype(o_ref.dtype)

    imap = imap_ref[...]
    hi = psum.astype(BF16)
    r1 = psum - hi.astype(F32)
    mid = r1.astype(BF16)
    lo = (r1 - mid.astype(F32)).astype(BF16)
    imp = _nt_dot(imap, hi) + _nt_dot(imap, mid) + _nt_dot(imap, lo)
    nb = imp.shape[0]
    blk = lax.broadcasted_iota(jnp.int32, (nb, tq), 0)
    t = lax.broadcasted_iota(jnp.int32, (nb, tq), 1) + t0
    score_ref[0] = jnp.where(blk == t // SLC_BLOCK, 2 * BIG,
                             jnp.where(blk == 0, BIG,
                                       jnp.where(blk * SLC_BLOCK <= t, imp, -BIG)))


def _cmp_attn(slopes, qkv, kvc, imap, tq=256):
    t = qkv.shape[0]
    tq = min(tq, t)
    ncp = kvc.shape[2]
    nb = t // SLC_BLOCK
    gw = HEADS_PER_GROUP * HEAD_DIM
    return pl.pallas_call(
        functools.partial(_cmp_kernel, tq=tq, ncp=ncp),
        out_shape=(jax.ShapeDtypeStruct((t, Q_W), F32),
                   jax.ShapeDtypeStruct((N_KV_GROUPS, nb, t), F32)),
        grid_spec=pltpu.PrefetchScalarGridSpec(
            num_scalar_prefetch=1,
            grid=(N_KV_GROUPS, t // tq),
            in_specs=[pl.BlockSpec((tq, gw), lambda g, i, s: (i, g)),
                      pl.BlockSpec((1, 1, ncp, HEAD_DIM), lambda g, i, s: (0, g, 0, 0)),
                      pl.BlockSpec((1, 1, ncp, HEAD_DIM), lambda g, i, s: (1, g, 0, 0)),
                      pl.BlockSpec((nb, ncp), lambda g, i, s: (0, 0))],
            out_specs=(pl.BlockSpec((tq, gw), lambda g, i, s: (i, g)),
                       pl.BlockSpec((1, nb, tq), lambda g, i, s: (g, 0, i)))),
        compiler_params=_cparams(("parallel", "parallel")),
        name="nsa_cmp",
    )(slopes, qkv, kvc, kvc, imap)


def _topk_kernel(score_ref, sel_ref, act_ref, *, tm, sub):
    score = score_ref[0]
    nb = score.shape[0]
    blk_f = lax.broadcasted_iota(jnp.int32, (nb, tm), 0).astype(F32)

    def pick_one(_, carry):
        work, sel = carry
        m = jnp.max(work, axis=0, keepdims=True)
        first = jnp.min(jnp.where(work == m, blk_f, float(nb)), axis=0, keepdims=True)
        pick = blk_f == first
        return jnp.where(pick, -3e38, work), jnp.where(pick, 1.0, sel)

    forced = score >= 0.5 * BIG
    start = (jnp.where(forced, -3e38, score), jnp.where(forced, 1.0, 0.0))
    _, sel_t = lax.fori_loop(0, max(min(SLC_TOPK, nb) - 2, 0), pick_one, start)
    sel = sel_t.T
    sel_ref[0] = sel.astype(sel_ref.dtype)
    for a in range(tm // sub):
        act_ref[0, a] = jnp.broadcast_to(jnp.max(sel[a * sub:(a + 1) * sub], axis=0, keepdims=True),
                                         act_ref.shape[2:])


def _topk_select(score, sub, tm=1024):
    g, nb, t = score.shape
    tm = min(tm, t)
    sub = min(sub, tm)
    return pl.pallas_call(
        functools.partial(_topk_kernel, tm=tm, sub=sub),
        out_shape=(jax.ShapeDtypeStruct((g, t, nb), BF16),
                   jax.ShapeDtypeStruct((g, t // sub, 8, nb), F32)),
        grid=(g, t // tm),
        in_specs=[pl.BlockSpec((1, nb, tm), lambda gi, i: (gi, 0, i))],
        out_specs=(pl.BlockSpec((1, tm, nb), lambda gi, i: (gi, i, 0)),
                   pl.BlockSpec((1, tm // sub, 8, nb), lambda gi, i: (gi, i, 0, 0))),
        compiler_params=_cparams(("parallel", "parallel")),
        name="nsa_topk",
    )(score)


def _slc_kernel(slopes_ref, flags_ref, q_ref, k_ref, v_ref, sel_ref, o_ref, m_ref, l_ref, acc_ref, *, tq, tk):
    g = pl.program_id(0)
    i = pl.program_id(1)
    t0 = i * tq
    flag_base = (g * pl.num_programs(1) + i) * (k_ref.shape[0] // tk)
    nb = sel_ref.shape[2]
    m_ref[...] = jnp.full(m_ref.shape, NEG, F32)
    l_ref[...] = jnp.zeros(l_ref.shape, F32)
    acc_ref[...] = jnp.zeros(acc_ref.shape, F32)
    sel = sel_ref[0]
    bpt = tk // SLC_BLOCK

    def body(j, carry):
        @pl.when(flags_ref[flag_base + j] > 0)
        def _():
            tile(j, False)
        return carry

    def tile(j, diagonal):
        k0 = pl.multiple_of(j * tk, tk)
        k = k_ref[pl.ds(k0, tk), :]
        v = v_ref[pl.ds(k0, tk), :]
        eb = lax.broadcasted_iota(jnp.int32, (nb, tk), 0)
        ec = lax.broadcasted_iota(jnp.int32, (nb, tk), 1)
        expand = jnp.where(eb == ec // SLC_BLOCK + j * bpt, 1.0, 0.0).astype(BF16)
        selx = jnp.dot(sel, expand, preferred_element_type=F32)
        if diagonal:
            later = (lax.broadcasted_iota(jnp.int32, (tq, tk), 0) + (t0 - k0)
                     < lax.broadcasted_iota(jnp.int32, (tq, tk), 1))
            selx = jnp.where(later, 0.0, selx)
        neg = jnp.where(selx > 0.5, 0.0, NEG)
        colb = (lax.broadcasted_iota(jnp.int32, (1, tk), 1) + (k0 - t0)).astype(F32)
        for p in range(HEADS_PER_GROUP):
            slope = slopes_ref[g * HEADS_PER_GROUP + p]
            q = q_ref[:, p * HEAD_DIM:(p + 1) * HEAD_DIM]
            s = _nt_dot(q, k) + slope * colb + neg
            m_old = m_ref[p]
            m_new = jnp.maximum(m_old, jnp.max(s, axis=1, keepdims=True))
            alpha = jnp.exp2(m_old - m_new)
            e = jnp.exp2(s - jnp.tile(m_new, (1, tk // LANES)))
            l_ref[p] = alpha * l_ref[p] + jnp.sum(e, axis=1, keepdims=True)
            acc_ref[p] = alpha * acc_ref[p] + jnp.dot(e.astype(BF16), v, preferred_element_type=F32)
            m_ref[p] = m_new

    own = t0 // tk
    lax.fori_loop(0, own, body, 0)
    tile(own, True)
    for p in range(HEADS_PER_GROUP):
        o_ref[:, p * HEAD_DIM:(p + 1) * HEAD_DIM] = (acc_ref[p] / l_ref[p]).astype(o_ref.dtype)


def _slc_attn(slopes, qkv, sel, blk_act, tq=256, tk=256):
    t = qkv.shape[0]
    tq = min(tq, t)
    tk = min(tk, t)
    nb = sel.shape[2]
    assert blk_act.shape[1] == t // tq and tk % tq == 0
    gw = HEADS_PER_GROUP * HEAD_DIM
    ks_col = (Q_W + 2 * KV_W) // HEAD_DIM
    vs_col = (Q_W + 3 * KV_W) // HEAD_DIM
    flags = jnp.max(blk_act.reshape(N_KV_GROUPS, t // tq, t // tk, tk // SLC_BLOCK), axis=-1)
    flags = (flags > 0.5).astype(jnp.int32).reshape(-1)
    return pl.pallas_call(
        functools.partial(_slc_kernel, tq=tq, tk=tk),
        out_shape=jax.ShapeDtypeStruct((t, Q_W), F32),
        grid_spec=pltpu.PrefetchScalarGridSpec(
            num_scalar_prefetch=2,
            grid=(N_KV_GROUPS, t // tq),
            in_specs=[pl.BlockSpec((tq, gw), lambda g, i, s, f: (i, g)),
                      pl.BlockSpec((t, HEAD_DIM), lambda g, i, s, f: (0, ks_col + g)),
                      pl.BlockSpec((t, HEAD_DIM), lambda g, i, s, f: (0, vs_col + g)),
                      pl.BlockSpec((1, tq, nb), lambda g, i, s, f: (g, i, 0))],
            out_specs=pl.BlockSpec((tq, gw), lambda g, i, s, f: (i, g)),
            scratch_shapes=[pltpu.VMEM((HEADS_PER_GROUP, tq, LANES), F32),
                            pltpu.VMEM((HEADS_PER_GROUP, tq, LANES), F32),
                            pltpu.VMEM((HEADS_PER_GROUP, tq, HEAD_DIM), F32)]),
        compiler_params=_cparams(("parallel", "parallel")),
        name="nsa_selected",
    )(slopes, flags, qkv, qkv, qkv, sel)


def _win_kernel(slopes_ref, q_ref, k0_ref, k1_ref, k2_ref, v0_ref, v1_ref, v2_ref, o_ref, *, tq):
    g = pl.program_id(0)
    i = pl.program_id(1)
    nprev = WINDOW // tq
    kw = (nprev + 1) * tq
    k = jnp.concatenate([k0_ref[...], k1_ref[...], k2_ref[...]], axis=0)
    v = jnp.concatenate([v0_ref[...], v1_ref[...], v2_ref[...]], axis=0)
    r = lax.broadcasted_iota(jnp.int32, (tq, kw), 0)
    c = lax.broadcasted_iota(jnp.int32, (tq, kw), 1)
    rel = WINDOW + r - c
    first_valid = (nprev - i) * tq
    inside = jnp.where(rel >= 0, jnp.where(rel < WINDOW, 1.0, 0.0), 0.0)
    neg = jnp.where(jnp.where(c >= first_valid, inside, 0.0) > 0.5, 0.0, NEG)
    colb = (lax.broadcasted_iota(jnp.int32, (1, kw), 1) - WINDOW).astype(F32)
    for p in range(HEADS_PER_GROUP):
        slope = slopes_ref[g * HEADS_PER_GROUP + p]
        q = q_ref[:, p * HEAD_DIM:(p + 1) * HEAD_DIM]
        s = _nt_dot(q, k) + slope * colb + neg
        e = jnp.exp2(s - jnp.max(s, axis=1, keepdims=True))
        o = jnp.dot(e.astype(BF16), v, preferred_element_type=F32) / jnp.sum(e, axis=1, keepdims=True)
        o_ref[:, p * HEAD_DIM:(p + 1) * HEAD_DIM] = o.astype(o_ref.dtype)


def _win_attn(slopes, qkv, tq=256):
    t = qkv.shape[0]
    assert WINDOW % tq == 0 and WINDOW // tq == 2
    gw = HEADS_PER_GROUP * HEAD_DIM
    kw_col = (Q_W + 4 * KV_W) // HEAD_DIM
    vw_col = (Q_W + 5 * KV_W) // HEAD_DIM

    def kv_spec(col, back):
        return pl.BlockSpec((tq, HEAD_DIM), lambda g, i, s: (jnp.maximum(i - back, 0), col + g))

    return pl.pallas_call(
        functools.partial(_win_kernel, tq=tq),
        out_shape=jax.ShapeDtypeStruct((t, Q_W), F32),
        grid_spec=pltpu.PrefetchScalarGridSpec(
            num_scalar_prefetch=1,
            grid=(N_KV_GROUPS, t // tq),
            in_specs=[pl.BlockSpec((tq, gw), lambda g, i, s: (i, g)),
                      kv_spec(kw_col, 2), kv_spec(kw_col, 1), kv_spec(kw_col, 0),
                      kv_spec(vw_col, 2), kv_spec(vw_col, 1), kv_spec(vw_col, 0)],
            out_specs=pl.BlockSpec((tq, gw), lambda g, i, s: (i, g))),
        compiler_params=_cparams(("parallel", "parallel")),
        name="nsa_window",
    )(slopes, qkv, qkv, qkv, qkv, qkv, qkv, qkv)


def _gate_sum_kernel(oc_ref, os_ref, ow_ref, g_ref, o_ref):
    sg = _sigmoid(g_ref[...])
    for h in range(N_MIX_HEADS):
        sl = slice(h * HEAD_DIM, (h + 1) * HEAD_DIM)
        o = (oc_ref[:, sl] * sg[:, 3 * h:3 * h + 1]
             + os_ref[:, sl] * sg[:, 3 * h + 1:3 * h + 2]
             + ow_ref[:, sl] * sg[:, 3 * h + 2:3 * h + 3])
        o_ref[:, sl] = o.astype(o_ref.dtype)


def _gate_sum(o_cmp, o_slc, o_win, gates, tm=512):
    t = o_cmp.shape[0]
    tm = min(tm, t)
    big = pl.BlockSpec((tm, Q_W), lambda i: (i, 0))
    return pl.pallas_call(
        _gate_sum_kernel,
        out_shape=jax.ShapeDtypeStruct((t, Q_W), BF16),
        grid=(t // tm,),
        in_specs=[big, big, big, pl.BlockSpec((tm, GATE_W), lambda i: (i, 0))],
        out_specs=big,
        compiler_params=_cparams(("parallel",)),
        name="nsa_gate_sum",
    )(o_cmp, o_slc, o_win, gates)


def _mem_attn_kernel(q_ref, k_ref, v_ref, g_ref, o_ref):
    q = (_rms(q_ref[...], g_ref[0:1, :]) * SCALE).astype(BF16)
    k = _rms(k_ref[...], g_ref[1:2, :]).astype(BF16)
    s = _nt_dot(q, k)
    e = jnp.exp(s - jnp.max(s, axis=1, keepdims=True))
    p = e * (1.0 / jnp.sum(e, axis=1, keepdims=True))
    o_ref[...] = jnp.dot(p.astype(BF16), v_ref[...].astype(BF16),
                         preferred_element_type=F32).astype(o_ref.dtype)


def _mem_attn(qsrc, q_col, memkv, qk_gain, tq=2048):
    t = qsrc.shape[0]
    tq = min(tq, t)
    nm = memkv.shape[0]
    gtab = jnp.zeros((8, HEAD_DIM), F32).at[0:2].set(qk_gain)
    return pl.pallas_call(
        _mem_attn_kernel,
        out_shape=jax.ShapeDtypeStruct((t, MEM_Q_W), BF16),
        grid=(N_MEM_HEADS, t // tq),
        in_specs=[pl.BlockSpec((tq, HEAD_DIM), lambda h, i: (i, q_col + h)),
                  pl.BlockSpec((nm, HEAD_DIM), lambda h, i: (0, h)),
                  pl.BlockSpec((nm, HEAD_DIM), lambda h, i: (0, N_MEM_HEADS + h)),
                  pl.BlockSpec((8, HEAD_DIM), lambda h, i: (0, 0))],
        out_specs=pl.BlockSpec((tq, HEAD_DIM), lambda h, i: (i, h)),
        compiler_params=_cparams(("parallel", "parallel")),
        name="mem_attn",
    )(qsrc, memkv, memkv, gtab)


def _stick_kernel(q_ref, k_ref, v_ref, o_ref, *, tq, hp):
    i = pl.program_id(1)
    r = lax.broadcasted_iota(jnp.int32, (tq, tq), 0)
    c = lax.broadcasted_iota(jnp.int32, (tq, tq), 1)
    later = jnp.where(r > c, 1.0, 0.0).astype(BF16)
    qs = [(q_ref[:, h * HEAD_DIM:(h + 1) * HEAD_DIM] * (SCALE * LOG2E)).astype(BF16) for h in range(hp)]

    def tile(h, k0, tail, acc, diagonal):
        k = k_ref[pl.ds(k0, tq), h * HEAD_DIM:(h + 1) * HEAD_DIM]
        v = v_ref[pl.ds(k0, tq), h * HEAD_DIM:(h + 1) * HEAD_DIM]
        z = _nt_dot(qs[h], k)
        sp = jnp.maximum(z, 0.0) + jnp.log2(1.0 + jnp.exp2(-jnp.abs(z)))
        log_keep = jnp.where(r > c, -sp, 0.0) if diagonal else -sp
        hi = log_keep.astype(BF16)
        lo = (log_keep - hi.astype(F32)).astype(BF16)
        after = (jnp.dot(hi, later, preferred_element_type=F32)
                 + jnp.dot(lo, later, preferred_element_type=F32) + tail)
        a = jnp.exp2(z - sp + after)
        if diagonal:
            a = jnp.where(r > c, a, 0.0)
        acc = acc + jnp.dot(a.astype(BF16), v, preferred_element_type=F32)
        return tail + jnp.sum(log_keep, axis=1, keepdims=True), acc

    k_diag = pl.multiple_of(i * tq, tq)
    state = [tile(h, k_diag, jnp.zeros((tq, 1), F32), jnp.zeros((tq, HEAD_DIM), F32), True) for h in range(hp)]

    def live(st):
        return jnp.logical_and(st[0] <= i, st[1] > 0)

    def body(st):
        k0 = pl.multiple_of((i - st[0]) * tq, tq)
        new = [tile(h, k0, st[2 + 2 * h], st[3 + 2 * h], False) for h in range(hp)]
        top = functools.reduce(jnp.maximum, [jnp.max(tl) for tl, _ in new])
        return (st[0] + 1, (top > STICK_DEAD).astype(jnp.int32)) + tuple(x for pair in new for x in pair)

    st = lax.while_loop(live, body, (jnp.int32(1), jnp.int32(1)) + tuple(x for pair in state for x in pair))
    for h in range(hp):
        o_ref[:, h * HEAD_DIM:(h + 1) * HEAD_DIM] = st[3 + 2 * h].astype(o_ref.dtype)


def _stick_attn(pb, kvb, tq=256, hp=2):
    t = pb.shape[0]
    tq = min(tq, t)
    hw = hp * HEAD_DIM
    return pl.pallas_call(
        functools.partial(_stick_kernel, tq=tq, hp=hp),
        out_shape=jax.ShapeDtypeStruct((t, Q_W), BF16),
        grid=(N_MIX_HEADS // hp, t // tq),
        in_specs=[pl.BlockSpec((tq, hw), lambda h, i: (i, h)),
                  pl.BlockSpec((t, hw), lambda h, i: (0, h)),
                  pl.BlockSpec((t, hw), lambda h, i: (0, N_MIX_HEADS // hp + h))],
        out_specs=pl.BlockSpec((tq, hw), lambda h, i: (i, h)),
        compiler_params=_cparams(("parallel", "parallel")),
        name="stick_breaking",
    )(pb, kvb, kvb)


def _router_kernel(x_ref, g_ref, w_ref, b_ref, c_ref):
    h = _rms(x_ref[...], g_ref[...])
    logits = jnp.dot(h, w_ref[...], preferred_element_type=F32, precision=lax.Precision.HIGHEST)
    lane = lax.broadcasted_iota(jnp.int32, logits.shape, 1).astype(F32)
    biased = jnp.where(lane < N_EXPERTS, logits + b_ref[...], -3e38)
    m1 = jnp.max(biased, axis=1, keepdims=True)
    i1 = jnp.min(jnp.where(biased == m1, lane, float(LANES)), axis=1, keepdims=True)
    rest = jnp.where(lane == i1, -3e38, biased)
    m2 = jnp.max(rest, axis=1, keepdims=True)
    i2 = jnp.min(jnp.where(rest == m2, lane, float(LANES)), axis=1, keepdims=True)
    l1 = jnp.sum(jnp.where(lane == i1, logits, 0.0), axis=1, keepdims=True)
    l2 = jnp.sum(jnp.where(lane == i2, logits, 0.0), axis=1, keepdims=True)
    mx = jnp.maximum(l1, l2)
    e1 = jnp.exp(l1 - mx)
    e2 = jnp.exp(l2 - mx)
    inv = 1.0 / (e1 + e2)
    c_ref[...] = (jnp.where(lane == 0.0, i1, 0.0) + jnp.where(lane == 1.0, i2, 0.0)
                  + jnp.where(lane == 2.0, e1 * inv, 0.0) + jnp.where(lane == 3.0, e2 * inv, 0.0))


def _router(x, gain, router_w, router_b, tm=512):
    t, d = x.shape
    tm = min(tm, t)
    w = jnp.zeros((d, LANES), F32).at[:, :N_EXPERTS].set(router_w)
    b = jnp.zeros((1, LANES), F32).at[0, :N_EXPERTS].set(router_b)
    return pl.pallas_call(
        _router_kernel,
        out_shape=jax.ShapeDtypeStruct((t, LANES), F32),
        grid=(t // tm,),
        in_specs=[pl.BlockSpec((tm, d), lambda i: (i, 0)), pl.BlockSpec((1, d), lambda i: (0, 0)),
                  pl.BlockSpec((d, LANES), lambda i: (0, 0)), pl.BlockSpec((1, LANES), lambda i: (0, 0))],
        out_specs=pl.BlockSpec((tm, LANES), lambda i: (i, 0)),
        compiler_params=_cparams(("parallel",)),
        name="moe_router",
    )(x, gain.reshape(1, d), w, b)


def _route_plan(rt, sub, per_group):
    t = rt.shape[0]
    i32 = jnp.int32
    e_flat = rt[:, 0:2].astype(i32).reshape(-1)
    onehot = (e_flat[:, None] == jnp.arange(N_EXPERTS, dtype=i32)[None, :]).astype(i32)
    csum = jnp.cumsum(onehot, axis=0)
    rank = jnp.sum(csum * onehot, axis=1) - 1
    tiles_e = (csum[-1] + sub - 1) // sub
    tile_end = jnp.cumsum(tiles_e)
    tile_start = tile_end - tiles_e
    dest = tile_start[e_flat] * sub + rank
    groups_e = (tiles_e + per_group - 1) // per_group
    group_end = jnp.cumsum(groups_e)
    n_groups = group_end[-1]
    max_groups = (2 * t // sub + N_EXPERTS) // per_group + N_EXPERTS
    gidx = jnp.arange(max_groups, dtype=i32)
    gcl = jnp.minimum(gidx, n_groups - 1)
    g_expert = jnp.sum((gcl[:, None] >= group_end[None, :]).astype(i32), axis=1)
    k_in_e = gcl - (group_end - groups_e)[g_expert]
    g_start = (tile_start[g_expert] + k_in_e * per_group) * sub
    g_tiles = jnp.where(gidx < n_groups, jnp.clip(tiles_e[g_expert] - k_in_e * per_group, 0, per_group), 0)
    return (dest.astype(i32), tile_end[-1].reshape(1).astype(i32),
            g_expert.astype(i32), g_start.astype(i32), g_tiles.astype(i32))


def _dispatch_kernel(dest_ref, nt_ref, x_hbm, g_ref, o_ref, tok_ref, xbuf, sems, *, tm):
    n = pl.program_id(0)
    n_tiles = nt_ref[0]

    def gather(tile, slot):
        base = tile * tm

        def issue(r, c):
            pltpu.make_async_copy(x_hbm.at[pl.ds(tok_ref[base + r], 1), :],
                                  xbuf.at[slot, pl.ds(r, 1), :], sems.at[slot]).start()
            return c

        lax.fori_loop(0, tm, issue, 0, unroll=8)

    @pl.when(n == 0)
    def _():
        def clear(r, c):
            tok_ref[r] = 0
            return c

        lax.fori_loop(0, tok_ref.shape[0], clear, 0, unroll=8)

        def fill(q, c):
            rows = [dest_ref[q * 8 + u] for u in range(8)]
            for u in range(8):
                tok_ref[rows[u]] = q * 4 + u // 2
            return c

        lax.fori_loop(0, dest_ref.shape[0] // 8, fill, 0)
        gather(0, 0)

    slot = n % 2

    @pl.when(n + 1 < n_tiles)
    def _():
        gather(n + 1, 1 - slot)

    @pl.when(n < n_tiles)
    def _():
        pltpu.make_async_copy(x_hbm.at[pl.ds(0, tm), :], xbuf.at[slot], sems.at[slot]).wait()
        o_ref[...] = _rms(xbuf[slot], g_ref[...]).astype(o_ref.dtype)

    @pl.when(n >= n_tiles)
    def _():
        o_ref[...] = jnp.zeros(o_ref.shape, o_ref.dtype)


def _dispatch(x, gain, dest, n_tiles, tm, nt):
    t, d = x.shape
    return pl.pallas_call(
        functools.partial(_dispatch_kernel, tm=tm),
        out_shape=jax.ShapeDtypeStruct((nt * tm, d), BF16),
        grid_spec=pltpu.PrefetchScalarGridSpec(
            num_scalar_prefetch=2,
            grid=(nt,),
            in_specs=[pl.BlockSpec(memory_space=pl.ANY), pl.BlockSpec((1, d), lambda n, dst, ntl: (0, 0))],
            out_specs=pl.BlockSpec((tm, d), lambda n, dst, ntl: (n, 0)),
            scratch_shapes=[pltpu.SMEM((nt * tm,), jnp.int32), pltpu.VMEM((2, tm, d), F32),
                            pltpu.SemaphoreType.DMA((2,))]),
        compiler_params=_cparams(("arbitrary",)),
        name="moe_dispatch",
    )(dest, n_tiles, x, gain.reshape(1, d))


def _experts_kernel(ge_ref, gs_ref, gn_ref, nt_ref, xg_hbm, wg_ref, wu_ref, wd_ref, y_hbm,
                    xbuf, yacc, wgc, wuc, wdc, sem_in, sem_out, *, sub):
    s = pl.program_id(0)
    j = pl.program_id(1)
    n_sub = gn_ref[s]
    start = pl.multiple_of(gs_ref[s], sub)

    @pl.when(jnp.logical_and(s == 0, j == 0))
    def _():
        yacc[0:sub, :] = jnp.zeros((sub, yacc.shape[1]), F32)
        first, last = nt_ref[0], y_hbm.shape[0] // sub

        def put(r, c):
            pltpu.make_async_copy(yacc.at[pl.ds(0, sub), :],
                                  y_hbm.at[pl.ds(pl.multiple_of(r * sub, sub), sub), :], sem_out).start()
            return c

        def done(r, c):
            pltpu.make_async_copy(yacc.at[pl.ds(0, sub), :], y_hbm.at[pl.ds(0, sub), :], sem_out).wait()
            return c

        lax.fori_loop(first, last, put, 0)
        lax.fori_loop(first, last, done, 0)

    @pl.when(n_sub > 0)
    def _():
        @pl.when(j == 0)
        def _():
            def fetch(r, c):
                r0 = pl.multiple_of(r * sub, sub)
                pltpu.make_async_copy(xg_hbm.at[pl.ds(start + r0, sub), :], xbuf.at[pl.ds(r0, sub), :],
                                      sem_in).start()
                return c

            def fetched(r, c):
                pltpu.make_async_copy(xg_hbm.at[pl.ds(0, sub), :], xbuf.at[pl.ds(0, sub), :], sem_in).wait()
                return c

            lax.fori_loop(0, n_sub, fetch, 0)
            yacc[...] = jnp.zeros(yacc.shape, F32)
            lax.fori_loop(0, n_sub, fetched, 0)

        wgc[...] = wg_ref[0].astype(BF16)
        wuc[...] = wu_ref[0].astype(BF16)
        wdc[...] = wd_ref[0].astype(BF16)

        def rows_at(r0, m):
            x = xbuf[pl.ds(r0, m), :]
            g = jnp.dot(x, wgc[...], preferred_element_type=F32)
            u = jnp.dot(x, wuc[...], preferred_element_type=F32)
            a = (g * _sigmoid(g) * u).astype(BF16)
            yacc[pl.ds(r0, m), :] += jnp.dot(a, wdc[...], preferred_element_type=F32)

        def tile_pair(r, c):
            rows_at(pl.multiple_of(r * 2 * sub, 2 * sub), 2 * sub)
            return c

        lax.fori_loop(0, n_sub // 2, tile_pair, 0)

        @pl.when(n_sub % 2 == 1)
        def _():
            rows_at(pl.multiple_of((n_sub - 1) * sub, sub), sub)

        @pl.when(j == pl.num_programs(1) - 1)
        def _():
            def put(r, c):
                r0 = pl.multiple_of(r * sub, sub)
                pltpu.make_async_copy(yacc.at[pl.ds(r0, sub), :], y_hbm.at[pl.ds(start + r0, sub), :],
                                      sem_out).start()
                return c

            def done(r, c):
                pltpu.make_async_copy(yacc.at[pl.ds(0, sub), :], y_hbm.at[pl.ds(0, sub), :], sem_out).wait()
                return c

            lax.fori_loop(0, n_sub, put, 0)
            lax.fori_loop(0, n_sub, done, 0)


def _moe_experts(xg, w_gu, w_down, g_expert, g_start, g_tiles, n_tiles, *, sub, per_group, tf=256):
    p, d = xg.shape
    f = w_down.shape[1]
    nj = f // tf
    rows = sub * per_group
    n_groups = g_expert.shape[0]

    def jcl(s, j, gn):
        return jnp.where(gn[s] > 0, j, nj - 1)

    return pl.pallas_call(
        functools.partial(_experts_kernel, sub=sub),
        out_shape=jax.ShapeDtypeStruct((p, d), F32),
        grid_spec=pltpu.PrefetchScalarGridSpec(
            num_scalar_prefetch=4,
            grid=(n_groups, nj),
            in_specs=[pl.BlockSpec(memory_space=pl.ANY),
                      pl.BlockSpec((1, d, tf), lambda s, j, ge, gs, gn, nt: (ge[s], 0, jcl(s, j, gn))),
                      pl.BlockSpec((1, d, tf), lambda s, j, ge, gs, gn, nt: (ge[s], 0, jcl(s, j, gn) + nj)),
                      pl.BlockSpec((1, tf, d), lambda s, j, ge, gs, gn, nt: (ge[s], jcl(s, j, gn), 0))],
            out_specs=pl.BlockSpec(memory_space=pl.ANY),
            scratch_shapes=[pltpu.VMEM((rows, d), BF16), pltpu.VMEM((rows, d), F32),
                            pltpu.VMEM((d, tf), BF16), pltpu.VMEM((d, tf), BF16), pltpu.VMEM((tf, d), BF16),
                            pltpu.SemaphoreType.DMA(()), pltpu.SemaphoreType.DMA(())]),
        compiler_params=pltpu.CompilerParams(dimension_semantics=("arbitrary", "arbitrary"),
                                             vmem_limit_bytes=MOE_VMEM_LIMIT),
        name="moe_experts",
    )(g_expert, g_start, g_tiles, n_tiles, xg, w_gu, w_gu, w_down)


def _moe_combine_kernel(dest_ref, x_ref, rt_ref, y_hbm, o_ref, ybuf, sems, *, tm):
    i = pl.program_id(0)

    def gather(tile, slot):
        base = tile * tm

        def issue(r, c):
            for k in range(2):
                pltpu.make_async_copy(y_hbm.at[pl.ds(dest_ref[2 * (base + r) + k], 1), :],
                                      ybuf.at[slot, pl.ds(k * tm + r, 1), :], sems.at[slot]).start()
            return c

        lax.fori_loop(0, tm, issue, 0, unroll=4)

    @pl.when(i == 0)
    def _():
        gather(0, 0)

    slot = i % 2

    @pl.when(i + 1 < pl.num_programs(0))
    def _():
        gather(i + 1, 1 - slot)

    pltpu.make_async_copy(y_hbm.at[pl.ds(0, 2 * tm), :], ybuf.at[slot], sems.at[slot]).wait()
    w = rt_ref[...]
    o_ref[...] = (x_ref[...] + w[:, 2:3] * ybuf[slot, 0:tm, :] + w[:, 3:4] * ybuf[slot, tm:2 * tm, :])


def _moe_combine(x, rt, y, dest, tm=512):
    t, d = x.shape
    tm = min(tm, t)
    return pl.pallas_call(
        functools.partial(_moe_combine_kernel, tm=tm),
        out_shape=jax.ShapeDtypeStruct((t, d), F32),
        grid_spec=pltpu.PrefetchScalarGridSpec(
            num_scalar_prefetch=1,
            grid=(t // tm,),
            in_specs=[pl.BlockSpec((tm, d), lambda i, dst: (i, 0)),
                      pl.BlockSpec((tm, LANES), lambda i, dst: (i, 0)),
                      pl.BlockSpec(memory_space=pl.ANY)],
            out_specs=pl.BlockSpec((tm, d), lambda i, dst: (i, 0)),
            scratch_shapes=[pltpu.VMEM((2, 2 * tm, d), F32), pltpu.SemaphoreType.DMA((2,))]),
        compiler_params=_cparams(("arbitrary",)),
        name="moe_combine",
    )(dest, x, rt, y)


def _moe_block(xs, gain, router_w, router_b, w_gu, w_down):
    t = xs.shape[0]
    sub = min(MOE_SUB, t)
    nt = 2 * t // sub + N_EXPERTS
    rt = _router(xs, gain, router_w, router_b)
    dest, n_tiles, g_expert, g_start, g_tiles = _route_plan(rt, sub, MOE_TILES_PER_GROUP)
    xg = _dispatch(xs, gain, dest, (n_tiles + 1) // 2, 2 * sub, nt // 2)
    y = _moe_experts(xg, w_gu, w_down, g_expert, g_start, g_tiles, n_tiles,
                     sub=sub, per_group=MOE_TILES_PER_GROUP)
    return _moe_combine(xs, rt, y, dest)


def _alibi_slopes():
    slopes = np.array([2.0 ** (-8.0 * (i + 1) / N_MIX_HEADS) for i in range(N_MIX_HEADS)], dtype=np.float32)
    return jnp.asarray((slopes.astype(np.float64) * LOG2E).astype(np.float32))


def _importance_map(ncp, nb):
    ratio = SLC_BLOCK // CMP_STRIDE
    c = np.arange(ncp)[:, None]
    b = np.arange(nb)[None, :]
    m = (c >= ratio * b - 1) & (c <= ratio * b + ratio - 1) & (c < ncp - 1)
    return jnp.asarray(m.T.astype(np.float32), dtype=BF16)


def _nsa_mix(p, slopes, q_gain, k_gain, cmp_w, cmp_pos, slc_tq=256):
    t = p.shape[0]
    qkv = _nsa_prep(p, q_gain, k_gain)
    kvc = _compress(p, cmp_w, cmp_pos, k_gain[0])
    imap = _importance_map(t // CMP_STRIDE, t // SLC_BLOCK)
    o_cmp, score = _cmp_attn(slopes, qkv, kvc, imap)
    sel, blk_act = _topk_select(score, sub=slc_tq)
    o_slc = _slc_attn(slopes, qkv, sel, blk_act[:, :, 0, :], tq=slc_tq)
    o_win = _win_attn(slopes, qkv)
    gates = p[:, Q_W + 6 * KV_W:Q_W + 6 * KV_W + GATE_W]
    return _gate_sum(o_cmp, o_slc, o_win, gates)


def kernel(x, mem, attn_norm, mem_norm, ffn_norm, w_in_a, cmp_w, cmp_pos, nsa_q_norm, nsa_k_norm, w_in_b,
           kv_norm_b, w_kv_b, w_mem_kv, mem_qk_norm, w_out, ffn_w_gu, ffn_w_down, router_w, router_b,
           moe_w_gu, moe_w_down):
    b, t, d = x.shape
    assert b == 1
    xs = x[0]
    mems = mem[0]
    slopes = _alibi_slopes()

    p = _norm_matmul(xs, attn_norm[0], w_in_a[0], tm=1024, tn=896, name="in_proj_a")
    mix = _nsa_mix(p, slopes, nsa_q_norm[0], nsa_k_norm[0], cmp_w[0], cmp_pos[0])
    memkv = _norm_matmul(mems, mem_norm[0], w_mem_kv[0], tm=256, tn=512, name="mem_kv")
    qm = p[:, Q_W + 6 * KV_W + GATE_W:]
    mo = _mem_attn(qm, 0, memkv, mem_qk_norm[0])
    xs = _out_proj(mix, mo, w_out[0], xs)
    act = _swiglu_up(xs, ffn_norm[0], ffn_w_gu[0])
    xs = _matmul(act, ffn_w_down[0], xs, tm=1024, tn=1024, tk=1024, name="ffn_down")

    kvb = _norm_matmul(xs, kv_norm_b, w_kv_b, tm=1024, tn=1024, out_dtype=BF16, name="kv_proj_b")
    pb = _norm_matmul(xs, attn_norm[1], w_in_b[0], tm=1024, tn=1024, name="in_proj_b")
    mix = _stick_attn(pb, kvb)
    memkv = _norm_matmul(mems, mem_norm[1], w_mem_kv[1], tm=256, tn=512, name="mem_kv")
    mo = _mem_attn(pb, N_MIX_HEADS, memkv, mem_qk_norm[1])
    xs = _out_proj(mix, mo, w_out[1], xs)
    xs = _moe_block(xs, ffn_norm[1], router_w[0], router_b[0], moe_w_gu[0], moe_w_down[0])
    return xs[None]
```

```python
import functools

import numpy as np
import jax
import jax.numpy as jnp
from jax import lax
from jax.experimental import pallas as pl
from jax.experimental.pallas import tpu as pltpu

HEAD_DIM = 128
N_MEM_HEADS = 4
N_MIX_HEADS = 12
N_KV_GROUPS = 3
HEADS_PER_GROUP = 4
CMP_STRIDE = 16
CMP_BLOCK = 32
SLC_BLOCK = 64
SLC_TOPK = 16
WINDOW = 512
N_EXPERTS = 8
Q_W = N_MIX_HEADS * HEAD_DIM
KV_W = N_KV_GROUPS * HEAD_DIM
GATE_W = 3 * N_MIX_HEADS
MEM_Q_W = N_MEM_HEADS * HEAD_DIM
NEG = -1e30
BIG = 1e6
EPS = 1e-6
SCALE = HEAD_DIM ** -0.5
LOG2E = 1.4426950408889634
STICK_DEAD = -151.0
LANES = 128
VMEM_LIMIT = 48 * 1024 * 1024
MOE_VMEM_LIMIT = 56 * 1024 * 1024
WIDE_VMEM_LIMIT = 56 * 1024 * 1024
MOE_SUB = 256
MOE_TILES_PER_GROUP = 9

F32 = jnp.float32
BF16 = jnp.bfloat16


def _cparams(sem, vmem=VMEM_LIMIT):
    return pltpu.CompilerParams(dimension_semantics=sem, vmem_limit_bytes=vmem)


def _nt_dot(a, b):
    return lax.dot_general(a, b, (((1,), (1,)), ((), ())), preferred_element_type=F32)


def _rms(x, gain):
    return x * lax.rsqrt(jnp.mean(x * x, axis=-1, keepdims=True) + EPS) * gain


def _sigmoid(x):
    return 1.0 / (1.0 + jnp.exp(-x))


def _rmsnorm_cast_kernel(x_ref, g_ref, o_ref):
    o_ref[...] = _rms(x_ref[...], g_ref[...]).astype(o_ref.dtype)


def _rmsnorm_cast(x, gain, tm=512):
    m, d = x.shape
    tm = min(tm, m)
    return pl.pallas_call(
        _rmsnorm_cast_kernel,
        out_shape=jax.ShapeDtypeStruct((m, d), BF16),
        grid=(m // tm,),
        in_specs=[pl.BlockSpec((tm, d), lambda i: (i, 0)), pl.BlockSpec((1, d), lambda i: (0, 0))],
        out_specs=pl.BlockSpec((tm, d), lambda i: (i, 0)),
        compiler_params=_cparams(("parallel",)),
        name="rmsnorm_cast",
    )(x, gain.reshape(1, d))


def _norm_matmul_kernel(x_ref, g_ref, w_ref, o_ref, xn_ref):
    @pl.when(pl.program_id(1) == 0)
    def _():
        xn_ref[...] = _rms(x_ref[...], g_ref[...]).astype(BF16)

    o_ref[...] = jnp.dot(xn_ref[...], w_ref[...].astype(BF16), preferred_element_type=F32).astype(o_ref.dtype)


def _norm_matmul(x, gain, w, *, tm, tn, out_dtype=F32, name="norm_matmul"):
    m, d = x.shape
    n = w.shape[1]
    tm = min(tm, m)
    return pl.pallas_call(
        _norm_matmul_kernel,
        out_shape=jax.ShapeDtypeStruct((m, n), out_dtype),
        grid=(m // tm, pl.cdiv(n, tn)),
        in_specs=[pl.BlockSpec((tm, d), lambda i, j: (i, 0)),
                  pl.BlockSpec((1, d), lambda i, j: (0, 0)),
                  pl.BlockSpec((d, tn), lambda i, j: (0, j))],
        out_specs=pl.BlockSpec((tm, tn), lambda i, j: (i, j)),
        scratch_shapes=[pltpu.VMEM((tm, d), BF16)],
        compiler_params=_cparams(("parallel", "arbitrary"), WIDE_VMEM_LIMIT),
        name=name,
    )(x, gain.reshape(1, d), w)


def _matmul_kernel(*refs, nk, has_res):
    if has_res:
        a_ref, w_ref, r_ref, o_ref = refs[:4]
    else:
        a_ref, w_ref, o_ref = refs[:3]
        r_ref = None
    part = jnp.dot(a_ref[...], w_ref[...].astype(BF16), preferred_element_type=F32)
    if nk == 1:
        if has_res:
            part = part + r_ref[...]
        o_ref[...] = part.astype(o_ref.dtype)
        return
    acc_ref = refs[-1]
    k = pl.program_id(2)

    @pl.when(k == 0)
    def _():
        acc_ref[...] = part

    @pl.when(k > 0)
    def _():
        acc_ref[...] += part

    @pl.when(k == nk - 1)
    def _():
        r = acc_ref[...]
        if has_res:
            r = r + r_ref[...]
        o_ref[...] = r.astype(o_ref.dtype)


def _matmul(a, w, res=None, *, tm, tn, tk=None, out_dtype=F32, name="matmul"):
    m, kdim = a.shape
    n = w.shape[1]
    tm = min(tm, m)
    tk = kdim if tk is None else tk
    nk = kdim // tk
    in_specs = [pl.BlockSpec((tm, tk), lambda i, j, k: (i, k)),
                pl.BlockSpec((tk, tn), lambda i, j, k: (k, j))]
    args = [a, w]
    if res is not None:
        in_specs.append(pl.BlockSpec((tm, tn), lambda i, j, k: (i, j)))
        args.append(res)
    scratch = [pltpu.VMEM((tm, tn), F32)] if nk > 1 else []
    return pl.pallas_call(
        functools.partial(_matmul_kernel, nk=nk, has_res=res is not None),
        out_shape=jax.ShapeDtypeStruct((m, n), out_dtype),
        grid=(m // tm, pl.cdiv(n, tn), nk),
        in_specs=in_specs,
        out_specs=pl.BlockSpec((tm, tn), lambda i, j, k: (i, j)),
        scratch_shapes=scratch,
        compiler_params=_cparams(("parallel", "parallel", "arbitrary")),
        name=name,
    )(*args)


def _out_proj_kernel(mix_ref, mo_ref, w1_ref, w2_ref, r_ref, o_ref):
    o_ref[...] = (jnp.dot(mix_ref[...], w1_ref[...].astype(BF16), preferred_element_type=F32)
                  + jnp.dot(mo_ref[...], w2_ref[...].astype(BF16), preferred_element_type=F32)
                  + r_ref[...])


def _out_proj(mix, mo, w, res, tm=1024, tn=1024):
    t = mix.shape[0]
    tm = min(tm, t)
    n = w.shape[1]
    assert Q_W % MEM_Q_W == 0
    return pl.pallas_call(
        _out_proj_kernel,
        out_shape=jax.ShapeDtypeStruct((t, n), F32),
        grid=(t // tm, n // tn),
        in_specs=[pl.BlockSpec((tm, Q_W), lambda i, j: (i, 0)),
                  pl.BlockSpec((tm, MEM_Q_W), lambda i, j: (i, 0)),
                  pl.BlockSpec((Q_W, tn), lambda i, j: (0, j)),
                  pl.BlockSpec((MEM_Q_W, tn), lambda i, j: (Q_W // MEM_Q_W, j)),
                  pl.BlockSpec((tm, tn), lambda i, j: (i, j))],
        out_specs=pl.BlockSpec((tm, tn), lambda i, j: (i, j)),
        compiler_params=_cparams(("parallel", "parallel")),
        name="out_proj",
    )(mix, mo, w, w, res)


def _swiglu_kernel(x_ref, gn_ref, wg_ref, wu_ref, o_ref, xn_ref):
    @pl.when(pl.program_id(1) == 0)
    def _():
        xn_ref[...] = _rms(x_ref[...], gn_ref[...]).astype(BF16)

    a = xn_ref[...]
    g = jnp.dot(a, wg_ref[...].astype(BF16), preferred_element_type=F32)
    u = jnp.dot(a, wu_ref[...].astype(BF16), preferred_element_type=F32)
    o_ref[...] = (g * _sigmoid(g) * u).astype(o_ref.dtype)


def _swiglu_up(x, gain, w_gu, *, tm=1024, tn=512):
    m, d = x.shape
    f = w_gu.shape[1] // 2
    tm = min(tm, m)
    nj = f // tn
    return pl.pallas_call(
        _swiglu_kernel,
        out_shape=jax.ShapeDtypeStruct((m, f), BF16),
        grid=(m // tm, nj),
        in_specs=[pl.BlockSpec((tm, d), lambda i, j: (i, 0)),
                  pl.BlockSpec((1, d), lambda i, j: (0, 0)),
                  pl.BlockSpec((d, tn), lambda i, j: (0, j)),
                  pl.BlockSpec((d, tn), lambda i, j: (0, j + nj))],
        out_specs=pl.BlockSpec((tm, tn), lambda i, j: (i, j)),
        scratch_shapes=[pltpu.VMEM((tm, d), BF16)],
        compiler_params=_cparams(("parallel", "arbitrary"), WIDE_VMEM_LIMIT),
        name="swiglu_up",
    )(x, gain.reshape(1, d), w_gu, w_gu)


def _nsa_prep_kernel(p_ref, gtab_ref, o_ref):
    for c in range((Q_W + 6 * KV_W) // HEAD_DIM):
        x = p_ref[:, c * HEAD_DIM:(c + 1) * HEAD_DIM]
        if c < N_MIX_HEADS:
            y = _rms(x, gtab_ref[0:1, :]) * (SCALE * LOG2E)
        elif 18 <= c < 21:
            y = _rms(x, gtab_ref[1:2, :])
        elif 24 <= c < 27:
            y = _rms(x, gtab_ref[2:3, :])
        else:
            y = x
        o_ref[:, c * HEAD_DIM:(c + 1) * HEAD_DIM] = y.astype(o_ref.dtype)


def _nsa_prep(p, q_gain, k_gain, tm=256):
    t = p.shape[0]
    tm = min(tm, t)
    w = Q_W + 6 * KV_W
    gtab = jnp.zeros((8, HEAD_DIM), F32).at[0].set(q_gain).at[1].set(k_gain[1]).at[2].set(k_gain[2])
    return pl.pallas_call(
        _nsa_prep_kernel,
        out_shape=jax.ShapeDtypeStruct((t, w), BF16),
        grid=(t // tm,),
        in_specs=[pl.BlockSpec((tm, w), lambda i: (i, 0)), pl.BlockSpec((8, HEAD_DIM), lambda i: (0, 0))],
        out_specs=pl.BlockSpec((tm, w), lambda i: (i, 0)),
        compiler_params=_cparams(("parallel",)),
        name="nsa_prep",
    )(p, gtab)


def _compress_kernel(p_ref, pos_ref, w_ref, g_ref, o_ref):
    ncp = o_ref.shape[2]
    first = jnp.zeros((ncp, HEAD_DIM), F32)
    second = jnp.zeros((ncp, HEAD_DIM), F32)
    for r in range(CMP_STRIDE):
        rows = p_ref[pl.ds(r, ncp, stride=CMP_STRIDE), :]
        lo = (rows + pos_ref[0, r:r + 1, :]).astype(BF16)
        hi = (rows + pos_ref[0, CMP_STRIDE + r:CMP_STRIDE + r + 1, :]).astype(BF16)
        first += jnp.dot(lo, w_ref[0, r * HEAD_DIM:(r + 1) * HEAD_DIM, :].astype(BF16),
                         preferred_element_type=F32)
        second += jnp.dot(hi, w_ref[0, (CMP_STRIDE + r) * HEAD_DIM:(CMP_STRIDE + r + 1) * HEAD_DIM, :].astype(BF16),
                          preferred_element_type=F32)
    y = first + jnp.concatenate([second[1:], jnp.zeros((1, HEAD_DIM), F32)], axis=0)
    is_k = pl.program_id(0) == 0
    o_ref[0, 0] = jnp.where(is_k, _rms(y, g_ref[...]), y).astype(o_ref.dtype)


def _compress(p, cmp_w, cmp_pos, k_gain0):
    t = p.shape[0]
    ncp = t // CMP_STRIDE
    col0 = Q_W // HEAD_DIM
    return pl.pallas_call(
        _compress_kernel,
        out_shape=jax.ShapeDtypeStruct((2, N_KV_GROUPS, ncp, HEAD_DIM), BF16),
        grid=(2, N_KV_GROUPS),
        in_specs=[pl.BlockSpec((t, HEAD_DIM), lambda s, gi: (0, col0 + N_KV_GROUPS * s + gi)),
                  pl.BlockSpec((1, CMP_BLOCK, HEAD_DIM), lambda s, gi: (s, 0, 0)),
                  pl.BlockSpec((1, CMP_BLOCK * HEAD_DIM, HEAD_DIM), lambda s, gi: (s, 0, 0)),
                  pl.BlockSpec((1, HEAD_DIM), lambda s, gi: (0, 0))],
        out_specs=pl.BlockSpec((1, 1, ncp, HEAD_DIM), lambda s, gi: (s, gi, 0, 0)),
        compiler_params=_cparams(("parallel", "parallel")),
        name="nsa_compress",
    )(p, cmp_pos, cmp_w, k_gain0.reshape(1, HEAD_DIM))


def _cmp_kernel(slopes_ref, q_ref, k_ref, v_ref, imap_ref, o_ref, score_ref, *, tq, ncp):
    g = pl.program_id(0)
    t0 = pl.program_id(1) * tq
    row = lax.broadcasted_iota(jnp.int32, (tq, ncp), 0) + t0
    col = lax.broadcasted_iota(jnp.int32, (tq, ncp), 1)
    c_end = col * CMP_STRIDE + (CMP_BLOCK - 1)
    neg = jnp.where(row >= c_end, 0.0, NEG)
    colb = (lax.broadcasted_iota(jnp.int32, (1, ncp), 1) * CMP_STRIDE + (CMP_BLOCK - 1) - t0).astype(F32)
    has_key = (lax.broadcasted_iota(jnp.int32, (tq, 1), 0) + t0 >= CMP_BLOCK - 1).astype(F32)
    k = k_ref[0, 0]
    v = v_ref[0, 0]
    psum = jnp.zeros((tq, ncp), F32)
    for p in range(HEADS_PER_GROUP):
        slope = slopes_ref[g * HEADS_PER_GROUP + p]
        q = q_ref[:, p * HEAD_DIM:(p + 1) * HEAD_DIM]
        s = _nt_dot(q, k) + slope * colb + neg
        e = jnp.exp2(s - jnp.max(s, axis=1, keepdims=True))
        pr = e * (has_key / jnp.sum(e, axis=1, keepdims=True))
        psum = psum + pr
        o_ref[:, p * HEAD_DIM:(p + 1) * HEAD_DIM] = jnp.dot(
            pr.astype(BF16), v, preferred_element_type=F32).astype(o_ref.dtype)

    imap = imap_ref[...]
    hi = psum.astype(BF16)
    r1 = psum - hi.astype(F32)
    mid = r1.astype(BF16)
    lo = (r1 - mid.astype(F32)).astype(BF16)
    imp = _nt_dot(imap, hi) + _nt_dot(imap, mid) + _nt_dot(imap, lo)
    nb = imp.shape[0]
    blk = lax.broadcasted_iota(jnp.int32, (nb, tq), 0)
    t = lax.broadcasted_iota(jnp.int32, (nb, tq), 1) + t0
    score_ref[0] = jnp.where(blk == t // SLC_BLOCK, 2 * BIG,
                             jnp.where(blk == 0, BIG,
                                       jnp.where(blk * SLC_BLOCK <= t, imp, -BIG)))


def _cmp_attn(slopes, qkv, kvc, imap, tq=256):
    t = qkv.shape[0]
    tq = min(tq, t)
    ncp = kvc.shape[2]
    nb = t // SLC_BLOCK
    gw = HEADS_PER_GROUP * HEAD_DIM
    return pl.pallas_call(
        functools.partial(_cmp_kernel, tq=tq, ncp=ncp),
        out_shape=(jax.ShapeDtypeStruct((t, Q_W), F32),
                   jax.ShapeDtypeStruct((N_KV_GROUPS, nb, t), F32)),
        grid_spec=pltpu.PrefetchScalarGridSpec(
            num_scalar_prefetch=1,
            grid=(N_KV_GROUPS, t // tq),
            in_specs=[pl.BlockSpec((tq, gw), lambda g, i, s: (i, g)),
                      pl.BlockSpec((1, 1, ncp, HEAD_DIM), lambda g, i, s: (0, g, 0, 0)),
                      pl.BlockSpec((1, 1, ncp, HEAD_DIM), lambda g, i, s: (1, g, 0, 0)),
                      pl.BlockSpec((nb, ncp), lambda g, i, s: (0, 0))],
            out_specs=(pl.BlockSpec((tq, gw), lambda g, i, s: (i, g)),
                       pl.BlockSpec((1, nb, tq), lambda g, i, s: (g, 0, i)))),
        compiler_params=_cparams(("parallel", "parallel")),
        name="nsa_cmp",
    )(slopes, qkv, kvc, kvc, imap)


def _topk_kernel(score_ref, sel_ref, act_ref, *, tm, sub):
    score = score_ref[0]
    nb = score.shape[0]
    blk_f = lax.broadcasted_iota(jnp.int32, (nb, tm), 0).astype(F32)

    def pick_one(_, carry):
        work, sel = carry
        m = jnp.max(work, axis=0, keepdims=True)
        first = jnp.min(jnp.where(work == m, blk_f, float(nb)), axis=0, keepdims=True)
        pick = blk_f == first
        return jnp.where(pick, -3e38, work), jnp.where(pick, 1.0, sel)

    forced = score >= 0.5 * BIG
    start = (jnp.where(forced, -3e38, score), jnp.where(forced, 1.0, 0.0))
    _, sel_t = lax.fori_loop(0, max(min(SLC_TOPK, nb) - 2, 0), pick_one, start)
    sel = sel_t.T
    sel_ref[0] = sel.astype(sel_ref.dtype)
    for a in range(tm // sub):
        act_ref[0, a] = jnp.broadcast_to(jnp.max(sel[a * sub:(a + 1) * sub], axis=0, keepdims=True),
                                         act_ref.shape[2:])


def _topk_select(score, sub, tm=1024):
    g, nb, t = score.shape
    tm = min(tm, t)
    sub = min(sub, tm)
    return pl.pallas_call(
        functools.partial(_topk_kernel, tm=tm, sub=sub),
        out_shape=(jax.ShapeDtypeStruct((g, t, nb), BF16),
                   jax.ShapeDtypeStruct((g, t // sub, 8, nb), F32)),
        grid=(g, t // tm),
        in_specs=[pl.BlockSpec((1, nb, tm), lambda gi, i: (gi, 0, i))],
        out_specs=(pl.BlockSpec((1, tm, nb), lambda gi, i: (gi, i, 0)),
                   pl.BlockSpec((1, tm // sub, 8, nb), lambda gi, i: (gi, i, 0, 0))),
        compiler_params=_cparams(("parallel", "parallel")),
        name="nsa_topk",
    )(score)


def _slc_kernel(slopes_ref, flags_ref, q_ref, k_ref, v_ref, sel_ref, o_ref, m_ref, l_ref, acc_ref, *, tq, tk):
    g = pl.program_id(0)
    i = pl.program_id(1)
    t0 = i * tq
    flag_base = (g * pl.num_programs(1) + i) * (k_ref.shape[0] // tk)
    nb = sel_ref.shape[2]
    m_ref[...] = jnp.full(m_ref.shape, NEG, F32)
    l_ref[...] = jnp.zeros(l_ref.shape, F32)
    acc_ref[...] = jnp.zeros(acc_ref.shape, F32)
    sel = sel_ref[0]
    bpt = tk // SLC_BLOCK

    def body(j, carry):
        @pl.when(flags_ref[flag_base + j] > 0)
        def _():
            tile(j, False)
        return carry

    def tile(j, diagonal):
        k0 = pl.multiple_of(j * tk, tk)
        k = k_ref[pl.ds(k0, tk), :]
        v = v_ref[pl.ds(k0, tk), :]
        eb = lax.broadcasted_iota(jnp.int32, (nb, tk), 0)
        ec = lax.broadcasted_iota(jnp.int32, (nb, tk), 1)
        expand = jnp.where(eb == ec // SLC_BLOCK + j * bpt, 1.0, 0.0).astype(BF16)
        selx = jnp.dot(sel, expand, preferred_element_type=F32)
        if diagonal:
            later = (lax.broadcasted_iota(jnp.int32, (tq, tk), 0) + (t0 - k0)
                     < lax.broadcasted_iota(jnp.int32, (tq, tk), 1))
            selx = jnp.where(later, 0.0, selx)
        neg = jnp.where(selx > 0.5, 0.0, NEG)
        colb = (lax.broadcasted_iota(jnp.int32, (1, tk), 1) + (k0 - t0)).astype(F32)
        for p in range(HEADS_PER_GROUP):
            slope = slopes_ref[g * HEADS_PER_GROUP + p]
            q = q_ref[:, p * HEAD_DIM:(p + 1) * HEAD_DIM]
            s = _nt_dot(q, k) + slope * colb + neg
            m_old = m_ref[p]
            m_new = jnp.maximum(m_old, jnp.max(s, axis=1, keepdims=True))
            alpha = jnp.exp2(m_old - m_new)
            e = jnp.exp2(s - jnp.tile(m_new, (1, tk // LANES)))
            l_ref[p] = alpha * l_ref[p] + jnp.sum(e, axis=1, keepdims=True)
            acc_ref[p] = alpha * acc_ref[p] + jnp.dot(e.astype(BF16), v, preferred_element_type=F32)
            m_ref[p] = m_new

    own = t0 // tk
    lax.fori_loop(0, own, body, 0)
    tile(own, True)
    for p in range(HEADS_PER_GROUP):
        o_ref[:, p * HEAD_DIM:(p + 1) * HEAD_DIM] = (acc_ref[p] / l_ref[p]).astype(o_ref.dtype)


def _slc_attn(slopes, qkv, sel, blk_act, tq=256, tk=256):
    t = qkv.shape[0]
    tq = min(tq, t)
    tk = min(tk, t)
    nb = sel.shape[2]
    assert blk_act.shape[1] == t // tq and tk % tq == 0
    gw = HEADS_PER_GROUP * HEAD_DIM
    ks_col = (Q_W + 2 * KV_W) // HEAD_DIM
    vs_col = (Q_W + 3 * KV_W) // HEAD_DIM
    flags = jnp.max(blk_act.reshape(N_KV_GROUPS, t // tq, t // tk, tk // SLC_BLOCK), axis=-1)
    flags = (flags > 0.5).astype(jnp.int32).reshape(-1)
    return pl.pallas_call(
        functools.partial(_slc_kernel, tq=tq, tk=tk),
        out_shape=jax.ShapeDtypeStruct((t, Q_W), F32),
        grid_spec=pltpu.PrefetchScalarGridSpec(
            num_scalar_prefetch=2,
            grid=(N_KV_GROUPS, t // tq),
            in_specs=[pl.BlockSpec((tq, gw), lambda g, i, s, f: (i, g)),
                      pl.BlockSpec((t, HEAD_DIM), lambda g, i, s, f: (0, ks_col + g)),
                      pl.BlockSpec((t, HEAD_DIM), lambda g, i, s, f: (0, vs_col + g)),
                      pl.BlockSpec((1, tq, nb), lambda g, i, s, f: (g, i, 0))],
            out_specs=pl.BlockSpec((tq, gw), lambda g, i, s, f: (i, g)),
            scratch_shapes=[pltpu.VMEM((HEADS_PER_GROUP, tq, LANES), F32),
                            pltpu.VMEM((HEADS_PER_GROUP, tq, LANES), F32),
                            pltpu.VMEM((HEADS_PER_GROUP, tq, HEAD_DIM), F32)]),
        compiler_params=_cparams(("parallel", "parallel")),
        name="nsa_selected",
    )(slopes, flags, qkv, qkv, qkv, sel)


def _win_kernel(slopes_ref, q_ref, k0_ref, k1_ref, k2_ref, v0_ref, v1_ref, v2_ref, o_ref, *, tq):
    g = pl.program_id(0)
    i = pl.program_id(1)
    nprev = WINDOW // tq
    kw = (nprev + 1) * tq
    k = jnp.concatenate([k0_ref[...], k1_ref[...], k2_ref[...]], axis=0)
    v = jnp.concatenate([v0_ref[...], v1_ref[...], v2_ref[...]], axis=0)
    r = lax.broadcasted_iota(jnp.int32, (tq, kw), 0)
    c = lax.broadcasted_iota(jnp.int32, (tq, kw), 1)
    rel = WINDOW + r - c
    first_valid = (nprev - i) * tq
    inside = jnp.where(rel >= 0, jnp.where(rel < WINDOW, 1.0, 0.0), 0.0)
    neg = jnp.where(jnp.where(c >= first_valid, inside, 0.0) > 0.5, 0.0, NEG)
    colb = (lax.broadcasted_iota(jnp.int32, (1, kw), 1) - WINDOW).astype(F32)
    for p in range(HEADS_PER_GROUP):
        slope = slopes_ref[g * HEADS_PER_GROUP + p]
        q = q_ref[:, p * HEAD_DIM:(p + 1) * HEAD_DIM]
        s = _nt_dot(q, k) + slope * colb + neg
        e = jnp.exp2(s - jnp.max(s, axis=1, keepdims=True))
        o = jnp.dot(e.astype(BF16), v, preferred_element_type=F32) / jnp.sum(e, axis=1, keepdims=True)
        o_ref[:, p * HEAD_DIM:(p + 1) * HEAD_DIM] = o.astype(o_ref.dtype)


def _win_attn(slopes, qkv, tq=256):
    t = qkv.shape[0]
    assert WINDOW % tq == 0 and WINDOW // tq == 2
    gw = HEADS_PER_GROUP * HEAD_DIM
    kw_col = (Q_W + 4 * KV_W) // HEAD_DIM
    vw_col = (Q_W + 5 * KV_W) // HEAD_DIM

    def kv_spec(col, back):
        return pl.BlockSpec((tq, HEAD_DIM), lambda g, i, s: (jnp.maximum(i - back, 0), col + g))

    return pl.pallas_call(
        functools.partial(_win_kernel, tq=tq),
        out_shape=jax.ShapeDtypeStruct((t, Q_W), F32),
        grid_spec=pltpu.PrefetchScalarGridSpec(
            num_scalar_prefetch=1,
            grid=(N_KV_GROUPS, t // tq),
            in_specs=[pl.BlockSpec((tq, gw), lambda g, i, s: (i, g)),
                      kv_spec(kw_col, 2), kv_spec(kw_col, 1), kv_spec(kw_col, 0),
                      kv_spec(vw_col, 2), kv_spec(vw_col, 1), kv_spec(vw_col, 0)],
            out_specs=pl.BlockSpec((tq, gw), lambda g, i, s: (i, g))),
        compiler_params=_cparams(("parallel", "parallel")),
        name="nsa_window",
    )(slopes, qkv, qkv, qkv, qkv, qkv, qkv, qkv)


def _gate_sum_kernel(oc_ref, os_ref, ow_ref, g_ref, o_ref):
    sg = _sigmoid(g_ref[...])
    for h in range(N_MIX_HEADS):
        sl = slice(h * HEAD_DIM, (h + 1) * HEAD_DIM)
        o = (oc_ref[:, sl] * sg[:, 3 * h:3 * h + 1]
             + os_ref[:, sl] * sg[:, 3 * h + 1:3 * h + 2]
             + ow_ref[:, sl] * sg[:, 3 * h + 2:3 * h + 3])
        o_ref[:, sl] = o.astype(o_ref.dtype)


def _gate_sum(o_cmp, o_slc, o_win, gates, tm=512):
    t = o_cmp.shape[0]
    tm = min(tm, t)
    big = pl.BlockSpec((tm, Q_W), lambda i: (i, 0))
    return pl.pallas_call(
        _gate_sum_kernel,
        out_shape=jax.ShapeDtypeStruct((t, Q_W), BF16),
        grid=(t // tm,),
        in_specs=[big, big, big, pl.BlockSpec((tm, GATE_W), lambda i: (i, 0))],
        out_specs=big,
        compiler_params=_cparams(("parallel",)),
        name="nsa_gate_sum",
    )(o_cmp, o_slc, o_win, gates)


def _mem_attn_kernel(q_ref, k_ref, v_ref, g_ref, o_ref):
    q = (_rms(q_ref[...], g_ref[0:1, :]) * SCALE).astype(BF16)
    k = _rms(k_ref[...], g_ref[1:2, :]).astype(BF16)
    s = _nt_dot(q, k)
    e = jnp.exp(s - jnp.max(s, axis=1, keepdims=True))
    p = e * (1.0 / jnp.sum(e, axis=1, keepdims=True))
    o_ref[...] = jnp.dot(p.astype(BF16), v_ref[...].astype(BF16),
                         preferred_element_type=F32).astype(o_ref.dtype)


def _mem_attn(qsrc, q_col, memkv, qk_gain, tq=2048):
    t = qsrc.shape[0]
    tq = min(tq, t)
    nm = memkv.shape[0]
    gtab = jnp.zeros((8, HEAD_DIM), F32).at[0:2].set(qk_gain)
    return pl.pallas_call(
        _mem_attn_kernel,
        out_shape=jax.ShapeDtypeStruct((t, MEM_Q_W), BF16),
        grid=(N_MEM_HEADS, t // tq),
        in_specs=[pl.BlockSpec((tq, HEAD_DIM), lambda h, i: (i, q_col + h)),
                  pl.BlockSpec((nm, HEAD_DIM), lambda h, i: (0, h)),
                  pl.BlockSpec((nm, HEAD_DIM), lambda h, i: (0, N_MEM_HEADS + h)),
                  pl.BlockSpec((8, HEAD_DIM), lambda h, i: (0, 0))],
        out_specs=pl.BlockSpec((tq, HEAD_DIM), lambda h, i: (i, h)),
        compiler_params=_cparams(("parallel", "parallel")),
        name="mem_attn",
    )(qsrc, memkv, memkv, gtab)


def _stick_kernel(q_ref, k_ref, v_ref, o_ref, *, tq, hp):
    i = pl.program_id(1)
    r = lax.broadcasted_iota(jnp.int32, (tq, tq), 0)
    c = lax.broadcasted_iota(jnp.int32, (tq, tq), 1)
    later = jnp.where(r > c, 1.0, 0.0).astype(BF16)
    qs = [(q_ref[:, h * HEAD_DIM:(h + 1) * HEAD_DIM] * (SCALE * LOG2E)).astype(BF16) for h in range(hp)]

    def tile(h, k0, tail, acc, diagonal):
        k = k_ref[pl.ds(k0, tq), h * HEAD_DIM:(h + 1) * HEAD_DIM]
        v = v_ref[pl.ds(k0, tq), h * HEAD_DIM:(h + 1) * HEAD_DIM]
        z = _nt_dot(qs[h], k)
        sp = jnp.maximum(z, 0.0) + jnp.log2(1.0 + jnp.exp2(-jnp.abs(z)))
        log_keep = jnp.where(r > c, -sp, 0.0) if diagonal else -sp
        hi = log_keep.astype(BF16)
        lo = (log_keep - hi.astype(F32)).astype(BF16)
        after = (jnp.dot(hi, later, preferred_element_type=F32)
                 + jnp.dot(lo, later, preferred_element_type=F32) + tail)
        a = jnp.exp2(z - sp + after)
        if diagonal:
            a = jnp.where(r > c, a, 0.0)
        acc = acc + jnp.dot(a.astype(BF16), v, preferred_element_type=F32)
        return tail + jnp.sum(log_keep, axis=1, keepdims=True), acc

    k_diag = pl.multiple_of(i * tq, tq)
    state = [tile(h, k_diag, jnp.zeros((tq, 1), F32), jnp.zeros((tq, HEAD_DIM), F32), True) for h in range(hp)]

    def live(st):
        return jnp.logical_and(st[0] <= i, st[1] > 0)

    def body(st):
        k0 = pl.multiple_of((i - st[0]) * tq, tq)
        new = [tile(h, k0, st[2 + 2 * h], st[3 + 2 * h], False) for h in range(hp)]
        top = functools.reduce(jnp.maximum, [jnp.max(tl) for tl, _ in new])
        return (st[0] + 1, (top > STICK_DEAD).astype(jnp.int32)) + tuple(x for pair in new for x in pair)

    st = lax.while_loop(live, body, (jnp.int32(1), jnp.int32(1)) + tuple(x for pair in state for x in pair))
    for h in range(hp):
        o_ref[:, h * HEAD_DIM:(h + 1) * HEAD_DIM] = st[3 + 2 * h].astype(o_ref.dtype)


def _stick_attn(pb, kvb, tq=256, hp=2):
    t = pb.shape[0]
    tq = min(tq, t)
    hw = hp * HEAD_DIM
    return pl.pallas_call(
        functools.partial(_stick_kernel, tq=tq, hp=hp),
        out_shape=jax.ShapeDtypeStruct((t, Q_W), BF16),
        grid=(N_MIX_HEADS // hp, t // tq),
        in_specs=[pl.BlockSpec((tq, hw), lambda h, i: (i, h)),
                  pl.BlockSpec((t, hw), lambda h, i: (0, h)),
                  pl.BlockSpec((t, hw), lambda h, i: (0, N_MIX_HEADS // hp + h))],
        out_specs=pl.BlockSpec((tq, hw), lambda h, i: (i, h)),
        compiler_params=_cparams(("parallel", "parallel")),
        name="stick_breaking",
    )(pb, kvb, kvb)


def _router_kernel(x_ref, g_ref, w_ref, b_ref, c_ref):
    h = _rms(x_ref[...], g_ref[...])
    logits = jnp.dot(h, w_ref[...], preferred_element_type=F32, precision=lax.Precision.HIGHEST)
    lane = lax.broadcasted_iota(jnp.int32, logits.shape, 1).astype(F32)
    biased = jnp.where(lane < N_EXPERTS, logits + b_ref[...], -3e38)
    m1 = jnp.max(biased, axis=1, keepdims=True)
    i1 = jnp.min(jnp.where(biased == m1, lane, float(LANES)), axis=1, keepdims=True)
    rest = jnp.where(lane == i1, -3e38, biased)
    m2 = jnp.max(rest, axis=1, keepdims=True)
    i2 = jnp.min(jnp.where(rest == m2, lane, float(LANES)), axis=1, keepdims=True)
    l1 = jnp.sum(jnp.where(lane == i1, logits, 0.0), axis=1, keepdims=True)
    l2 = jnp.sum(jnp.where(lane == i2, logits, 0.0), axis=1, keepdims=True)
    mx = jnp.maximum(l1, l2)
    e1 = jnp.exp(l1 - mx)
    e2 = jnp.exp(l2 - mx)
    inv = 1.0 / (e1 + e2)
    c_ref[...] = (jnp.where(lane == 0.0, i1, 0.0) + jnp.where(lane == 1.0, i2, 0.0)
                  + jnp.where(lane == 2.0, e1 * inv, 0.0) + jnp.where(lane == 3.0, e2 * inv, 0.0))


def _router(x, gain, router_w, router_b, tm=512):
    t, d = x.shape
    tm = min(tm, t)
    w = jnp.zeros((d, LANES), F32).at[:, :N_EXPERTS].set(router_w)
    b = jnp.zeros((1, LANES), F32).at[0, :N_EXPERTS].set(router_b)
    return pl.pallas_call(
        _router_kernel,
        out_shape=jax.ShapeDtypeStruct((t, LANES), F32),
        grid=(t // tm,),
        in_specs=[pl.BlockSpec((tm, d), lambda i: (i, 0)), pl.BlockSpec((1, d), lambda i: (0, 0)),
                  pl.BlockSpec((d, LANES), lambda i: (0, 0)), pl.BlockSpec((1, LANES), lambda i: (0, 0))],
        out_specs=pl.BlockSpec((tm, LANES), lambda i: (i, 0)),
        compiler_params=_cparams(("parallel",)),
        name="moe_router",
    )(x, gain.reshape(1, d), w, b)


def _route_plan(rt, sub, per_group):
    t = rt.shape[0]
    i32 = jnp.int32
    e_flat = rt[:, 0:2].astype(i32).reshape(-1)
    onehot = (e_flat[:, None] == jnp.arange(N_EXPERTS, dtype=i32)[None, :]).astype(i32)
    csum = jnp.cumsum(onehot, axis=0)
    rank = jnp.sum(csum * onehot, axis=1) - 1
    tiles_e = (csum[-1] + sub - 1) // sub
    tile_end = jnp.cumsum(tiles_e)
    tile_start = tile_end - tiles_e
    dest = tile_start[e_flat] * sub + rank
    groups_e = (tiles_e + per_group - 1) // per_group
    group_end = jnp.cumsum(groups_e)
    n_groups = group_end[-1]
    max_groups = (2 * t // sub + N_EXPERTS) // per_group + N_EXPERTS
    gidx = jnp.arange(max_groups, dtype=i32)
    gcl = jnp.minimum(gidx, n_groups - 1)
    g_expert = jnp.sum((gcl[:, None] >= group_end[None, :]).astype(i32), axis=1)
    k_in_e = gcl - (group_end - groups_e)[g_expert]
    g_start = (tile_start[g_expert] + k_in_e * per_group) * sub
    g_tiles = jnp.where(gidx < n_groups, jnp.clip(tiles_e[g_expert] - k_in_e * per_group, 0, per_group), 0)
    return (dest.astype(i32), tile_end[-1].reshape(1).astype(i32),
            g_expert.astype(i32), g_start.astype(i32), g_tiles.astype(i32))


def _dispatch_kernel(dest_ref, nt_ref, x_hbm, g_ref, o_ref, tok_ref, xbuf, sems, *, tm):
    n = pl.program_id(0)
    n_tiles = nt_ref[0]

    def gather(tile, slot):
        base = tile * tm

        def issue(q, c):
            for u in range(8):
                r = q * 8 + u
                pltpu.make_async_copy(x_hbm.at[pl.ds(tok_ref[base + r], 1), :],
                                      xbuf.at[slot, pl.ds(r, 1), :], sems.at[slot]).start(priority=u % 2)
            return c

        lax.fori_loop(0, tm // 8, issue, 0)

    @pl.when(n == 0)
    def _():
        def clear(r, c):
            tok_ref[r] = 0
            return c

        lax.fori_loop(0, tok_ref.shape[0], clear, 0, unroll=8)

        def fill(q, c):
            rows = [dest_ref[q * 8 + u] for u in range(8)]
            for u in range(8):
                tok_ref[rows[u]] = q * 4 + u // 2
            return c

        lax.fori_loop(0, dest_ref.shape[0] // 8, fill, 0)
        gather(0, 0)

    slot = n % 2

    @pl.when(n + 1 < n_tiles)
    def _():
        gather(n + 1, 1 - slot)

    @pl.when(n < n_tiles)
    def _():
        pltpu.make_async_copy(x_hbm.at[pl.ds(0, tm), :], xbuf.at[slot], sems.at[slot]).wait()
        o_ref[...] = _rms(xbuf[slot], g_ref[...]).astype(o_ref.dtype)

    @pl.when(n >= n_tiles)
    def _():
        o_ref[...] = jnp.zeros(o_ref.shape, o_ref.dtype)


def _dispatch(x, gain, dest, n_tiles, tm, nt):
    t, d = x.shape
    return pl.pallas_call(
        functools.partial(_dispatch_kernel, tm=tm),
        out_shape=jax.ShapeDtypeStruct((nt * tm, d), BF16),
        grid_spec=pltpu.PrefetchScalarGridSpec(
            num_scalar_prefetch=2,
            grid=(nt,),
            in_specs=[pl.BlockSpec(memory_space=pl.ANY), pl.BlockSpec((1, d), lambda n, dst, ntl: (0, 0))],
            out_specs=pl.BlockSpec((tm, d), lambda n, dst, ntl: (n, 0)),
            scratch_shapes=[pltpu.SMEM((nt * tm,), jnp.int32), pltpu.VMEM((2, tm, d), F32),
                            pltpu.SemaphoreType.DMA((2,))]),
        compiler_params=_cparams(("arbitrary",)),
        name="moe_dispatch",
    )(dest, n_tiles, x, gain.reshape(1, d))


def _experts_kernel(ge_ref, gs_ref, gn_ref, nt_ref, xg_hbm, wg_ref, wu_ref, wd_ref, y_hbm,
                    xbuf, yacc, wgc, wuc, wdc, sem_in, sem_out, *, sub):
    s = pl.program_id(0)
    j = pl.program_id(1)
    n_sub = gn_ref[s]
    start = pl.multiple_of(gs_ref[s], sub)

    @pl.when(jnp.logical_and(s == 0, j == 0))
    def _():
        yacc[0:sub, :] = jnp.zeros((sub, yacc.shape[1]), F32)
        first, last = nt_ref[0], y_hbm.shape[0] // sub

        def put(r, c):
            pltpu.make_async_copy(yacc.at[pl.ds(0, sub), :],
                                  y_hbm.at[pl.ds(pl.multiple_of(r * sub, sub), sub), :], sem_out).start()
            return c

        def done(r, c):
            pltpu.make_async_copy(yacc.at[pl.ds(0, sub), :], y_hbm.at[pl.ds(0, sub), :], sem_out).wait()
            return c

        lax.fori_loop(first, last, put, 0)
        lax.fori_loop(first, last, done, 0)

    @pl.when(n_sub > 0)
    def _():
        @pl.when(j == 0)
        def _():
            def fetch(r, c):
                r0 = pl.multiple_of(r * sub, sub)
                pltpu.make_async_copy(xg_hbm.at[pl.ds(start + r0, sub), :], xbuf.at[pl.ds(r0, sub), :],
                                      sem_in).start()
                return c

            def fetched(r, c):
                pltpu.make_async_copy(xg_hbm.at[pl.ds(0, sub), :], xbuf.at[pl.ds(0, sub), :], sem_in).wait()
                return c

            lax.fori_loop(0, n_sub, fetch, 0)
            yacc[...] = jnp.zeros(yacc.shape, F32)
            lax.fori_loop(0, n_sub, fetched, 0)

        wgc[...] = wg_ref[0].astype(BF16)
        wuc[...] = wu_ref[0].astype(BF16)
        wdc[...] = wd_ref[0].astype(BF16)

        def rows_at(r0, m):
            x = xbuf[pl.ds(r0, m), :]
            g = jnp.dot(x, wgc[...], preferred_element_type=F32)
            u = jnp.dot(x, wuc[...], preferred_element_type=F32)
            a = (g * _sigmoid(g) * u).astype(BF16)
            yacc[pl.ds(r0, m), :] += jnp.dot(a, wdc[...], preferred_element_type=F32)

        def tile_pair(r, c):
            rows_at(pl.multiple_of(r * 2 * sub, 2 * sub), 2 * sub)
            return c

        lax.fori_loop(0, n_sub // 2, tile_pair, 0)

        @pl.when(n_sub % 2 == 1)
        def _():
            rows_at(pl.multiple_of((n_sub - 1) * sub, sub), sub)

        @pl.when(j == pl.num_programs(1) - 1)
        def _():
            def put(r, c):
                r0 = pl.multiple_of(r * sub, sub)
                pltpu.make_async_copy(yacc.at[pl.ds(r0, sub), :], y_hbm.at[pl.ds(start + r0, sub), :],
                                      sem_out).start()
                return c

            def done(r, c):
                pltpu.make_async_copy(yacc.at[pl.ds(0, sub), :], y_hbm.at[pl.ds(0, sub), :], sem_out).wait()
                return c

            lax.fori_loop(0, n_sub, put, 0)
            lax.fori_loop(0, n_sub, done, 0)


def _moe_experts(xg, w_gu, w_down, g_expert, g_start, g_tiles, n_tiles, *, sub, per_group, tf=256):
    p, d = xg.shape
    f = w_down.shape[1]
    nj = f // tf
    rows = sub * per_group
    n_groups = g_expert.shape[0]

    def jcl(s, j, gn):
        return jnp.where(gn[s] > 0, j, nj - 1)

    return pl.pallas_call(
        functools.partial(_experts_kernel, sub=sub),
        out_shape=jax.ShapeDtypeStruct((p, d), F32),
        grid_spec=pltpu.PrefetchScalarGridSpec(
            num_scalar_prefetch=4,
            grid=(n_groups, nj),
            in_specs=[pl.BlockSpec(memory_space=pl.ANY),
                      pl.BlockSpec((1, d, tf), lambda s, j, ge, gs, gn, nt: (ge[s], 0, jcl(s, j, gn))),
                      pl.BlockSpec((1, d, tf), lambda s, j, ge, gs, gn, nt: (ge[s], 0, jcl(s, j, gn) + nj)),
                      pl.BlockSpec((1, tf, d), lambda s, j, ge, gs, gn, nt: (ge[s], jcl(s, j, gn), 0))],
            out_specs=pl.BlockSpec(memory_space=pl.ANY),
            scratch_shapes=[pltpu.VMEM((rows, d), BF16), pltpu.VMEM((rows, d), F32),
                            pltpu.VMEM((d, tf), BF16), pltpu.VMEM((d, tf), BF16), pltpu.VMEM((tf, d), BF16),
                            pltpu.SemaphoreType.DMA(()), pltpu.SemaphoreType.DMA(())]),
        compiler_params=pltpu.CompilerParams(dimension_semantics=("arbitrary", "arbitrary"),
                                             vmem_limit_bytes=MOE_VMEM_LIMIT),
        name="moe_experts",
    )(g_expert, g_start, g_tiles, n_tiles, xg, w_gu, w_gu, w_down)


def _moe_combine_kernel(dest_ref, x_ref, rt_ref, y_hbm, o_ref, ybuf, sems, *, tm):
    i = pl.program_id(0)

    def gather(tile, slot):
        base = tile * tm

        def issue(q, c):
            for u in range(4):
                r = q * 4 + u
                for k in range(2):
                    pltpu.make_async_copy(y_hbm.at[pl.ds(dest_ref[2 * (base + r) + k], 1), :],
                                          ybuf.at[slot, pl.ds(k * tm + r, 1), :],
                                          sems.at[slot]).start(priority=k)
            return c

        lax.fori_loop(0, tm // 4, issue, 0)

    @pl.when(i == 0)
    def _():
        gather(0, 0)

    slot = i % 2

    @pl.when(i + 1 < pl.num_programs(0))
    def _():
        gather(i + 1, 1 - slot)

    pltpu.make_async_copy(y_hbm.at[pl.ds(0, 2 * tm), :], ybuf.at[slot], sems.at[slot]).wait()
    w = rt_ref[...]
    o_ref[...] = (x_ref[...] + w[:, 2:3] * ybuf[slot, 0:tm, :] + w[:, 3:4] * ybuf[slot, tm:2 * tm, :])


def _moe_combine(x, rt, y, dest, tm=512):
    t, d = x.shape
    tm = min(tm, t)
    return pl.pallas_call(
        functools.partial(_moe_combine_kernel, tm=tm),
        out_shape=jax.ShapeDtypeStruct((t, d), F32),
        grid_spec=pltpu.PrefetchScalarGridSpec(
            num_scalar_prefetch=1,
            grid=(t // tm,),
            in_specs=[pl.BlockSpec((tm, d), lambda i, dst: (i, 0)),
                      pl.BlockSpec((tm, LANES), lambda i, dst: (i, 0)),
                      pl.BlockSpec(memory_space=pl.ANY)],
            out_specs=pl.BlockSpec((tm, d), lambda i, dst: (i, 0)),
            scratch_shapes=[pltpu.VMEM((2, 2 * tm, d), F32), pltpu.SemaphoreType.DMA((2,))]),
        compiler_params=_cparams(("arbitrary",)),
        name="moe_combine",
    )(dest, x, rt, y)


def _moe_block(xs, gain, router_w, router_b, w_gu, w_down):
    t = xs.shape[0]
    sub = min(MOE_SUB, t)
    nt = 2 * t // sub + N_EXPERTS
    rt = _router(xs, gain, router_w, router_b)
    dest, n_tiles, g_expert, g_start, g_tiles = _route_plan(rt, sub, MOE_TILES_PER_GROUP)
    xg = _dispatch(xs, gain, dest, (n_tiles + 1) // 2, 2 * sub, nt // 2)
    y = _moe_experts(xg, w_gu, w_down, g_expert, g_start, g_tiles, n_tiles,
                     sub=sub, per_group=MOE_TILES_PER_GROUP)
    return _moe_combine(xs, rt, y, dest)


def _alibi_slopes():
    slopes = np.array([2.0 ** (-8.0 * (i + 1) / N_MIX_HEADS) for i in range(N_MIX_HEADS)], dtype=np.float32)
    return jnp.asarray((slopes.astype(np.float64) * LOG2E).astype(np.float32))


def _importance_map(ncp, nb):
    ratio = SLC_BLOCK // CMP_STRIDE
    c = np.arange(ncp)[:, None]
    b = np.arange(nb)[None, :]
    m = (c >= ratio * b - 1) & (c <= ratio * b + ratio - 1) & (c < ncp - 1)
    return jnp.asarray(m.T.astype(np.float32), dtype=BF16)


def _nsa_mix(p, slopes, q_gain, k_gain, cmp_w, cmp_pos, slc_tq=256):
    t = p.shape[0]
    qkv = _nsa_prep(p, q_gain, k_gain)
    kvc = _compress(p, cmp_w, cmp_pos, k_gain[0])
    imap = _importance_map(t // CMP_STRIDE, t // SLC_BLOCK)
    o_cmp, score = _cmp_attn(slopes, qkv, kvc, imap)
    sel, blk_act = _topk_select(score, sub=slc_tq)
    o_slc = _slc_attn(slopes, qkv, sel, blk_act[:, :, 0, :], tq=slc_tq)
    o_win = _win_attn(slopes, qkv)
    gates = p[:, Q_W + 6 * KV_W:Q_W + 6 * KV_W + GATE_W]
    return _gate_sum(o_cmp, o_slc, o_win, gates)


def kernel(x, mem, attn_norm, mem_norm, ffn_norm, w_in_a, cmp_w, cmp_pos, nsa_q_norm, nsa_k_norm, w_in_b,
           kv_norm_b, w_kv_b, w_mem_kv, mem_qk_norm, w_out, ffn_w_gu, ffn_w_down, router_w, router_b,
           moe_w_gu, moe_w_down):
    b, t, d = x.shape
    assert b == 1
    xs = x[0]
    mems = mem[0]
    slopes = _alibi_slopes()

    p = _norm_matmul(xs, attn_norm[0], w_in_a[0], tm=1024, tn=896, name="in_proj_a")
    mix = _nsa_mix(p, slopes, nsa_q_norm[0], nsa_k_norm[0], cmp_w[0], cmp_pos[0])
    memkv = _norm_matmul(mems, mem_norm[0], w_mem_kv[0], tm=256, tn=512, name="mem_kv")
    qm = p[:, Q_W + 6 * KV_W + GATE_W:]
    mo = _mem_attn(qm, 0, memkv, mem_qk_norm[0])
    xs = _out_proj(mix, mo, w_out[0], xs)
    act = _swiglu_up(xs, ffn_norm[0], ffn_w_gu[0])
    xs = _matmul(act, ffn_w_down[0], xs, tm=1024, tn=1024, tk=1024, name="ffn_down")

    kvb = _norm_matmul(xs, kv_norm_b, w_kv_b, tm=1024, tn=1024, out_dtype=BF16, name="kv_proj_b")
    pb = _norm_matmul(xs, attn_norm[1], w_in_b[0], tm=1024, tn=1024, name="in_proj_b")
    mix = _stick_attn(pb, kvb)
    memkv = _norm_matmul(mems, mem_norm[1], w_mem_kv[1], tm=256, tn=512, name="mem_kv")
    mo = _mem_attn(pb, N_MIX_HEADS, memkv, mem_qk_norm[1])
    xs = _out_proj(mix, mo, w_out[1], xs)
    xs = _moe_block(xs, ffn_norm[1], router_w[0], router_b[0], moe_w_gu[0], moe_w_down[0])
    return xs[None]
```

```python
import functools

import numpy as np
import jax
import jax.numpy as jnp
from jax import lax
from jax.experimental import pallas as pl
from jax.experimental.pallas import tpu as pltpu

HEAD_DIM = 128
N_MEM_HEADS = 4
N_MIX_HEADS = 12
N_KV_GROUPS = 3
HEADS_PER_GROUP = 4
CMP_STRIDE = 16
CMP_BLOCK = 32
SLC_BLOCK = 64
SLC_TOPK = 16
WINDOW = 512
N_EXPERTS = 8
Q_W = N_MIX_HEADS * HEAD_DIM
KV_W = N_KV_GROUPS * HEAD_DIM
GATE_W = 3 * N_MIX_HEADS
MEM_Q_W = N_MEM_HEADS * HEAD_DIM
NEG = -1e30
BIG = 1e6
EPS = 1e-6
SCALE = HEAD_DIM ** -0.5
LOG2E = 1.4426950408889634
STICK_DEAD = -151.0
LANES = 128
VMEM_LIMIT = 48 * 1024 * 1024
MOE_VMEM_LIMIT = 56 * 1024 * 1024
WIDE_VMEM_LIMIT = 56 * 1024 * 1024
MOE_SUB = 256
MOE_TILES_PER_GROUP = 9

F32 = jnp.float32
BF16 = jnp.bfloat16


def _cparams(sem, vmem=VMEM_LIMIT):
    return pltpu.CompilerParams(dimension_semantics=sem, vmem_limit_bytes=vmem)


def _nt_dot(a, b):
    return lax.dot_general(a, b, (((1,), (1,)), ((), ())), preferred_element_type=F32)


def _rms(x, gain):
    return x * lax.rsqrt(jnp.mean(x * x, axis=-1, keepdims=True) + EPS) * gain


def _sigmoid(x):
    return 1.0 / (1.0 + jnp.exp(-x))


def _rmsnorm_cast_kernel(x_ref, g_ref, o_ref):
    o_ref[...] = _rms(x_ref[...], g_ref[...]).astype(o_ref.dtype)


def _rmsnorm_cast(x, gain, tm=512):
    m, d = x.shape
    tm = min(tm, m)
    return pl.pallas_call(
        _rmsnorm_cast_kernel,
        out_shape=jax.ShapeDtypeStruct((m, d), BF16),
        grid=(m // tm,),
        in_specs=[pl.BlockSpec((tm, d), lambda i: (i, 0)), pl.BlockSpec((1, d), lambda i: (0, 0))],
        out_specs=pl.BlockSpec((tm, d), lambda i: (i, 0)),
        compiler_params=_cparams(("parallel",)),
        name="rmsnorm_cast",
    )(x, gain.reshape(1, d))


def _norm_matmul_kernel(x_ref, g_ref, w_ref, o_ref, xn_ref):
    @pl.when(pl.program_id(1) == 0)
    def _():
        xn_ref[...] = _rms(x_ref[...], g_ref[...]).astype(BF16)

    o_ref[...] = jnp.dot(xn_ref[...], w_ref[...].astype(BF16), preferred_element_type=F32).astype(o_ref.dtype)


def _norm_matmul(x, gain, w, *, tm, tn, out_dtype=F32, name="norm_matmul"):
    m, d = x.shape
    n = w.shape[1]
    tm = min(tm, m)
    return pl.pallas_call(
        _norm_matmul_kernel,
        out_shape=jax.ShapeDtypeStruct((m, n), out_dtype),
        grid=(m // tm, pl.cdiv(n, tn)),
        in_specs=[pl.BlockSpec((tm, d), lambda i, j: (i, 0)),
                  pl.BlockSpec((1, d), lambda i, j: (0, 0)),
                  pl.BlockSpec((d, tn), lambda i, j: (0, j))],
        out_specs=pl.BlockSpec((tm, tn), lambda i, j: (i, j)),
        scratch_shapes=[pltpu.VMEM((tm, d), BF16)],
        compiler_params=_cparams(("parallel", "arbitrary"), WIDE_VMEM_LIMIT),
        name=name,
    )(x, gain.reshape(1, d), w)


def _matmul_kernel(*refs, nk, has_res):
    if has_res:
        a_ref, w_ref, r_ref, o_ref = refs[:4]
    else:
        a_ref, w_ref, o_ref = refs[:3]
        r_ref = None
    part = jnp.dot(a_ref[...], w_ref[...].astype(BF16), preferred_element_type=F32)
    if nk == 1:
        if has_res:
            part = part + r_ref[...]
        o_ref[...] = part.astype(o_ref.dtype)
        return
    acc_ref = refs[-1]
    k = pl.program_id(2)

    @pl.when(k == 0)
    def _():
        acc_ref[...] = part

    @pl.when(k > 0)
    def _():
        acc_ref[...] += part

    @pl.when(k == nk - 1)
    def _():
        r = acc_ref[...]
        if has_res:
            r = r + r_ref[...]
        o_ref[...] = r.astype(o_ref.dtype)


def _matmul(a, w, res=None, *, tm, tn, tk=None, out_dtype=F32, name="matmul"):
    m, kdim = a.shape
    n = w.shape[1]
    tm = min(tm, m)
    tk = kdim if tk is None else tk
    nk = kdim // tk
    in_specs = [pl.BlockSpec((tm, tk), lambda i, j, k: (i, k)),
                pl.BlockSpec((tk, tn), lambda i, j, k: (k, j))]
    args = [a, w]
    if res is not None:
        in_specs.append(pl.BlockSpec((tm, tn), lambda i, j, k: (i, j)))
        args.append(res)
    scratch = [pltpu.VMEM((tm, tn), F32)] if nk > 1 else []
    return pl.pallas_call(
        functools.partial(_matmul_kernel, nk=nk, has_res=res is not None),
        out_shape=jax.ShapeDtypeStruct((m, n), out_dtype),
        grid=(m // tm, pl.cdiv(n, tn), nk),
        in_specs=in_specs,
        out_specs=pl.BlockSpec((tm, tn), lambda i, j, k: (i, j)),
        scratch_shapes=scratch,
        compiler_params=_cparams(("parallel", "parallel", "arbitrary")),
        name=name,
    )(*args)


def _out_proj_kernel(mix_ref, mo_ref, w1_ref, w2_ref, r_ref, o_ref):
    o_ref[...] = (jnp.dot(mix_ref[...], w1_ref[...].astype(BF16), preferred_element_type=F32)
                  + jnp.dot(mo_ref[...], w2_ref[...].astype(BF16), preferred_element_type=F32)
                  + r_ref[...])


def _out_proj(mix, mo, w, res, tm=1024, tn=1024):
    t = mix.shape[0]
    tm = min(tm, t)
    n = w.shape[1]
    assert Q_W % MEM_Q_W == 0
    return pl.pallas_call(
        _out_proj_kernel,
        out_shape=jax.ShapeDtypeStruct((t, n), F32),
        grid=(t // tm, n // tn),
        in_specs=[pl.BlockSpec((tm, Q_W), lambda i, j: (i, 0)),
                  pl.BlockSpec((tm, MEM_Q_W), lambda i, j: (i, 0)),
                  pl.BlockSpec((Q_W, tn), lambda i, j: (0, j)),
                  pl.BlockSpec((MEM_Q_W, tn), lambda i, j: (Q_W // MEM_Q_W, j)),
                  pl.BlockSpec((tm, tn), lambda i, j: (i, j))],
        out_specs=pl.BlockSpec((tm, tn), lambda i, j: (i, j)),
        compiler_params=_cparams(("parallel", "parallel")),
        name="out_proj",
    )(mix, mo, w, w, res)


def _swiglu_kernel(x_ref, gn_ref, wg_ref, wu_ref, o_ref, xn_ref):
    @pl.when(pl.program_id(1) == 0)
    def _():
        xn_ref[...] = _rms(x_ref[...], gn_ref[...]).astype(BF16)

    a = xn_ref[...]
    g = jnp.dot(a, wg_ref[...].astype(BF16), preferred_element_type=F32)
    u = jnp.dot(a, wu_ref[...].astype(BF16), preferred_element_type=F32)
    o_ref[...] = (g * _sigmoid(g) * u).astype(o_ref.dtype)


def _swiglu_up(x, gain, w_gu, *, tm=1024, tn=512):
    m, d = x.shape
    f = w_gu.shape[1] // 2
    tm = min(tm, m)
    nj = f // tn
    return pl.pallas_call(
        _swiglu_kernel,
        out_shape=jax.ShapeDtypeStruct((m, f), BF16),
        grid=(m // tm, nj),
        in_specs=[pl.BlockSpec((tm, d), lambda i, j: (i, 0)),
                  pl.BlockSpec((1, d), lambda i, j: (0, 0)),
                  pl.BlockSpec((d, tn), lambda i, j: (0, j)),
                  pl.BlockSpec((d, tn), lambda i, j: (0, j + nj))],
        out_specs=pl.BlockSpec((tm, tn), lambda i, j: (i, j)),
        scratch_shapes=[pltpu.VMEM((tm, d), BF16)],
        compiler_params=_cparams(("parallel", "arbitrary"), WIDE_VMEM_LIMIT),
        name="swiglu_up",
    )(x, gain.reshape(1, d), w_gu, w_gu)


def _nsa_prep_kernel(p_ref, gtab_ref, o_ref):
    for c in range((Q_W + 6 * KV_W) // HEAD_DIM):
        x = p_ref[:, c * HEAD_DIM:(c + 1) * HEAD_DIM]
        if c < N_MIX_HEADS:
            y = _rms(x, gtab_ref[0:1, :]) * (SCALE * LOG2E)
        elif 18 <= c < 21:
            y = _rms(x, gtab_ref[1:2, :])
        elif 24 <= c < 27:
            y = _rms(x, gtab_ref[2:3, :])
        else:
            y = x
        o_ref[:, c * HEAD_DIM:(c + 1) * HEAD_DIM] = y.astype(o_ref.dtype)


def _nsa_prep(p, q_gain, k_gain, tm=256):
    t = p.shape[0]
    tm = min(tm, t)
    w = Q_W + 6 * KV_W
    gtab = jnp.zeros((8, HEAD_DIM), F32).at[0].set(q_gain).at[1].set(k_gain[1]).at[2].set(k_gain[2])
    return pl.pallas_call(
        _nsa_prep_kernel,
        out_shape=jax.ShapeDtypeStruct((t, w), BF16),
        grid=(t // tm,),
        in_specs=[pl.BlockSpec((tm, w), lambda i: (i, 0)), pl.BlockSpec((8, HEAD_DIM), lambda i: (0, 0))],
        out_specs=pl.BlockSpec((tm, w), lambda i: (i, 0)),
        compiler_params=_cparams(("parallel",)),
        name="nsa_prep",
    )(p, gtab)


def _compress_kernel(p_ref, pos_ref, w_ref, g_ref, o_ref):
    ncp = o_ref.shape[2]
    first = jnp.zeros((ncp, HEAD_DIM), F32)
    second = jnp.zeros((ncp, HEAD_DIM), F32)
    for r in range(CMP_STRIDE):
        rows = p_ref[pl.ds(r, ncp, stride=CMP_STRIDE), :]
        lo = (rows + pos_ref[0, r:r + 1, :]).astype(BF16)
        hi = (rows + pos_ref[0, CMP_STRIDE + r:CMP_STRIDE + r + 1, :]).astype(BF16)
        first += jnp.dot(lo, w_ref[0, r * HEAD_DIM:(r + 1) * HEAD_DIM, :].astype(BF16),
                         preferred_element_type=F32)
        second += jnp.dot(hi, w_ref[0, (CMP_STRIDE + r) * HEAD_DIM:(CMP_STRIDE + r + 1) * HEAD_DIM, :].astype(BF16),
                          preferred_element_type=F32)
    y = first + jnp.concatenate([second[1:], jnp.zeros((1, HEAD_DIM), F32)], axis=0)
    is_k = pl.program_id(0) == 0
    o_ref[0, 0] = jnp.where(is_k, _rms(y, g_ref[...]), y).astype(o_ref.dtype)


def _compress(p, cmp_w, cmp_pos, k_gain0):
    t = p.shape[0]
    ncp = t // CMP_STRIDE
    col0 = Q_W // HEAD_DIM
    return pl.pallas_call(
        _compress_kernel,
        out_shape=jax.ShapeDtypeStruct((2, N_KV_GROUPS, ncp, HEAD_DIM), BF16),
        grid=(2, N_KV_GROUPS),
        in_specs=[pl.BlockSpec((t, HEAD_DIM), lambda s, gi: (0, col0 + N_KV_GROUPS * s + gi)),
                  pl.BlockSpec((1, CMP_BLOCK, HEAD_DIM), lambda s, gi: (s, 0, 0)),
                  pl.BlockSpec((1, CMP_BLOCK * HEAD_DIM, HEAD_DIM), lambda s, gi: (s, 0, 0)),
                  pl.BlockSpec((1, HEAD_DIM), lambda s, gi: (0, 0))],
        out_specs=pl.BlockSpec((1, 1, ncp, HEAD_DIM), lambda s, gi: (s, gi, 0, 0)),
        compiler_params=_cparams(("parallel", "parallel")),
        name="nsa_compress",
    )(p, cmp_pos, cmp_w, k_gain0.reshape(1, HEAD_DIM))


def _cmp_kernel(slopes_ref, q_ref, k_ref, v_ref, imap_ref, o_ref, score_ref, *, tq, ncp):
    g = pl.program_id(0)
    t0 = pl.program_id(1) * tq
    row = lax.broadcasted_iota(jnp.int32, (tq, ncp), 0) + t0
    col = lax.broadcasted_iota(jnp.int32, (tq, ncp), 1)
    c_end = col * CMP_STRIDE + (CMP_BLOCK - 1)
    neg = jnp.where(row >= c_end, 0.0, NEG)
    colb = (lax.broadcasted_iota(jnp.int32, (1, ncp), 1) * CMP_STRIDE + (CMP_BLOCK - 1) - t0).astype(F32)
    has_key = (lax.broadcasted_iota(jnp.int32, (tq, 1), 0) + t0 >= CMP_BLOCK - 1).astype(F32)
    k = k_ref[0, 0]
    v = v_ref[0, 0]
    psum = jnp.zeros((tq, ncp), F32)
    for p in range(HEADS_PER_GROUP):
        slope = slopes_ref[g * HEADS_PER_GROUP + p]
        q = q_ref[:, p * HEAD_DIM:(p + 1) * HEAD_DIM]
        s = _nt_dot(q, k) + slope * colb + neg
        e = jnp.exp2(s - jnp.max(s, axis=1, keepdims=True))
        pr = e * (has_key / jnp.sum(e, axis=1, keepdims=True))
        psum = psum + pr
        o_ref[:, p * HEAD_DIM:(p + 1) * HEAD_DIM] = jnp.dot(
            pr.astype(BF16), v, preferred_element_type=F32).astype(o_ref.dtype)

    imap = imap_ref[...]
    hi = psum.astype(BF16)
    r1 = psum - hi.astype(F32)
    mid = r1.astype(BF16)
    lo = (r1 - mid.astype(F32)).astype(BF16)
    imp = _nt_dot(imap, hi) + _nt_dot(imap, mid) + _nt_dot(imap, lo)
    nb = imp.shape[0]
    blk = lax.broadcasted_iota(jnp.int32, (nb, tq), 0)
    t = lax.broadcasted_iota(jnp.int32, (nb, tq), 1) + t0
    score_ref[0] = jnp.where(blk == t // SLC_BLOCK, 2 * BIG,
                             jnp.where(blk == 0, BIG,
                                       jnp.where(blk * SLC_BLOCK <= t, imp, -BIG)))


def _cmp_attn(slopes, qkv, kvc, imap, tq=256):
    t = qkv.shape[0]
    tq = min(tq, t)
    ncp = kvc.shape[2]
    nb = t // SLC_BLOCK
    gw = HEADS_PER_GROUP * HEAD_DIM
    return pl.pallas_call(
        functools.partial(_cmp_kernel, tq=tq, ncp=ncp),
        out_shape=(jax.ShapeDtypeStruct((t, Q_W), F32),
                   jax.ShapeDtypeStruct((N_KV_GROUPS, nb, t), F32)),
        grid_spec=pltpu.PrefetchScalarGridSpec(
            num_scalar_prefetch=1,
            grid=(N_KV_GROUPS, t // tq),
            in_specs=[pl.BlockSpec((tq, gw), lambda g, i, s: (i, g)),
                      pl.BlockSpec((1, 1, ncp, HEAD_DIM), lambda g, i, s: (0, g, 0, 0)),
                      pl.BlockSpec((1, 1, ncp, HEAD_DIM), lambda g, i, s: (1, g, 0, 0)),
                      pl.BlockSpec((nb, ncp), lambda g, i, s: (0, 0))],
            out_specs=(pl.BlockSpec((tq, gw), lambda g, i, s: (i, g)),
                       pl.BlockSpec((1, nb, tq), lambda g, i, s: (g, 0, i)))),
        compiler_params=_cparams(("parallel", "parallel")),
        name="nsa_cmp",
    )(slopes, qkv, kvc, kvc, imap)


def _topk_kernel(score_ref, sel_ref, act_ref, *, tm, sub):
    score = score_ref[0]
    nb = score.shape[0]
    blk_f = lax.broadcasted_iota(jnp.int32, (nb, tm), 0).astype(F32)

    def pick_one(_, carry):
        work, sel = carry
        m = jnp.max(work, axis=0, keepdims=True)
        first = jnp.min(jnp.where(work == m, blk_f, float(nb)), axis=0, keepdims=True)
        pick = blk_f == first
        return jnp.where(pick, -3e38, work), jnp.where(pick, 1.0, sel)

    forced = score >= 0.5 * BIG
    start = (jnp.where(forced, -3e38, score), jnp.where(forced, 1.0, 0.0))
    _, sel_t = lax.fori_loop(0, max(min(SLC_TOPK, nb) - 2, 0), pick_one, start)
    sel = sel_t.T
    sel_ref[0] = sel.astype(sel_ref.dtype)
    for a in range(tm // sub):
        act_ref[0, a] = jnp.broadcast_to(jnp.max(sel[a * sub:(a + 1) * sub], axis=0, keepdims=True),
                                         act_ref.shape[2:])


def _topk_select(score, sub, tm=1024):
    g, nb, t = score.shape
    tm = min(tm, t)
    sub = min(sub, tm)
    return pl.pallas_call(
        functools.partial(_topk_kernel, tm=tm, sub=sub),
        out_shape=(jax.ShapeDtypeStruct((g, t, nb), BF16),
                   jax.ShapeDtypeStruct((g, t // sub, 8, nb), F32)),
        grid=(g, t // tm),
        in_specs=[pl.BlockSpec((1, nb, tm), lambda gi, i: (gi, 0, i))],
        out_specs=(pl.BlockSpec((1, tm, nb), lambda gi, i: (gi, i, 0)),
                   pl.BlockSpec((1, tm // sub, 8, nb), lambda gi, i: (gi, i, 0, 0))),
        compiler_params=_cparams(("parallel", "parallel")),
        name="nsa_topk",
    )(score)


def _slc_kernel(slopes_ref, flags_ref, q_ref, k_ref, v_ref, sel_ref, oc_ref, ow_ref, gate_ref, o_ref,
                m_ref, l_ref, acc_ref, *, tq, tk):
    g = pl.program_id(0)
    i = pl.program_id(1)
    t0 = i * tq
    flag_base = (g * pl.num_programs(1) + i) * (k_ref.shape[0] // tk)
    nb = sel_ref.shape[2]
    m_ref[...] = jnp.full(m_ref.shape, NEG, F32)
    l_ref[...] = jnp.zeros(l_ref.shape, F32)
    acc_ref[...] = jnp.zeros(acc_ref.shape, F32)
    sel = sel_ref[0]
    bpt = tk // SLC_BLOCK

    def body(j, carry):
        @pl.when(flags_ref[flag_base + j] > 0)
        def _():
            tile(j, False)
        return carry

    def tile(j, diagonal):
        k0 = pl.multiple_of(j * tk, tk)
        k = k_ref[pl.ds(k0, tk), :]
        v = v_ref[pl.ds(k0, tk), :]
        eb = lax.broadcasted_iota(jnp.int32, (nb, tk), 0)
        ec = lax.broadcasted_iota(jnp.int32, (nb, tk), 1)
        expand = jnp.where(eb == ec // SLC_BLOCK + j * bpt, 1.0, 0.0).astype(BF16)
        selx = jnp.dot(sel, expand, preferred_element_type=F32)
        if diagonal:
            later = (lax.broadcasted_iota(jnp.int32, (tq, tk), 0) + (t0 - k0)
                     < lax.broadcasted_iota(jnp.int32, (tq, tk), 1))
            selx = jnp.where(later, 0.0, selx)
        neg = jnp.where(selx > 0.5, 0.0, NEG)
        colb = (lax.broadcasted_iota(jnp.int32, (1, tk), 1) + (k0 - t0)).astype(F32)
        for p in range(HEADS_PER_GROUP):
            slope = slopes_ref[g * HEADS_PER_GROUP + p]
            q = q_ref[:, p * HEAD_DIM:(p + 1) * HEAD_DIM]
            s = _nt_dot(q, k) + slope * colb + neg
            m_old = m_ref[p]
            m_new = jnp.maximum(m_old, jnp.max(s, axis=1, keepdims=True))
            alpha = jnp.exp2(m_old - m_new)
            e = jnp.exp2(s - jnp.tile(m_new, (1, tk // LANES)))
            l_ref[p] = alpha * l_ref[p] + jnp.sum(e, axis=1, keepdims=True)
            acc_ref[p] = alpha * acc_ref[p] + jnp.dot(e.astype(BF16), v, preferred_element_type=F32)
            m_ref[p] = m_new

    own = t0 // tk
    lax.fori_loop(0, own, body, 0)
    tile(own, True)
    sg = _sigmoid(gate_ref[0])
    for p in range(HEADS_PER_GROUP):
        sl = slice(p * HEAD_DIM, (p + 1) * HEAD_DIM)
        o = (oc_ref[:, sl] * sg[:, 3 * p:3 * p + 1]
             + (acc_ref[p] / l_ref[p]) * sg[:, 3 * p + 1:3 * p + 2]
             + ow_ref[:, sl] * sg[:, 3 * p + 2:3 * p + 3])
        o_ref[:, sl] = o.astype(o_ref.dtype)


def _slc_attn(slopes, qkv, sel, blk_act, o_cmp, o_win, gates, tq=256, tk=256):
    t = qkv.shape[0]
    gates_g = gates.reshape(t, N_KV_GROUPS, 3 * HEADS_PER_GROUP).transpose(1, 0, 2)
    tq = min(tq, t)
    tk = min(tk, t)
    nb = sel.shape[2]
    assert blk_act.shape[1] == t // tq and tk % tq == 0
    gw = HEADS_PER_GROUP * HEAD_DIM
    ks_col = (Q_W + 2 * KV_W) // HEAD_DIM
    vs_col = (Q_W + 3 * KV_W) // HEAD_DIM
    flags = jnp.max(blk_act.reshape(N_KV_GROUPS, t // tq, t // tk, tk // SLC_BLOCK), axis=-1)
    flags = (flags > 0.5).astype(jnp.int32).reshape(-1)
    return pl.pallas_call(
        functools.partial(_slc_kernel, tq=tq, tk=tk),
        out_shape=jax.ShapeDtypeStruct((t, Q_W), BF16),
        grid_spec=pltpu.PrefetchScalarGridSpec(
            num_scalar_prefetch=2,
            grid=(N_KV_GROUPS, t // tq),
            in_specs=[pl.BlockSpec((tq, gw), lambda g, i, s, f: (i, g)),
                      pl.BlockSpec((t, HEAD_DIM), lambda g, i, s, f: (0, ks_col + g)),
                      pl.BlockSpec((t, HEAD_DIM), lambda g, i, s, f: (0, vs_col + g)),
                      pl.BlockSpec((1, tq, nb), lambda g, i, s, f: (g, i, 0)),
                      pl.BlockSpec((tq, gw), lambda g, i, s, f: (i, g)),
                      pl.BlockSpec((tq, gw), lambda g, i, s, f: (i, g)),
                      pl.BlockSpec((1, tq, 3 * HEADS_PER_GROUP), lambda g, i, s, f: (g, i, 0))],
            out_specs=pl.BlockSpec((tq, gw), lambda g, i, s, f: (i, g)),
            scratch_shapes=[pltpu.VMEM((HEADS_PER_GROUP, tq, LANES), F32),
                            pltpu.VMEM((HEADS_PER_GROUP, tq, LANES), F32),
                            pltpu.VMEM((HEADS_PER_GROUP, tq, HEAD_DIM), F32)]),
        compiler_params=_cparams(("parallel", "parallel")),
        name="nsa_selected",
    )(slopes, flags, qkv, qkv, qkv, sel, o_cmp, o_win, gates_g)


def _win_kernel(slopes_ref, q_ref, k0_ref, k1_ref, k2_ref, v0_ref, v1_ref, v2_ref, o_ref, *, tq):
    g = pl.program_id(0)
    i = pl.program_id(1)
    nprev = WINDOW // tq
    kw = (nprev + 1) * tq
    k = jnp.concatenate([k0_ref[...], k1_ref[...], k2_ref[...]], axis=0)
    v = jnp.concatenate([v0_ref[...], v1_ref[...], v2_ref[...]], axis=0)
    r = lax.broadcasted_iota(jnp.int32, (tq, kw), 0)
    c = lax.broadcasted_iota(jnp.int32, (tq, kw), 1)
    rel = WINDOW + r - c
    first_valid = (nprev - i) * tq
    inside = jnp.where(rel >= 0, jnp.where(rel < WINDOW, 1.0, 0.0), 0.0)
    neg = jnp.where(jnp.where(c >= first_valid, inside, 0.0) > 0.5, 0.0, NEG)
    colb = (lax.broadcasted_iota(jnp.int32, (1, kw), 1) - WINDOW).astype(F32)
    for p in range(HEADS_PER_GROUP):
        slope = slopes_ref[g * HEADS_PER_GROUP + p]
        q = q_ref[:, p * HEAD_DIM:(p + 1) * HEAD_DIM]
        s = _nt_dot(q, k) + slope * colb + neg
        e = jnp.exp2(s - jnp.max(s, axis=1, keepdims=True))
        o = jnp.dot(e.astype(BF16), v, preferred_element_type=F32) / jnp.sum(e, axis=1, keepdims=True)
        o_ref[:, p * HEAD_DIM:(p + 1) * HEAD_DIM] = o.astype(o_ref.dtype)


def _win_attn(slopes, qkv, tq=256):
    t = qkv.shape[0]
    assert WINDOW % tq == 0 and WINDOW // tq == 2
    gw = HEADS_PER_GROUP * HEAD_DIM
    kw_col = (Q_W + 4 * KV_W) // HEAD_DIM
    vw_col = (Q_W + 5 * KV_W) // HEAD_DIM

    def kv_spec(col, back):
        return pl.BlockSpec((tq, HEAD_DIM), lambda g, i, s: (jnp.maximum(i - back, 0), col + g))

    return pl.pallas_call(
        functools.partial(_win_kernel, tq=tq),
        out_shape=jax.ShapeDtypeStruct((t, Q_W), F32),
        grid_spec=pltpu.PrefetchScalarGridSpec(
            num_scalar_prefetch=1,
            grid=(N_KV_GROUPS, t // tq),
            in_specs=[pl.BlockSpec((tq, gw), lambda g, i, s: (i, g)),
                      kv_spec(kw_col, 2), kv_spec(kw_col, 1), kv_spec(kw_col, 0),
                      kv_spec(vw_col, 2), kv_spec(vw_col, 1), kv_spec(vw_col, 0)],
            out_specs=pl.BlockSpec((tq, gw), lambda g, i, s: (i, g))),
        compiler_params=_cparams(("parallel", "parallel")),
        name="nsa_window",
    )(slopes, qkv, qkv, qkv, qkv, qkv, qkv, qkv)


def _gate_sum_kernel(oc_ref, os_ref, ow_ref, g_ref, o_ref):
    sg = _sigmoid(g_ref[...])
    for h in range(N_MIX_HEADS):
        sl = slice(h * HEAD_DIM, (h + 1) * HEAD_DIM)
        o = (oc_ref[:, sl] * sg[:, 3 * h:3 * h + 1]
             + os_ref[:, sl] * sg[:, 3 * h + 1:3 * h + 2]
             + ow_ref[:, sl] * sg[:, 3 * h + 2:3 * h + 3])
        o_ref[:, sl] = o.astype(o_ref.dtype)


def _gate_sum(o_cmp, o_slc, o_win, gates, tm=512):
    t = o_cmp.shape[0]
    tm = min(tm, t)
    big = pl.BlockSpec((tm, Q_W), lambda i: (i, 0))
    return pl.pallas_call(
        _gate_sum_kernel,
        out_shape=jax.ShapeDtypeStruct((t, Q_W), BF16),
        grid=(t // tm,),
        in_specs=[big, big, big, pl.BlockSpec((tm, GATE_W), lambda i: (i, 0))],
        out_specs=big,
        compiler_params=_cparams(("parallel",)),
        name="nsa_gate_sum",
    )(o_cmp, o_slc, o_win, gates)


def _mem_attn_kernel(q_ref, k_ref, v_ref, g_ref, o_ref):
    q = (_rms(q_ref[...], g_ref[0:1, :]) * SCALE).astype(BF16)
    k = _rms(k_ref[...], g_ref[1:2, :]).astype(BF16)
    s = _nt_dot(q, k)
    e = jnp.exp(s - jnp.max(s, axis=1, keepdims=True))
    p = e * (1.0 / jnp.sum(e, axis=1, keepdims=True))
    o_ref[...] = jnp.dot(p.astype(BF16), v_ref[...].astype(BF16),
                         preferred_element_type=F32).astype(o_ref.dtype)


def _mem_attn(qsrc, q_col, memkv, qk_gain, tq=2048):
    t = qsrc.shape[0]
    tq = min(tq, t)
    nm = memkv.shape[0]
    gtab = jnp.zeros((8, HEAD_DIM), F32).at[0:2].set(qk_gain)
    return pl.pallas_call(
        _mem_attn_kernel,
        out_shape=jax.ShapeDtypeStruct((t, MEM_Q_W), BF16),
        grid=(N_MEM_HEADS, t // tq),
        in_specs=[pl.BlockSpec((tq, HEAD_DIM), lambda h, i: (i, q_col + h)),
                  pl.BlockSpec((nm, HEAD_DIM), lambda h, i: (0, h)),
                  pl.BlockSpec((nm, HEAD_DIM), lambda h, i: (0, N_MEM_HEADS + h)),
                  pl.BlockSpec((8, HEAD_DIM), lambda h, i: (0, 0))],
        out_specs=pl.BlockSpec((tq, HEAD_DIM), lambda h, i: (i, h)),
        compiler_params=_cparams(("parallel", "parallel")),
        name="mem_attn",
    )(qsrc, memkv, memkv, gtab)


def _stick_kernel(q_ref, k_ref, v_ref, o_ref, *, tq, hp):
    i = pl.program_id(1)
    r = lax.broadcasted_iota(jnp.int32, (tq, tq), 0)
    c = lax.broadcasted_iota(jnp.int32, (tq, tq), 1)
    later = jnp.where(r > c, 1.0, 0.0).astype(BF16)
    qs = [(q_ref[:, h * HEAD_DIM:(h + 1) * HEAD_DIM] * (SCALE * LOG2E)).astype(BF16) for h in range(hp)]

    def tile(h, k0, tail, acc, diagonal):
        k = k_ref[pl.ds(k0, tq), h * HEAD_DIM:(h + 1) * HEAD_DIM]
        v = v_ref[pl.ds(k0, tq), h * HEAD_DIM:(h + 1) * HEAD_DIM]
        z = _nt_dot(qs[h], k)
        sp = jnp.maximum(z, 0.0) + jnp.log2(1.0 + jnp.exp2(-jnp.abs(z)))
        log_keep = jnp.where(r > c, -sp, 0.0) if diagonal else -sp
        hi = log_keep.astype(BF16)
        lo = (log_keep - hi.astype(F32)).astype(BF16)
        after = (jnp.dot(hi, later, preferred_element_type=F32)
                 + jnp.dot(lo, later, preferred_element_type=F32) + tail)
        a = jnp.exp2(z - sp + after)
        if diagonal:
            a = jnp.where(r > c, a, 0.0)
        acc = acc + jnp.dot(a.astype(BF16), v, preferred_element_type=F32)
        return tail + jnp.sum(log_keep, axis=1, keepdims=True), acc

    k_diag = pl.multiple_of(i * tq, tq)
    state = [tile(h, k_diag, jnp.zeros((tq, 1), F32), jnp.zeros((tq, HEAD_DIM), F32), True) for h in range(hp)]

    def live(st):
        return jnp.logical_and(st[0] <= i, st[1] > 0)

    def body(st):
        k0 = pl.multiple_of((i - st[0]) * tq, tq)
        new = [tile(h, k0, st[2 + 2 * h], st[3 + 2 * h], False) for h in range(hp)]
        top = functools.reduce(jnp.maximum, [jnp.max(tl) for tl, _ in new])
        return (st[0] + 1, (top > STICK_DEAD).astype(jnp.int32)) + tuple(x for pair in new for x in pair)

    st = lax.while_loop(live, body, (jnp.int32(1), jnp.int32(1)) + tuple(x for pair in state for x in pair))
    for h in range(hp):
        o_ref[:, h * HEAD_DIM:(h + 1) * HEAD_DIM] = st[3 + 2 * h].astype(o_ref.dtype)


def _stick_attn(pb, kvb, tq=256, hp=2):
    t = pb.shape[0]
    tq = min(tq, t)
    hw = hp * HEAD_DIM
    return pl.pallas_call(
        functools.partial(_stick_kernel, tq=tq, hp=hp),
        out_shape=jax.ShapeDtypeStruct((t, Q_W), BF16),
        grid=(N_MIX_HEADS // hp, t // tq),
        in_specs=[pl.BlockSpec((tq, hw), lambda h, i: (i, h)),
                  pl.BlockSpec((t, hw), lambda h, i: (0, h)),
                  pl.BlockSpec((t, hw), lambda h, i: (0, N_MIX_HEADS // hp + h))],
        out_specs=pl.BlockSpec((tq, hw), lambda h, i: (i, h)),
        compiler_params=_cparams(("parallel", "parallel")),
        name="stick_breaking",
    )(pb, kvb, kvb)


def _router_kernel(x_ref, g_ref, w_ref, b_ref, c_ref):
    h = _rms(x_ref[...], g_ref[...])
    logits = jnp.dot(h, w_ref[...], preferred_element_type=F32, precision=lax.Precision.HIGHEST)
    lane = lax.broadcasted_iota(jnp.int32, logits.shape, 1).astype(F32)
    biased = jnp.where(lane < N_EXPERTS, logits + b_ref[...], -3e38)
    m1 = jnp.max(biased, axis=1, keepdims=True)
    i1 = jnp.min(jnp.where(biased == m1, lane, float(LANES)), axis=1, keepdims=True)
    rest = jnp.where(lane == i1, -3e38, biased)
    m2 = jnp.max(rest, axis=1, keepdims=True)
    i2 = jnp.min(jnp.where(rest == m2, lane, float(LANES)), axis=1, keepdims=True)
    l1 = jnp.sum(jnp.where(lane == i1, logits, 0.0), axis=1, keepdims=True)
    l2 = jnp.sum(jnp.where(lane == i2, logits, 0.0), axis=1, keepdims=True)
    mx = jnp.maximum(l1, l2)
    e1 = jnp.exp(l1 - mx)
    e2 = jnp.exp(l2 - mx)
    inv = 1.0 / (e1 + e2)
    c_ref[...] = (jnp.where(lane == 0.0, i1, 0.0) + jnp.where(lane == 1.0, i2, 0.0)
                  + jnp.where(lane == 2.0, e1 * inv, 0.0) + jnp.where(lane == 3.0, e2 * inv, 0.0))


def _router(x, gain, router_w, router_b, tm=512):
    t, d = x.shape
    tm = min(tm, t)
    w = jnp.zeros((d, LANES), F32).at[:, :N_EXPERTS].set(router_w)
    b = jnp.zeros((1, LANES), F32).at[0, :N_EXPERTS].set(router_b)
    return pl.pallas_call(
        _router_kernel,
        out_shape=jax.ShapeDtypeStruct((t, LANES), F32),
        grid=(t // tm,),
        in_specs=[pl.BlockSpec((tm, d), lambda i: (i, 0)), pl.BlockSpec((1, d), lambda i: (0, 0)),
                  pl.BlockSpec((d, LANES), lambda i: (0, 0)), pl.BlockSpec((1, LANES), lambda i: (0, 0))],
        out_specs=pl.BlockSpec((tm, LANES), lambda i: (i, 0)),
        compiler_params=_cparams(("parallel",)),
        name="moe_router",
    )(x, gain.reshape(1, d), w, b)


def _route_plan(rt, sub, per_group):
    t = rt.shape[0]
    i32 = jnp.int32
    e_flat = rt[:, 0:2].astype(i32).reshape(-1)
    onehot = (e_flat[:, None] == jnp.arange(N_EXPERTS, dtype=i32)[None, :]).astype(i32)
    csum = jnp.cumsum(onehot, axis=0)
    rank = jnp.sum(csum * onehot, axis=1) - 1
    tiles_e = (csum[-1] + sub - 1) // sub
    tile_end = jnp.cumsum(tiles_e)
    tile_start = tile_end - tiles_e
    dest = tile_start[e_flat] * sub + rank
    groups_e = (tiles_e + per_group - 1) // per_group
    group_end = jnp.cumsum(groups_e)
    n_groups = group_end[-1]
    max_groups = (2 * t // sub + N_EXPERTS) // per_group + N_EXPERTS
    gidx = jnp.arange(max_groups, dtype=i32)
    gcl = jnp.minimum(gidx, n_groups - 1)
    g_expert = jnp.sum((gcl[:, None] >= group_end[None, :]).astype(i32), axis=1)
    k_in_e = gcl - (group_end - groups_e)[g_expert]
    g_start = (tile_start[g_expert] + k_in_e * per_group) * sub
    g_tiles = jnp.where(gidx < n_groups, jnp.clip(tiles_e[g_expert] - k_in_e * per_group, 0, per_group), 0)
    return (dest.astype(i32), tile_end[-1].reshape(1).astype(i32),
            g_expert.astype(i32), g_start.astype(i32), g_tiles.astype(i32))


def _dispatch_kernel(dest_ref, nt_ref, x_hbm, g_ref, o_ref, tok_ref, xbuf, sems, *, tm):
    n = pl.program_id(0)
    n_tiles = nt_ref[0]

    def gather(tile, slot):
        base = tile * tm

        def issue(r, c):
            pltpu.make_async_copy(x_hbm.at[pl.ds(tok_ref[base + r], 1), :],
                                  xbuf.at[slot, pl.ds(r, 1), :], sems.at[slot]).start()
            return c

        lax.fori_loop(0, tm, issue, 0, unroll=8)

    @pl.when(n == 0)
    def _():
        def clear(r, c):
            tok_ref[r] = 0
            return c

        lax.fori_loop(0, tok_ref.shape[0], clear, 0, unroll=8)

        def fill(q, c):
            rows = [dest_ref[q * 8 + u] for u in range(8)]
            for u in range(8):
                tok_ref[rows[u]] = q * 4 + u // 2
            return c

        lax.fori_loop(0, dest_ref.shape[0] // 8, fill, 0)
        gather(0, 0)

    slot = n % 2

    @pl.when(n + 1 < n_tiles)
    def _():
        gather(n + 1, 1 - slot)

    @pl.when(n < n_tiles)
    def _():
        pltpu.make_async_copy(x_hbm.at[pl.ds(0, tm), :], xbuf.at[slot], sems.at[slot]).wait()
        o_ref[...] = _rms(xbuf[slot], g_ref[...]).astype(o_ref.dtype)

    @pl.when(n >= n_tiles)
    def _():
        o_ref[...] = jnp.zeros(o_ref.shape, o_ref.dtype)


def _dispatch(x, gain, dest, n_tiles, tm, nt):
    t, d = x.shape
    return pl.pallas_call(
        functools.partial(_dispatch_kernel, tm=tm),
        out_shape=jax.ShapeDtypeStruct((nt * tm, d), BF16),
        grid_spec=pltpu.PrefetchScalarGridSpec(
            num_scalar_prefetch=2,
            grid=(nt,),
            in_specs=[pl.BlockSpec(memory_space=pl.ANY), pl.BlockSpec((1, d), lambda n, dst, ntl: (0, 0))],
            out_specs=pl.BlockSpec((tm, d), lambda n, dst, ntl: (n, 0)),
            scratch_shapes=[pltpu.SMEM((nt * tm,), jnp.int32), pltpu.VMEM((2, tm, d), F32),
                            pltpu.SemaphoreType.DMA((2,))]),
        compiler_params=_cparams(("arbitrary",)),
        name="moe_dispatch",
    )(dest, n_tiles, x, gain.reshape(1, d))


def _experts_kernel(ge_ref, gs_ref, gn_ref, nt_ref, xg_hbm, wg_ref, wu_ref, wd_ref, y_hbm,
                    xbuf, yacc, wgc, wuc, wdc, sem_in, sem_out, *, sub):
    s = pl.program_id(0)
    j = pl.program_id(1)
    n_sub = gn_ref[s]
    start = pl.multiple_of(gs_ref[s], sub)

    @pl.when(jnp.logical_and(s == 0, j == 0))
    def _():
        yacc[0:sub, :] = jnp.zeros((sub, yacc.shape[1]), F32)
        first, last = nt_ref[0], y_hbm.shape[0] // sub

        def put(r, c):
            pltpu.make_async_copy(yacc.at[pl.ds(0, sub), :],
                                  y_hbm.at[pl.ds(pl.multiple_of(r * sub, sub), sub), :], sem_out).start()
            return c

        def done(r, c):
            pltpu.make_async_copy(yacc.at[pl.ds(0, sub), :], y_hbm.at[pl.ds(0, sub), :], sem_out).wait()
            return c

        lax.fori_loop(first, last, put, 0)
        lax.fori_loop(first, last, done, 0)

    @pl.when(n_sub > 0)
    def _():
        @pl.when(j == 0)
        def _():
            def fetch(r, c):
                r0 = pl.multiple_of(r * sub, sub)
                pltpu.make_async_copy(xg_hbm.at[pl.ds(start + r0, sub), :], xbuf.at[pl.ds(r0, sub), :],
                                      sem_in).start()
                return c

            def fetched(r, c):
                pltpu.make_async_copy(xg_hbm.at[pl.ds(0, sub), :], xbuf.at[pl.ds(0, sub), :], sem_in).wait()
                return c

            lax.fori_loop(0, n_sub, fetch, 0)
            yacc[...] = jnp.zeros(yacc.shape, F32)
            lax.fori_loop(0, n_sub, fetched, 0)

        wgc[...] = wg_ref[0].astype(BF16)
        wuc[...] = wu_ref[0].astype(BF16)
        wdc[...] = wd_ref[0].astype(BF16)

        def rows_at(r0, m):
            x = xbuf[pl.ds(r0, m), :]
            g = jnp.dot(x, wgc[...], preferred_element_type=F32)
            u = jnp.dot(x, wuc[...], preferred_element_type=F32)
            a = (g * _sigmoid(g) * u).astype(BF16)
            yacc[pl.ds(r0, m), :] += jnp.dot(a, wdc[...], preferred_element_type=F32)

        def tile_pair(r, c):
            rows_at(pl.multiple_of(r * 2 * sub, 2 * sub), 2 * sub)
            return c

        lax.fori_loop(0, n_sub // 2, tile_pair, 0)

        @pl.when(n_sub % 2 == 1)
        def _():
            rows_at(pl.multiple_of((n_sub - 1) * sub, sub), sub)

        @pl.when(j == pl.num_programs(1) - 1)
        def _():
            def put(r, c):
                r0 = pl.multiple_of(r * sub, sub)
                pltpu.make_async_copy(yacc.at[pl.ds(r0, sub), :], y_hbm.at[pl.ds(start + r0, sub), :],
                                      sem_out).start()
                return c

            def done(r, c):
                pltpu.make_async_copy(yacc.at[pl.ds(0, sub), :], y_hbm.at[pl.ds(0, sub), :], sem_out).wait()
                return c

            lax.fori_loop(0, n_sub, put, 0)
            lax.fori_loop(0, n_sub, done, 0)


def _moe_experts(xg, w_gu, w_down, g_expert, g_start, g_tiles, n_tiles, *, sub, per_group, tf=256):
    p, d = xg.shape
    f = w_down.shape[1]
    nj = f // tf
    rows = sub * per_group
    n_groups = g_expert.shape[0]

    def jcl(s, j, gn):
        return jnp.where(gn[s] > 0, j, nj - 1)

    return pl.pallas_call(
        functools.partial(_experts_kernel, sub=sub),
        out_shape=jax.ShapeDtypeStruct((p, d), F32),
        grid_spec=pltpu.PrefetchScalarGridSpec(
            num_scalar_prefetch=4,
            grid=(n_groups, nj),
            in_specs=[pl.BlockSpec(memory_space=pl.ANY),
                      pl.BlockSpec((1, d, tf), lambda s, j, ge, gs, gn, nt: (ge[s], 0, jcl(s, j, gn))),
                      pl.BlockSpec((1, d, tf), lambda s, j, ge, gs, gn, nt: (ge[s], 0, jcl(s, j, gn) + nj)),
                      pl.BlockSpec((1, tf, d), lambda s, j, ge, gs, gn, nt: (ge[s], jcl(s, j, gn), 0))],
            out_specs=pl.BlockSpec(memory_space=pl.ANY),
            scratch_shapes=[pltpu.VMEM((rows, d), BF16), pltpu.VMEM((rows, d), F32),
                            pltpu.VMEM((d, tf), BF16), pltpu.VMEM((d, tf), BF16), pltpu.VMEM((tf, d), BF16),
                            pltpu.SemaphoreType.DMA(()), pltpu.SemaphoreType.DMA(())]),
        compiler_params=pltpu.CompilerParams(dimension_semantics=("arbitrary", "arbitrary"),
                                             vmem_limit_bytes=MOE_VMEM_LIMIT),
        name="moe_experts",
    )(g_expert, g_start, g_tiles, n_tiles, xg, w_gu, w_gu, w_down)


def _moe_combine_kernel(dest_ref, x_ref, rt_ref, y_hbm, o_ref, ybuf, sems, *, tm):
    i = pl.program_id(0)

    def gather(tile, slot):
        base = tile * tm

        def issue(r, c):
            for k in range(2):
                pltpu.make_async_copy(y_hbm.at[pl.ds(dest_ref[2 * (base + r) + k], 1), :],
                                      ybuf.at[slot, pl.ds(k * tm + r, 1), :], sems.at[slot]).start()
            return c

        lax.fori_loop(0, tm, issue, 0, unroll=4)

    @pl.when(i == 0)
    def _():
        gather(0, 0)

    slot = i % 2

    @pl.when(i + 1 < pl.num_programs(0))
    def _():
        gather(i + 1, 1 - slot)

    pltpu.make_async_copy(y_hbm.at[pl.ds(0, 2 * tm), :], ybuf.at[slot], sems.at[slot]).wait()
    w = rt_ref[...]
    o_ref[...] = (x_ref[...] + w[:, 2:3] * ybuf[slot, 0:tm, :] + w[:, 3:4] * ybuf[slot, tm:2 * tm, :])


def _moe_combine(x, rt, y, dest, tm=512):
    t, d = x.shape
    tm = min(tm, t)
    return pl.pallas_call(
        functools.partial(_moe_combine_kernel, tm=tm),
        out_shape=jax.ShapeDtypeStruct((t, d), F32),
        grid_spec=pltpu.PrefetchScalarGridSpec(
            num_scalar_prefetch=1,
            grid=(t // tm,),
            in_specs=[pl.BlockSpec((tm, d), lambda i, dst: (i, 0)),
                      pl.BlockSpec((tm, LANES), lambda i, dst: (i, 0)),
                      pl.BlockSpec(memory_space=pl.ANY)],
            out_specs=pl.BlockSpec((tm, d), lambda i, dst: (i, 0)),
            scratch_shapes=[pltpu.VMEM((2, 2 * tm, d), F32), pltpu.SemaphoreType.DMA((2,))]),
        compiler_params=_cparams(("arbitrary",)),
        name="moe_combine",
    )(dest, x, rt, y)


def _moe_block(xs, gain, router_w, router_b, w_gu, w_down):
    t = xs.shape[0]
    sub = min(MOE_SUB, t)
    nt = 2 * t // sub + N_EXPERTS
    rt = _router(xs, gain, router_w, router_b)
    dest, n_tiles, g_expert, g_start, g_tiles = _route_plan(rt, sub, MOE_TILES_PER_GROUP)
    xg = _dispatch(xs, gain, dest, (n_tiles + 1) // 2, 2 * sub, nt // 2)
    y = _moe_experts(xg, w_gu, w_down, g_expert, g_start, g_tiles, n_tiles,
                     sub=sub, per_group=MOE_TILES_PER_GROUP)
    return _moe_combine(xs, rt, y, dest)


def _alibi_slopes():
    slopes = np.array([2.0 ** (-8.0 * (i + 1) / N_MIX_HEADS) for i in range(N_MIX_HEADS)], dtype=np.float32)
    return jnp.asarray((slopes.astype(np.float64) * LOG2E).astype(np.float32))


def _importance_map(ncp, nb):
    ratio = SLC_BLOCK // CMP_STRIDE
    c = np.arange(ncp)[:, None]
    b = np.arange(nb)[None, :]
    m = (c >= ratio * b - 1) & (c <= ratio * b + ratio - 1) & (c < ncp - 1)
    return jnp.asarray(m.T.astype(np.float32), dtype=BF16)


def _nsa_mix(p, slopes, q_gain, k_gain, cmp_w, cmp_pos, slc_tq=256):
    t = p.shape[0]
    qkv = _nsa_prep(p, q_gain, k_gain)
    kvc = _compress(p, cmp_w, cmp_pos, k_gain[0])
    imap = _importance_map(t // CMP_STRIDE, t // SLC_BLOCK)
    o_cmp, score = _cmp_attn(slopes, qkv, kvc, imap)
    sel, blk_act = _topk_select(score, sub=slc_tq)
    o_win = _win_attn(slopes, qkv)
    gates = p[:, Q_W + 6 * KV_W:Q_W + 6 * KV_W + GATE_W]
    return _slc_attn(slopes, qkv, sel, blk_act[:, :, 0, :], o_cmp, o_win, gates, tq=slc_tq)


def kernel(x, mem, attn_norm, mem_norm, ffn_norm, w_in_a, cmp_w, cmp_pos, nsa_q_norm, nsa_k_norm, w_in_b,
           kv_norm_b, w_kv_b, w_mem_kv, mem_qk_norm, w_out, ffn_w_gu, ffn_w_down, router_w, router_b,
           moe_w_gu, moe_w_down):
    b, t, d = x.shape
    assert b == 1
    xs = x[0]
    mems = mem[0]
    slopes = _alibi_slopes()

    p = _norm_matmul(xs, attn_norm[0], w_in_a[0], tm=1024, tn=896, name="in_proj_a")
    mix = _nsa_mix(p, slopes, nsa_q_norm[0], nsa_k_norm[0], cmp_w[0], cmp_pos[0])
    memkv = _norm_matmul(mems, mem_norm[0], w_mem_kv[0], tm=256, tn=512, name="mem_kv")
    qm = p[:, Q_W + 6 * KV_W + GATE_W:]
    mo = _mem_attn(qm, 0, memkv, mem_qk_norm[0])
    xs = _out_proj(mix, mo, w_out[0], xs)
    act = _swiglu_up(xs, ffn_norm[0], ffn_w_gu[0])
    xs = _matmul(act, ffn_w_down[0], xs, tm=1024, tn=1024, tk=1024, name="ffn_down")

    kvb = _norm_matmul(xs, kv_norm_b, w_kv_b, tm=1024, tn=1024, out_dtype=BF16, name="kv_proj_b")
    pb = _norm_matmul(xs, attn_norm[1], w_in_b[0], tm=1024, tn=1024, name="in_proj_b")
    mix = _stick_attn(pb, kvb)
    memkv = _norm_matmul(mems, mem_norm[1], w_mem_kv[1], tm=256, tn=512, name="mem_kv")
    mo = _mem_attn(pb, N_MIX_HEADS, memkv, mem_qk_norm[1])
    xs = _out_proj(mix, mo, w_out[1], xs)
    xs = _moe_block(xs, ffn_norm[1], router_w[0], router_b[0], moe_w_gu[0], moe_w_down[0])
    return xs[None]
```
